```python
import math
import jax
import jax.numpy as jnp
from jax import lax
import numpy as np


D_MODEL = 2048
BATCH = 4
SEQ = 2048
DEPTH = 4

GRID_W = 64
CTX_LEN = 256
RET_HEADS = 8
RET_QK_DIM = 64
RET_V_DIM = 128
DIFF_HEADS = 8
DIFF_QK_DIM = 64
DIFF_V_DIM = 128
RET_WIDTH = RET_HEADS * RET_V_DIM
DIFF_WIDTH = DIFF_HEADS * DIFF_V_DIM
MIX_WIDTH = RET_WIDTH + DIFF_WIDTH
PROJ_SPLITS = (RET_HEADS * RET_QK_DIM, RET_HEADS * RET_QK_DIM, RET_WIDTH, RET_WIDTH,
               DIFF_HEADS * 2 * DIFF_QK_DIM, DIFF_HEADS * 2 * DIFF_QK_DIM, DIFF_WIDTH)
PROJ_WIDTH = 512 + 512 + 1024 + 1024 + 1024 + 1024 + 1024
CHUNK = 128
QBLOCK = 128
N_EXPERTS = 16
EXPERT_FF = 1024
CAPACITY_FACTOR = 2
ROPE_BASE = 10000.0
EPS = 1e-6
N_MOD = 6

kernel_name = 'hybrid_retention_diffattn_ec_moe_dit'


def rmsnorm(x, gain):
    x32 = x.astype(jnp.float32)
    y = x32 * lax.rsqrt(jnp.mean(x32 * x32, axis=-1, keepdims=True) + EPS)
    return (y * gain.astype(jnp.float32)).astype(x.dtype)


def modulate(h, shift, scale):
    return h * (1.0 + scale) + shift


def head_groupnorm(y, gain):
    y32 = y.astype(jnp.float32)
    yc = y32 - jnp.mean(y32, axis=-1, keepdims=True)
    out = yc * lax.rsqrt(jnp.mean(yc * yc, axis=-1, keepdims=True) + EPS)
    b, l = y.shape[:2]
    return (out.reshape(b, l, -1) * gain.astype(jnp.float32)).astype(y.dtype)


def head_rmsnorm(y, gain):
    y32 = y.astype(jnp.float32)
    out = y32 * lax.rsqrt(jnp.mean(y32 * y32, axis=-1, keepdims=True) + EPS)
    b, l = y.shape[:2]
    return (out.reshape(b, l, -1) * gain.astype(jnp.float32)).astype(y.dtype)


def axial_rope_tables(rows, head_dim):
    n_freq = head_dim // 4
    freqs = ROPE_BASE ** (-jnp.arange(n_freq, dtype=jnp.float32) / n_freq)
    row = jnp.repeat(jnp.arange(rows, dtype=jnp.float32), GRID_W)
    col = jnp.tile(jnp.arange(GRID_W, dtype=jnp.float32), rows)
    ang = jnp.concatenate([row[:, None] * freqs, col[:, None] * freqs], axis=-1)
    return jnp.cos(ang), jnp.sin(ang)


def apply_rope(x, cos, sin):
    half = x.shape[-1] // 2
    shape = (1, cos.shape[0]) + (1,) * (x.ndim - 3) + (half,)
    cs = cos.reshape(shape).astype(x.dtype)
    sn = sin.reshape(shape).astype(x.dtype)
    x1, x2 = x[..., :half], x[..., half:]
    return jnp.concatenate([x1 * cs - x2 * sn, x1 * sn + x2 * cs], axis=-1)


def split_projection(p):
    b, l, _ = p.shape
    pieces = []
    start = 0
    for w in PROJ_SPLITS:
        pieces.append(p[..., start:start + w])
        start += w
    rq, rk, rv, rg, dq, dk, dv = pieces
    return (rq.reshape(b, l, RET_HEADS, RET_QK_DIM), rk.reshape(b, l, RET_HEADS, RET_QK_DIM),
            rv.reshape(b, l, RET_HEADS, RET_V_DIM), rg,
            dq.reshape(b, l, DIFF_HEADS, 2, DIFF_QK_DIM), dk.reshape(b, l, DIFF_HEADS, 2, DIFF_QK_DIM),
            dv.reshape(b, l, DIFF_HEADS, DIFF_V_DIM))


def retention_chunked(q, k, v, log_gamma, init_state, strict):
    b, h, l, dk = q.shape
    dv = v.shape[-1]
    n = l // CHUNK
    idx = jnp.arange(CHUNK, dtype=jnp.float32)
    diff = idx[:, None] - idx[None, :]
    mask = (diff > 0) if strict else (diff >= 0)
    lg = log_gamma.astype(jnp.float32)
    decay_in = jnp.where(mask[None], jnp.exp(lg[:, None, None] * jnp.where(mask, diff, 0.0)[None]), 0.0).astype(q.dtype)
    q_decay = jnp.exp(lg[:, None] * (idx + 1.0))[None, :, :, None].astype(q.dtype)
    k_decay = jnp.exp(lg[:, None] * (CHUNK - 1.0 - idx))[None, :, :, None].astype(q.dtype)
    chunk_decay = jnp.exp(lg * CHUNK)[None, :, None, None].astype(q.dtype)
    to_chunks = lambda t: t.reshape(b, h, n, CHUNK, t.shape[-1]).transpose(2, 0, 1, 3, 4)

    def step(state, inp):
        qi, ki, vi = inp
        scores = jnp.einsum('bhid,bhjd->bhij', qi, ki) * decay_in
        o = jnp.einsum('bhij,bhjv->bhiv', scores, vi) + jnp.einsum('bhid,bhdv->bhiv', qi * q_decay, state)
        state = state * chunk_decay + jnp.einsum('bhjd,bhjv->bhdv', ki * k_decay, vi)
        return state, o

    _, out = lax.scan(step, init_state, (to_chunks(q), to_chunks(k), to_chunks(v)))
    return out.transpose(1, 2, 0, 3, 4).reshape(b, h, l, dv)


def bidirectional_retention(q, k, v, log_g, init_f, init_b):
    fwd = retention_chunked(q, k, v, log_g[0], init_f, False)
    flip = lambda t: jnp.flip(t, axis=2)
    bwd = flip(retention_chunked(flip(q), flip(k), flip(v), log_g[1], init_b, True))
    return fwd + bwd


def retention_context_states(k, v, log_g):
    lc = k.shape[2]
    pos = jnp.arange(lc, dtype=jnp.float32)
    w_f = jnp.exp(log_g[0][:, None] * (lc - 1.0 - pos)).astype(k.dtype)
    w_b = jnp.exp(log_g[1][:, None] * pos).astype(k.dtype)
    s_f = jnp.einsum('bhld,bhlv,hl->bhdv', k, v, w_f)
    s_b = jnp.einsum('bhld,bhlv,hl->bhdv', k, v, w_b)
    return s_f, s_b


def diff_attention(q, keys, vals, lam):
    b, h, _, l, dk = q.shape
    nb = l // QBLOCK
    qb = q.reshape(b, h, 2, nb, QBLOCK, dk).transpose(3, 0, 1, 2, 4, 5)

    def block(qi):
        s = jnp.einsum('bhtqd,bhtkd->bhtqk', qi, keys).astype(jnp.float32)
        p = jax.nn.softmax(s, axis=-1)
        a = p[:, :, 0] - lam * p[:, :, 1]
        return jnp.einsum('bhqk,bhkv->bhqv', a.astype(vals.dtype), vals)

    out = lax.map(block, qb)
    return out.transpose(1, 0, 3, 2, 4).reshape(b, l, h, vals.shape[-1])


def merge_heads(ret, gate, diff, ret_gain, diff_gain, lam_init, w_out_l):
    ret_out = jax.nn.silu(gate) * head_groupnorm(ret.transpose(0, 2, 1, 3), ret_gain)
    diff_out = head_rmsnorm(diff, diff_gain) * (1.0 - lam_init)
    return jnp.concatenate([ret_out, diff_out], axis=-1) @ w_out_l


def expert_choice_ffn(h, w_router_l, w_gate_l, w_up_l, w_down_l):
    b, l, _ = h.shape
    cap = CAPACITY_FACTOR * l // N_EXPERTS
    aff = jax.nn.softmax((h @ w_router_l).astype(jnp.float32), axis=-1)
    gates, idx = lax.top_k(jnp.swapaxes(aff, 1, 2), cap)
    bidx = jnp.arange(b)[:, None, None]
    xs = h[bidx, idx]
    hid = jax.nn.silu(jnp.einsum('becd,edf->becf', xs, w_gate_l)) * jnp.einsum('becd,edf->becf', xs, w_up_l)
    ys = jnp.einsum('becf,efd->becd', hid, w_down_l) * gates[..., None].astype(h.dtype)
    return jnp.zeros_like(h).at[bidx, idx].add(ys)


def setup_inputs(seed: int = 0) -> dict:
    key = jax.random.key(seed)
    ks = jax.random.split(key, 20)
    f32 = jnp.float32
    nrm = lambda k, s: jax.random.normal(k, s, dtype=f32)
    base_decay = -jnp.log(2.0) * (5.0 + jnp.arange(RET_HEADS, dtype=f32))
    return {
        'x': nrm(ks[0], (BATCH, SEQ, D_MODEL)),
        'c': nrm(ks[1], (BATCH, D_MODEL)),
        'ctx': nrm(ks[2], (BATCH, CTX_LEN, D_MODEL)),
        'c_ctx': nrm(ks[3], (D_MODEL,)),
        'w_ada': nrm(ks[4], (DEPTH, D_MODEL, N_MOD * D_MODEL)) * (0.5 * D_MODEL ** -0.5),
        'b_ada': nrm(ks[5], (DEPTH, N_MOD * D_MODEL)) * 0.02,
        'norm_mix': 1.0 + 0.02 * nrm(ks[6], (DEPTH, D_MODEL)),
        'norm_ffn': 1.0 + 0.02 * nrm(ks[7], (DEPTH, D_MODEL)),
        'w_in': nrm(ks[8], (DEPTH, D_MODEL, PROJ_WIDTH)) * D_MODEL ** -0.5,
        'w_out': nrm(ks[9], (DEPTH, MIX_WIDTH, D_MODEL)) * MIX_WIDTH ** -0.5,
        'ret_log_decay': base_decay[None, None, :] + 0.1 * nrm(ks[10], (DEPTH, 2, RET_HEADS)),
        'ret_norm': 1.0 + 0.02 * nrm(ks[11], (DEPTH, RET_WIDTH)),
        'diff_lambda': 0.1 * nrm(ks[12], (DEPTH, 4, DIFF_QK_DIM)),
        'diff_norm': 1.0 + 0.02 * nrm(ks[13], (DEPTH, DIFF_WIDTH)),
        'w_router': nrm(ks[14], (DEPTH, D_MODEL, N_EXPERTS)) * D_MODEL ** -0.5,
        'w_gate': nrm(ks[15], (DEPTH, N_EXPERTS, D_MODEL, EXPERT_FF)) * D_MODEL ** -0.5,
        'w_up': nrm(ks[16], (DEPTH, N_EXPERTS, D_MODEL, EXPERT_FF)) * D_MODEL ** -0.5,
        'w_down': nrm(ks[17], (DEPTH, N_EXPERTS, EXPERT_FF, D_MODEL)) * EXPERT_FF ** -0.5,
        'norm_final': 1.0 + 0.02 * nrm(ks[18], (D_MODEL,)),
    }


def reference(x, c, ctx, c_ctx, w_ada, b_ada, norm_mix, norm_ffn, w_in, w_out, ret_log_decay,
              ret_norm, diff_lambda, diff_norm, w_router, w_gate, w_up, w_down, norm_final):
    rows = x.shape[1] // GRID_W
    cos, sin = axial_rope_tables(rows, RET_QK_DIM)
    silu_c = jax.nn.silu(c)
    silu_cc = jax.nn.silu(c_ctx)
    r_scale = RET_QK_DIM ** -0.5
    d_scale = DIFF_QK_DIM ** -0.5
    to_bhld = lambda t: t.transpose(0, 2, 1, 3)
    to_bhtld = lambda t: t.transpose(0, 2, 3, 1, 4)
    for layer in range(DEPTH):
        last = layer == DEPTH - 1
        mod_x = jnp.split((silu_c @ w_ada[layer] + b_ada[layer])[:, None, :], N_MOD, axis=-1)
        mod_c = jnp.split(silu_cc @ w_ada[layer] + b_ada[layer], N_MOD, axis=-1)
        hx = modulate(rmsnorm(x, norm_mix[layer]), mod_x[0], mod_x[1])
        hc = modulate(rmsnorm(ctx, norm_mix[layer]), mod_c[0], mod_c[1])
        rq_x, rk_x, rv_x, rg_x, dq_x, dk_x, dv_x = split_projection(hx @ w_in[layer])
        rq_c, rk_c, rv_c, rg_c, dq_c, dk_c, dv_c = split_projection(hc @ w_in[layer])

        log_g = jnp.log1p(-jnp.exp(ret_log_decay[layer].astype(jnp.float32)))
        q_c, k_c, v_c = to_bhld(rq_c * r_scale), to_bhld(rk_c), to_bhld(rv_c)
        state_f, state_b = retention_context_states(k_c, v_c, log_g)
        ret_x = bidirectional_retention(to_bhld(apply_rope(rq_x, cos, sin) * r_scale),
                                        to_bhld(apply_rope(rk_x, cos, sin)), to_bhld(rv_x),
                                        log_g, state_f, state_b)

        lam_init = 0.8 - 0.6 * math.exp(-0.3 * layer)
        lv = diff_lambda[layer].astype(jnp.float32)
        lam = jnp.exp(jnp.sum(lv[0] * lv[1])) - jnp.exp(jnp.sum(lv[2] * lv[3])) + lam_init
        dk_ct = to_bhtld(dk_c)
        dv_ct = to_bhld(dv_c)
        keys_x = jnp.concatenate([dk_ct, to_bhtld(apply_rope(dk_x, cos, sin))], axis=3)
        vals_x = jnp.concatenate([dv_ct, to_bhld(dv_x)], axis=2)
        diff_x = diff_attention(to_bhtld(apply_rope(dq_x, cos, sin) * d_scale), keys_x, vals_x, lam)

        x_mid = x + mod_x[2] * merge_heads(ret_x, rg_x, diff_x, ret_norm[layer], diff_norm[layer], lam_init, w_out[layer])

        if not last:
            zeros = jnp.zeros(state_f.shape, dtype=state_f.dtype)
            ret_c = bidirectional_retention(q_c, k_c, v_c, log_g, zeros, zeros)
            diff_c = diff_attention(to_bhtld(dq_c * d_scale), dk_ct, dv_ct, lam)
            ctx = ctx + mod_c[2] * merge_heads(ret_c, rg_c, diff_c, ret_norm[layer], diff_norm[layer], lam_init, w_out[layer])
            hc2 = modulate(rmsnorm(ctx, norm_ffn[layer]), mod_c[3], mod_c[4])
            ctx = ctx + mod_c[5] * expert_choice_ffn(hc2, w_router[layer], w_gate[layer], w_up[layer], w_down[layer])

        hx2 = modulate(rmsnorm(x_mid, norm_ffn[layer]), mod_x[3], mod_x[4])
        x = x_mid + mod_x[5] * expert_choice_ffn(hx2, w_router[layer], w_gate[layer], w_up[layer], w_down[layer])
    return rmsnorm(x, norm_final)
```

```python
import functools
import math

import jax
import jax.numpy as jnp
from jax import lax
from jax.experimental import pallas as pl
from jax.experimental.pallas import tpu as pltpu

F32 = jnp.float32
BF16 = jnp.bfloat16
I32 = jnp.int32

EPS = 1e-6
N_MOD = 6
GRID_W = 64
ROPE_BASE = 10000.0
QK_DIM = 64
V_DIM = 128
LANES = 128
CHUNK = 128
CAPACITY_FACTOR = 2
MOD_ROWS = 8
VMEM_LIMIT = 56 * 1024 * 1024


def _params(sem, vmem=VMEM_LIMIT):
    return pltpu.CompilerParams(dimension_semantics=sem, vmem_limit_bytes=vmem)


def _silu(x):
    return x / (1.0 + jnp.exp(-x))


def _dot(a, b):
    return jnp.dot(a, b, preferred_element_type=F32)


def _dot_nt(a, b):
    return lax.dot_general(a, b, (((1,), (1,)), ((), ())), preferred_element_type=F32)


def _ada_kernel(c_ref, w_ref, b_ref, o_ref):
    s = _silu(c_ref[...]).astype(BF16)
    o_ref[0] = _dot(s, w_ref[0].astype(BF16)) + b_ref[0]


def ada_modulation(cc, w_ada, b_ada):
    depth, d, n = w_ada.shape
    tn = 1024 if n % 1024 == 0 else n
    return pl.pallas_call(
        _ada_kernel,
        name="ada_mod",
        grid=(depth, n // tn),
        in_specs=[pl.BlockSpec((MOD_ROWS, d), lambda l, j: (0, 0)),
                  pl.BlockSpec((1, d, tn), lambda l, j: (l, 0, j)),
                  pl.BlockSpec((1, 1, tn), lambda l, j: (l, 0, j))],
        out_specs=pl.BlockSpec((1, MOD_ROWS, tn), lambda l, j: (l, 0, j)),
        out_shape=jax.ShapeDtypeStruct((depth, MOD_ROWS, n), F32),
        compiler_params=_params(("arbitrary", "arbitrary")),
    )(cc, w_ada, b_ada.reshape(depth, 1, n))


def _rmsnorm_mod(x, gain, shift, scale):
    y = x * lax.rsqrt(jnp.mean(x * x, axis=-1, keepdims=True) + EPS) * gain
    return y * (1.0 + scale) + shift


def _swap_halves(a):
    lane = lax.broadcasted_iota(I32, a.shape, 1)
    return jnp.where((lane & 32) == 0, pltpu.roll(a, 96, 1), pltpu.roll(a, 32, 1))


def _inproj_kernel(x_ref, mod_ref, g_ref, w_ref, ct_ref, st_ref, o_ref, h_scr, *, d, tn, kinds):
    j = pl.program_id(1)

    @pl.when(j == 0)
    def _():
        m = mod_ref[0]
        h_scr[...] = _rmsnorm_mod(x_ref[...], g_ref[...], m[:, 0:d], m[:, d:2 * d]).astype(BF16)

    acc = _dot(h_scr[...], w_ref[...])
    for jj in range(len(kinds) * LANES // tn):
        @pl.when(j == jj)
        def _(jj=jj):
            for s in range(tn // LANES):
                rope, scale = kinds[jj * (tn // LANES) + s]
                a = acc[:, s * LANES:(s + 1) * LANES]
                if rope:
                    a = a * ct_ref[...] + _swap_halves(a) * st_ref[...]
                if scale != 1.0:
                    a = a * scale
                o_ref[:, s * LANES:(s + 1) * LANES] = a.astype(BF16)


def in_projection(xa, mod, gain, w_bf, ct, st, n_x_tiles, tiles_per_sample, n_samples, kinds, tm):
    r, d = xa.shape
    n = w_bf.shape[1]
    tn = 1024 if n % 1024 == 0 else 512
    mod_idx = lambda i, j: (jnp.where(i < n_x_tiles, i // tiles_per_sample, n_samples), 0, 0)
    return pl.pallas_call(
        functools.partial(_inproj_kernel, d=d, tn=tn, kinds=kinds),
        name="in_proj",
        grid=(r // tm, n // tn),
        in_specs=[pl.BlockSpec((tm, d), lambda i, j: (i, 0)),
                  pl.BlockSpec((1, 1, N_MOD * d), mod_idx),
                  pl.BlockSpec((1, d), lambda i, j: (0, 0)),
                  pl.BlockSpec((d, tn), lambda i, j: (0, j)),
                  pl.BlockSpec((tm, LANES), lambda i, j: (i, 0)),
                  pl.BlockSpec((tm, LANES), lambda i, j: (i, 0))],
        out_specs=pl.BlockSpec((tm, tn), lambda i, j: (i, j)),
        out_shape=jax.ShapeDtypeStruct((r, n), BF16),
        scratch_shapes=[pltpu.VMEM((tm, d), BF16)],
        compiler_params=_params(("arbitrary", "arbitrary")),
    )(xa, mod, gain, w_bf, ct, st)


def _ret_kernel(lg_ref, qx_ref, kx_ref, vx_ref, gx_ref, qc_ref, kc_ref, vc_ref, gc_ref, gain_ref,
                *rest, lx, lc, ctx_out):
    if ctx_out:
        ox_ref, oc_ref, accx, accc = rest
    else:
        ox_ref, accx = rest
    pair = pl.program_id(1)
    lane = lax.broadcasted_iota(I32, (CHUNK, LANES), 1)
    lane_c = lax.broadcasted_iota(I32, (lc, LANES), 1)
    dif = (lax.broadcasted_iota(I32, (CHUNK, CHUNK), 0) - lax.broadcasted_iota(I32, (CHUNK, CHUNK), 1)).astype(F32)
    pos = lax.broadcasted_iota(I32, (CHUNK, 1), 0).astype(F32)
    cpos = lax.broadcasted_iota(I32, (lc, 1), 0).astype(F32)
    c_len = float(CHUNK)

    for hh in range(2):
        head = 2 * pair + hh
        lgf = lg_ref[0, head]
        lgb = lg_ref[1, head]
        hm = (lane >= QK_DIM * hh) & (lane < QK_DIM * (hh + 1))
        hm_c = (lane_c >= QK_DIM * hh) & (lane_c < QK_DIM * (hh + 1))
        vsl = slice(hh * V_DIM, (hh + 1) * V_DIM)

        dec_f = jnp.where(dif >= 0, jnp.exp(lgf * jnp.maximum(dif, 0.0)), 0.0)
        dec_b = jnp.where(dif < 0, jnp.exp(lgb * jnp.maximum(-dif, 0.0)), 0.0)
        qd_f = jnp.exp(lgf * (pos + 1.0))
        kd_f = jnp.exp(lgf * (c_len - 1.0 - pos))
        cd_f = jnp.exp(lgf * c_len)
        qd_b = jnp.exp(lgb * (c_len - pos))
        kd_b = jnp.exp(lgb * pos)
        cd_b = jnp.exp(lgb * c_len)

        kc = jnp.where(hm_c, kc_ref[...], 0).astype(F32)
        vc = vc_ref[:, vsl]
        s_f = _dot((kc * jnp.exp(lgf * (lc - 1.0 - cpos))).T.astype(BF16), vc)
        s_b = _dot((kc * jnp.exp(lgb * cpos)).T.astype(BF16), vc)

        def scan(q_ref, k_ref, v_ref, acc_ref, n, s0, dec, qd, kd, cd, reverse, first):
            def body(t, s):
                c = (n - 1 - t) if reverse else t
                rows = pl.ds(pl.multiple_of(c * CHUNK, CHUNK), CHUNK)
                q = jnp.where(hm, q_ref[rows, :], 0)
                k = jnp.where(hm, k_ref[rows, :], 0)
                v = v_ref[rows, vsl]
                sc = _dot_nt(q, k) * dec
                o = _dot(sc.astype(BF16), v) + _dot(q, s.astype(BF16)) * qd
                if first:
                    acc_ref[rows, vsl] = o
                else:
                    acc_ref[rows, vsl] += o
                kdm = (k.astype(F32) * kd).T.astype(BF16)
                return s * cd + _dot(kdm, v)
            lax.fori_loop(0, n, body, s0)

        def finish(acc_ref, g_ref, o_ref, n):
            def body(c, carry):
                rows = pl.ds(pl.multiple_of(c * CHUNK, CHUNK), CHUNK)
                y = acc_ref[rows, vsl]
                yc = y - jnp.mean(y, axis=-1, keepdims=True)
                out = yc * lax.rsqrt(jnp.mean(yc * yc, axis=-1, keepdims=True) + EPS) * gain_ref[:, vsl]
                o_ref[rows, vsl] = (_silu(g_ref[rows, vsl].astype(F32)) * out).astype(BF16)
                return carry
            lax.fori_loop(0, n, body, 0)

        scan(qx_ref, kx_ref, vx_ref, accx, lx // CHUNK, s_f, dec_f, qd_f, kd_f, cd_f, False, True)
        scan(qx_ref, kx_ref, vx_ref, accx, lx // CHUNK, s_b, dec_b, qd_b, kd_b, cd_b, True, False)
        finish(accx, gx_ref, ox_ref, lx // CHUNK)
        if ctx_out:
            zero = jnp.zeros((LANES, V_DIM), F32)
            scan(qc_ref, kc_ref, vc_ref, accc, lc // CHUNK, zero, dec_f, qd_f, kd_f, cd_f, False, True)
            scan(qc_ref, kc_ref, vc_ref, accc, lc // CHUNK, zero, dec_b, qd_b, kd_b, cd_b, True, False)
            finish(accc, gc_ref, oc_ref, lc // CHUNK)


def retention(p, lg, gain, n_samples, lx, lc, heads, ctx_out):
    pairs = heads // 2
    half = heads // 2
    cb = n_samples * lx // lc
    w2 = 2 * V_DIM
    in_specs = [
        pl.BlockSpec(memory_space=pltpu.SMEM),
        pl.BlockSpec((lx, LANES), lambda b, h: (b, h)),
        pl.BlockSpec((lx, LANES), lambda b, h: (b, half + h)),
        pl.BlockSpec((lx, w2), lambda b, h: (b, half + h)),
        pl.BlockSpec((lx, w2), lambda b, h: (b, heads + h)),
        pl.BlockSpec((lc, LANES), lambda b, h: (cb + b, h)),
        pl.BlockSpec((lc, LANES), lambda b, h: (cb + b, half + h)),
        pl.BlockSpec((lc, w2), lambda b, h: (cb + b, half + h)),
        pl.BlockSpec((lc, w2), lambda b, h: (cb + b, heads + h)),
        pl.BlockSpec((1, w2), lambda b, h: (0, h)),
    ]
    out_specs = [pl.BlockSpec((lx, w2), lambda b, h: (b, h))]
    out_shape = [jax.ShapeDtypeStruct((n_samples * lx, heads * V_DIM), BF16)]
    scratch = [pltpu.VMEM((lx, w2), F32)]
    if ctx_out:
        out_specs.append(pl.BlockSpec((lc, w2), lambda b, h: (b, h)))
        out_shape.append(jax.ShapeDtypeStruct((n_samples * lc, heads * V_DIM), BF16))
        scratch.append(pltpu.VMEM((lc, w2), F32))
    return pl.pallas_call(
        functools.partial(_ret_kernel, lx=lx, lc=lc, ctx_out=ctx_out),
        name="retention",
        grid=(n_samples, pairs),
        in_specs=in_specs, out_specs=out_specs, out_shape=out_shape, scratch_shapes=scratch,
        compiler_params=_params(("arbitrary", "arbitrary")),
    )(lg, p, p, p, p, p, p, p, p, gain)


def _diff_kernel(lam_ref, q_ref, kc_ref, vc_ref, *rest, lam_init, has_x):
    if has_x:
        kx_ref, vx_ref, gain_ref, o_ref = rest
    else:
        gain_ref, o_ref = rest
    lv = lam_ref[...]
    lam = (jnp.exp(jnp.sum(lv[0:1] * lv[1:2], axis=-1, keepdims=True))
           - jnp.exp(jnp.sum(lv[2:3] * lv[3:4], axis=-1, keepdims=True)) + lam_init)
    q = q_ref[...]
    lane = lax.broadcasted_iota(I32, q.shape, 1)
    parts = []
    for t in range(2):
        qm = jnp.where((lane >= QK_DIM * t) & (lane < QK_DIM * (t + 1)), q, 0)
        sc = _dot_nt(qm, kc_ref[...])
        m = jnp.max(sc, axis=-1, keepdims=True)
        if has_x:
            sx = _dot_nt(qm, kx_ref[...])
            m = jnp.maximum(m, jnp.max(sx, axis=-1, keepdims=True))
        pc = jnp.exp(sc - m)
        l = jnp.sum(pc, axis=-1, keepdims=True)
        px = None
        if has_x:
            px = jnp.exp(sx - m)
            l = l + jnp.sum(px, axis=-1, keepdims=True)
        parts.append((pc, px, 1.0 / l))
    (pc0, px0, r0), (pc1, px1, r1) = parts
    w1 = lam * r1
    out = _dot((pc0 * r0 - pc1 * w1).astype(BF16), vc_ref[...])
    if has_x:
        out = out + _dot((px0 * r0 - px1 * w1).astype(BF16), vx_ref[...])
    y = out * lax.rsqrt(jnp.mean(out * out, axis=-1, keepdims=True) + EPS) * gain_ref[...]
    o_ref[...] = (y * (1.0 - lam_init)).astype(BF16)


def diff_attention(p, lam_vec, gain, n_samples, lx, lc, ret_heads, heads, lam_init, latent, tq):
    qo = 3 * ret_heads
    ko = qo + heads
    vo = ko + heads
    cb = n_samples * lx // lc
    lq = lx if latent else lc
    nq = lq // tq
    q_row = (lambda b, t: b * nq + t) if latent else (lambda b, t: n_samples * lx // tq + b * nq + t)
    in_specs = [
        pl.BlockSpec((4, QK_DIM), lambda b, h, t: (0, 0)),
        pl.BlockSpec((tq, LANES), lambda b, h, t: (q_row(b, t), qo + h)),
        pl.BlockSpec((lc, LANES), lambda b, h, t: (cb + b, ko + h)),
        pl.BlockSpec((lc, LANES), lambda b, h, t: (cb + b, vo + h)),
    ]
    args = [lam_vec, p, p, p]
    if latent:
        in_specs += [pl.BlockSpec((lx, LANES), lambda b, h, t: (b, ko + h)),
                     pl.BlockSpec((lx, LANES), lambda b, h, t: (b, vo + h))]
        args += [p, p]
    in_specs.append(pl.BlockSpec((1, LANES), lambda b, h, t: (0, h)))
    args.append(gain)
    return pl.pallas_call(
        functools.partial(_diff_kernel, lam_init=lam_init, has_x=latent),
        name="diff_attn_x" if latent else "diff_attn_ctx",
        grid=(n_samples, heads, nq),
        in_specs=in_specs,
        out_specs=pl.BlockSpec((tq, LANES), lambda b, h, t: (b * nq + t, h)),
        out_shape=jax.ShapeDtypeStruct((n_samples * lq, heads * V_DIM), BF16),
        compiler_params=_params(("arbitrary", "arbitrary", "arbitrary")),
    )(*args)


def _outproj_kernel(ret_ref, dif_ref, w_ref, x_ref, mod_ref, g_ref, wr_ref, xm_ref, h_ref, aff_ref, *, d, wr):
    y = _dot(ret_ref[...], w_ref[0:wr, :]) + _dot(dif_ref[...], w_ref[wr:, :])
    m = mod_ref[0]
    xm = x_ref[...] + m[:, 2 * d:3 * d] * y
    xm_ref[...] = xm
    h = _rmsnorm_mod(xm, g_ref[...], m[:, 3 * d:4 * d], m[:, 4 * d:5 * d])
    h_ref[...] = h
    logits = lax.dot_general(wr_ref[...], h, (((1,), (1,)), ((), ())),
                             precision=lax.Precision.HIGHEST, preferred_element_type=F32)
    e = jnp.exp(logits - jnp.max(logits, axis=0, keepdims=True))
    aff_ref[...] = e / jnp.sum(e, axis=0, keepdims=True)


def out_projection(ret, dif, w_bf, xa, mod, gain, w_router_t, rows, n_x_tiles, tiles_per_sample, n_samples, tm):
    d = xa.shape[1]
    wr = ret.shape[1]
    e = w_router_t.shape[0]
    mod_idx = lambda i: (jnp.where(i < n_x_tiles, i // tiles_per_sample, n_samples), 0, 0)
    return pl.pallas_call(
        functools.partial(_outproj_kernel, d=d, wr=wr),
        name="out_proj_router",
        grid=(rows // tm,),
        in_specs=[pl.BlockSpec((tm, wr), lambda i: (i, 0)),
                  pl.BlockSpec((tm, dif.shape[1]), lambda i: (i, 0)),
                  pl.BlockSpec(w_bf.shape, lambda i: (0, 0)),
                  pl.BlockSpec((tm, d), lambda i: (i, 0)),
                  pl.BlockSpec((1, 1, N_MOD * d), mod_idx),
                  pl.BlockSpec((1, d), lambda i: (0, 0)),
                  pl.BlockSpec((e, d), lambda i: (0, 0))],
        out_specs=[pl.BlockSpec((tm, d), lambda i: (i, 0)),
                   pl.BlockSpec((tm, d), lambda i: (i, 0)),
                   pl.BlockSpec((e, tm), lambda i: (0, i))],
        out_shape=[jax.ShapeDtypeStruct((rows, d), F32),
                   jax.ShapeDtypeStruct((rows, d), F32),
                   jax.ShapeDtypeStruct((e, rows), F32)],
        compiler_params=_params(("arbitrary",)),
    )(ret, dif, w_bf, xa, mod, gain, w_router_t)


def _ones(mask, dtype):
    return jnp.where(mask, 1.0, 0.0).astype(dtype)


def _prefix_lanes(m):
    upper = _ones(lax.broadcasted_iota(I32, (LANES, LANES), 0) < lax.broadcasted_iota(I32, (LANES, LANES), 1), BF16)
    run = jnp.zeros((m.shape[0], 1), F32)
    outs = []
    for s in range(m.shape[1] // LANES):
        seg = m[:, s * LANES:(s + 1) * LANES]
        outs.append(_dot(seg, upper) + run)
        run = run + jnp.sum(seg.astype(F32), axis=1, keepdims=True)
    return jnp.concatenate(outs, axis=1)


def _topk_kernel(a_ref, idx_ref, gate_ref, dest_ref, start_ref, cnt_ref, posm_scr, dest_scr,
                 *, lg, cap, row_base, ent_base):
    g = pl.program_id(0)
    a = a_ref[...]
    n_e = a.shape[0]
    bits = pltpu.bitcast(a, I32)

    def bisect(_, lohi):
        lo, hi = lohi
        mid = lo + ((hi - lo) >> 1)
        ge = jnp.sum(_ones(bits >= mid, F32), axis=1, keepdims=True) >= cap
        return jnp.where(ge, mid, lo), jnp.where(ge, hi, mid)

    thr, _ = lax.fori_loop(0, 31, bisect,
                           (jnp.zeros((n_e, 1), I32), jnp.full((n_e, 1), 0x7F800000, I32)))
    gt = bits > thr
    eq = bits == thr
    need = cap - jnp.sum(_ones(gt, F32), axis=1, keepdims=True)
    sel = gt | (eq & (_prefix_lanes(_ones(eq, BF16)) < need))
    selb = _ones(sel, BF16)
    pos = _prefix_lanes(selb)
    start_t = jnp.sum(pos, axis=0, keepdims=True)
    lower = _ones(lax.broadcasted_iota(I32, (n_e, n_e), 0) > lax.broadcasted_iota(I32, (n_e, n_e), 1), BF16)
    base = (ent_base + g * (n_e * cap)).astype(F32)
    start_ref[...] = (start_t + base).astype(I32)
    cnt_ref[...] = jnp.sum(_ones(sel, F32), axis=0, keepdims=True).astype(I32)
    posm_scr[...] = jnp.where(sel, pos, -1.0)
    dest_scr[...] = start_t + base + _dot(lower, selb)

    tvals = (row_base + g * lg + lax.broadcasted_iota(I32, (1, lg), 1)).astype(F32)
    slot = lax.broadcasted_iota(I32, (cap, 1), 0).astype(F32)
    col = lax.broadcasted_iota(I32, (cap, n_e), 1)

    def compact(e, carry):
        ia, ga, da = carry
        m = posm_scr[pl.ds(e, 1), :] == slot
        i_e = jnp.sum(jnp.where(m, tvals, 0.0), axis=1, keepdims=True)
        g_e = jnp.sum(jnp.where(m, a_ref[pl.ds(e, 1), :], 0.0), axis=1, keepdims=True)
        d_e = jnp.sum(jnp.where(m, dest_scr[pl.ds(e, 1), :], 0.0), axis=1, keepdims=True)
        here = col == e
        return jnp.where(here, i_e, ia), jnp.where(here, g_e, ga), jnp.where(here, d_e, da)

    z = jnp.zeros((cap, n_e), F32)
    ia, ga, da = lax.fori_loop(0, n_e, compact, (z, z, z))
    idx_ref[0] = ia.astype(I32)
    gate_ref[0] = ga
    dest_ref[0] = da.astype(I32)


def expert_choice(aff_t, groups, lg, first_block, cap, row_base, ent_base):
    e = aff_t.shape[0]
    return pl.pallas_call(
        functools.partial(_topk_kernel, lg=lg, cap=cap, row_base=row_base, ent_base=ent_base),
        name="expert_choice",
        grid=(groups,),
        in_specs=[pl.BlockSpec((e, lg), lambda g: (0, first_block + g))],
        out_specs=[pl.BlockSpec((1, cap, e), lambda g: (g, 0, 0)),
                   pl.BlockSpec((1, cap, e), lambda g: (g, 0, 0)),
                   pl.BlockSpec((1, cap, e), lambda g: (g, 0, 0)),
                   pl.BlockSpec((1, lg), lambda g: (0, g)),
                   pl.BlockSpec((1, lg), lambda g: (0, g))],
        out_shape=[jax.ShapeDtypeStruct((groups, cap, e), I32),
                   jax.ShapeDtypeStruct((groups, cap, e), F32),
                   jax.ShapeDtypeStruct((groups, cap, e), I32),
                   jax.ShapeDtypeStruct((1, groups * lg), I32),
                   jax.ShapeDtypeStruct((1, groups * lg), I32)],
        scratch_shapes=[pltpu.VMEM((e, lg), F32), pltpu.VMEM((e, lg), F32)],
        compiler_params=_params(("arbitrary",)),
    )(aff_t)


DMA_WINDOW = 32


def _rowperm_kernel(si_ref, di_ref, src_ref, dst_ref, sem, *, n):
    def row_copy(i):
        return pltpu.make_async_copy(src_ref.at[pl.ds(si_ref[i], 1)], dst_ref.at[pl.ds(di_ref[i], 1)], sem)

    def issue(i, carry):
        row_copy(i).start()

        @pl.when(i >= DMA_WINDOW)
        def _():
            row_copy(i - DMA_WINDOW).wait()
        return carry

    def drain(i, carry):
        row_copy(i).wait()
        return carry

    lax.fori_loop(0, n, issue, 0)
    lax.fori_loop(n - DMA_WINDOW, n, drain, 0)


def row_permute(src, si, di, n_dst):
    n = si.shape[0]
    assert n == n_dst and n >= DMA_WINDOW
    return pl.pallas_call(
        functools.partial(_rowperm_kernel, n=n),
        name="row_permute",
        grid_spec=pltpu.PrefetchScalarGridSpec(
            num_scalar_prefetch=2, grid=(1,),
            in_specs=[pl.BlockSpec(memory_space=pl.ANY)],
            out_specs=pl.BlockSpec(memory_space=pl.ANY),
            scratch_shapes=[pltpu.SemaphoreType.DMA(())]),
        out_shape=jax.ShapeDtypeStruct((n_dst, src.shape[1]), src.dtype),
        compiler_params=_params(("arbitrary",)),
    )(si, di, src)


def _ffn_kernel(xs_ref, wg_ref, wu_ref, wd_ref, gate_ref, ys_ref, xb_scr, *, ts):
    f = pl.program_id(1)

    @pl.when(f == 0)
    def _():
        xb_scr[...] = xs_ref[0].astype(BF16)

    wg = wg_ref[0].astype(BF16)
    wu = wu_ref[0].astype(BF16)
    wd = wd_ref[0].astype(BF16)
    for r in range(xb_scr.shape[0] // ts):
        rows = slice(r * ts, (r + 1) * ts)
        xb = xb_scr[rows, :]
        hid = _silu(_dot(xb, wg)) * _dot(xb, wu)
        y = _dot(hid.astype(BF16), wd)

        @pl.when(f == 0)
        def _():
            ys_ref[0, rows, :] = y

        @pl.when(f > 0)
        def _():
            ys_ref[0, rows, :] += y

    @pl.when(f == pl.num_programs(1) - 1)
    def _():
        ys_ref[0] = ys_ref[0] * gate_ref[0]


def expert_ffn(xs, w_gate, w_up, w_down, gates):
    e, s, d = xs.shape
    ff = w_gate.shape[2]
    tf = 256 if ff % 256 == 0 else ff
    once = pl.Buffered(1)
    ts = max(t for t in range(8, 577, 8) if s % t == 0)
    return pl.pallas_call(
        functools.partial(_ffn_kernel, ts=ts),
        name="expert_ffn",
        grid=(e, ff // tf),
        in_specs=[pl.BlockSpec((1, s, d), lambda i, f: (i, 0, 0), pipeline_mode=once),
                  pl.BlockSpec((1, d, tf), lambda i, f: (i, 0, f)),
                  pl.BlockSpec((1, d, tf), lambda i, f: (i, 0, f)),
                  pl.BlockSpec((1, tf, d), lambda i, f: (i, f, 0)),
                  pl.BlockSpec((1, s, 1), lambda i, f: (i, 0, 0))],
        out_specs=pl.BlockSpec((1, s, d), lambda i, f: (i, 0, 0), pipeline_mode=once),
        out_shape=jax.ShapeDtypeStruct((e, s, d), F32),
        scratch_shapes=[pltpu.VMEM((s, d), BF16)],
        compiler_params=_params(("arbitrary", "arbitrary")),
    )(xs, w_gate, w_up, w_down, gates)


COMB_ROWS = 128
COMB_ENT = 256


def _combine_kernel(kb_ref, ch_ref, flag_ref, ys_ref, xm_ref, start_ref, cnt_ref, mod_ref, *rest, d, final):
    if final:
        gf_ref, o_ref = rest
    else:
        (o_ref,) = rest
    p = pl.program_id(0)
    flags = flag_ref[p]

    @pl.when((flags & 2) != 0)
    def _():
        o_ref[...] = jnp.zeros_like(o_ref)

    @pl.when((flags & 1) != 0)
    def _():
        gpos = ch_ref[p] * COMB_ENT + lax.broadcasted_iota(I32, (COMB_ROWS, COMB_ENT), 1)
        st = start_ref[...]
        onehot = _ones((gpos >= st) & (gpos < st + cnt_ref[...]), BF16)
        y = ys_ref[...]
        hi = y.astype(BF16)
        lo = (y - hi.astype(F32)).astype(BF16)
        o_ref[...] += _dot(onehot, hi) + _dot(onehot, lo)

    @pl.when((flags & 4) != 0)
    def _():
        m = mod_ref[0]
        out = xm_ref[...] + m[:, 5 * d:6 * d] * o_ref[...]
        if final:
            out = out * lax.rsqrt(jnp.mean(out * out, axis=-1, keepdims=True) + EPS) * gf_ref[...]
        o_ref[...] = out


def combine(ys_sorted, xm, start_col, cnt_col, mod, kb, ch, flags, rows, tiles_per_sample, n_x_tiles,
            n_samples, final_gain):
    d = xm.shape[1]
    n_pairs = kb.shape[0]
    mod_idx = lambda p, kb, ch, fl: (jnp.where(kb[p] < n_x_tiles, kb[p] // tiles_per_sample, n_samples), 0, 0)
    in_specs = [pl.BlockSpec((COMB_ENT, d), lambda p, kb, ch, fl: (ch[p], 0)),
                pl.BlockSpec((COMB_ROWS, d), lambda p, kb, ch, fl: (kb[p], 0)),
                pl.BlockSpec((COMB_ROWS, 1), lambda p, kb, ch, fl: (kb[p], 0)),
                pl.BlockSpec((COMB_ROWS, 1), lambda p, kb, ch, fl: (kb[p], 0)),
                pl.BlockSpec((1, 1, N_MOD * d), mod_idx)]
    args = [ys_sorted, xm, start_col, cnt_col, mod]
    if final_gain is not None:
        in_specs.append(pl.BlockSpec((1, d), lambda p, kb, ch, fl: (0, 0)))
        args.append(final_gain)
    return pl.pallas_call(
        functools.partial(_combine_kernel, d=d, final=final_gain is not None),
        name="combine",
        grid_spec=pltpu.PrefetchScalarGridSpec(
            num_scalar_prefetch=3, grid=(n_pairs,),
            in_specs=in_specs,
            out_specs=pl.BlockSpec((COMB_ROWS, d), lambda p, kb, ch, fl: (kb[p], 0))),
        out_shape=jax.ShapeDtypeStruct((rows, d), F32),
        compiler_params=_params(("arbitrary",)),
    )(kb, ch, flags, *args)


def _combine_schedule(start_row, n_blocks, n_chunks):
    blk_start = start_row[::COMB_ROWS]
    blk_end = jnp.concatenate([blk_start[1:], jnp.full((1,), n_chunks * COMB_ENT, I32)])
    lo = jnp.minimum(blk_start // COMB_ENT, n_chunks - 1)
    hi = jnp.maximum(lo, (blk_end - 1) // COMB_ENT)
    cnt = hi - lo + 1
    off = jnp.cumsum(cnt) - cnt
    total = off[-1] + cnt[-1]
    n_pairs = n_blocks + n_chunks
    pidx = jnp.arange(n_pairs, dtype=I32)
    kb = jnp.minimum(jnp.searchsorted(off, pidx, side='right').astype(I32) - 1, n_blocks - 1)
    valid = pidx < total
    ch = jnp.where(valid, lo[kb] + pidx - off[kb], hi[n_blocks - 1])
    first = valid & (pidx == off[kb])
    last = valid & (pidx == off[kb] + cnt[kb] - 1)
    flags = valid.astype(I32) + 2 * first.astype(I32) + 4 * last.astype(I32)
    return kb, ch.astype(I32), flags


def _rope_tables(lx, n_ctx_rows, n_samples):
    n_freq = QK_DIM // 4
    freqs = ROPE_BASE ** (-jnp.arange(n_freq, dtype=F32) / n_freq)
    rows = lx // GRID_W
    row = jnp.repeat(jnp.arange(rows, dtype=F32), GRID_W)
    col = jnp.tile(jnp.arange(GRID_W, dtype=F32), rows)
    ang = jnp.concatenate([row[:, None] * freqs, col[:, None] * freqs], axis=-1)
    cos, sin = jnp.cos(ang), jnp.sin(ang)
    ct = jnp.tile(jnp.concatenate([cos, cos], axis=-1), (n_samples, LANES // QK_DIM))
    st = jnp.tile(jnp.concatenate([-sin, sin], axis=-1), (n_samples, LANES // QK_DIM))
    ct = jnp.concatenate([ct, jnp.ones((n_ctx_rows, LANES), F32)], axis=0)
    st = jnp.concatenate([st, jnp.zeros((n_ctx_rows, LANES), F32)], axis=0)
    return ct, st


def kernel(x, c, ctx, c_ctx, w_ada, b_ada, norm_mix, norm_ffn, w_in, w_out, ret_log_decay, ret_norm,
           diff_lambda, diff_norm, w_router, w_gate, w_up, w_down, norm_final):
    n_b, lx, d = x.shape
    lc = ctx.shape[1]
    depth = w_ada.shape[0]
    heads = ret_log_decay.shape[-1]
    dheads = diff_norm.shape[-1] // V_DIM
    n_e = w_router.shape[-1]
    rx, rc = n_b * lx, n_b * lc
    cap_x = CAPACITY_FACTOR * lx // n_e
    cap_c = CAPACITY_FACTOR * lc // n_e
    tm = 512 if (lx % 512 == 0 and rx % 512 == 0 and rc % 512 == 0) else 256
    tmo = 256
    assert lx % lc == 0 and lc % tmo == 0 and heads % 2 == 0 and n_b < MOD_ROWS

    qk_scale = QK_DIM ** -0.5
    n_rq = heads * QK_DIM // LANES
    kinds = ([(True, qk_scale)] * n_rq + [(True, 1.0)] * n_rq + [(False, 1.0)] * (2 * heads)
             + [(True, qk_scale)] * dheads + [(True, 1.0)] * dheads + [(False, 1.0)] * dheads)

    xa = jnp.concatenate([x.reshape(rx, d), ctx.reshape(rc, d)], axis=0)
    cc = jnp.concatenate([c, c_ctx[None], jnp.zeros((MOD_ROWS - n_b - 1, d), F32)], axis=0)
    mods = ada_modulation(cc, w_ada, b_ada)
    ct, st = _rope_tables(lx, rc, n_b)
    log_g = jnp.log1p(-jnp.exp(ret_log_decay.astype(F32)))
    out = None

    for layer in range(depth):
        last = layer == depth - 1
        lam_init = 0.8 - 0.6 * math.exp(-0.3 * layer)
        mod = mods[layer].reshape(MOD_ROWS, 1, N_MOD * d)
        p = in_projection(xa, mod, norm_mix[layer][None], w_in[layer].astype(BF16), ct, st,
                          rx // tm, lx // tm, n_b, kinds, tm)

        ret = retention(p, log_g[layer], ret_norm[layer][None], n_b, lx, lc, heads, not last)
        dif_x = diff_attention(p, diff_lambda[layer], diff_norm[layer][None], n_b, lx, lc, heads, dheads,
                               lam_init, True, 256)
        if last:
            rows = rx
            ret_a, dif_a = ret[0], dif_x
        else:
            rows = rx + rc
            dif_c = diff_attention(p, diff_lambda[layer], diff_norm[layer][None], n_b, lx, lc, heads, dheads,
                                   lam_init, False, lc)
            ret_a = jnp.concatenate([ret[0], ret[1]], axis=0)
            dif_a = jnp.concatenate([dif_x, dif_c], axis=0)

        xm, h2, aff_t = out_projection(ret_a, dif_a, w_out[layer].astype(BF16), xa, mod, norm_ffn[layer][None],
                                       w_router[layer].T, rows, rx // tmo, lx // tmo, n_b, tmo)

        sel = [expert_choice(aff_t, n_b, lx, 0, cap_x, 0, 0)]
        if not last:
            sel.append(expert_choice(aff_t, n_b, lc, rx // lc, cap_c, rx, n_b * n_e * cap_x))
        to_es = lambda t: jnp.transpose(t, (2, 0, 1)).reshape(n_e, -1)
        idx = jnp.concatenate([to_es(s[0]) for s in sel], axis=1)
        gates = jnp.concatenate([to_es(s[1]) for s in sel], axis=1)
        dest = jnp.concatenate([to_es(s[2]) for s in sel], axis=1)
        start_row = jnp.concatenate([s[3][0] for s in sel])
        cnt_row = jnp.concatenate([s[4][0] for s in sel])
        slots = idx.shape[1]
        n_ent = n_e * slots

        xs = row_permute(h2, idx.reshape(-1), jnp.arange(n_ent, dtype=I32), n_ent)
        ys = expert_ffn(xs.reshape(n_e, slots, d), w_gate[layer], w_up[layer], w_down[layer],
                        gates.reshape(n_e, slots, 1))
        ys_sorted = row_permute(ys.reshape(n_ent, d), jnp.arange(n_ent, dtype=I32), dest.reshape(-1), n_ent)

        kb, ch, flags = _combine_schedule(start_row, rows // COMB_ROWS, n_ent // COMB_ENT)
        xa_new = combine(ys_sorted, xm, start_row[:, None], cnt_row[:, None], mod, kb, ch, flags, rows,
                         lx // COMB_ROWS, rx // COMB_ROWS, n_b, norm_final[None] if last else None)
        if last:
            out = xa_new
        else:
            xa = xa_new
    return out.reshape(n_b, lx, d)
```

```python
import functools
import math

import jax
import jax.numpy as jnp
from jax import lax
from jax.experimental import pallas as pl
from jax.experimental.pallas import tpu as pltpu

F32 = jnp.float32
BF16 = jnp.bfloat16
I32 = jnp.int32

EPS = 1e-6
N_MOD = 6
GRID_W = 64
ROPE_BASE = 10000.0
QK_DIM = 64
V_DIM = 128
LANES = 128
CHUNK = 128
CAPACITY_FACTOR = 2
MOD_ROWS = 8
VMEM_LIMIT = 56 * 1024 * 1024


def _params(sem, vmem=VMEM_LIMIT):
    return pltpu.CompilerParams(dimension_semantics=sem, vmem_limit_bytes=vmem)


def _silu(x):
    return x / (1.0 + jnp.exp(-x))


def _dot(a, b):
    return jnp.dot(a, b, preferred_element_type=F32)


def _dot_nt(a, b):
    return lax.dot_general(a, b, (((1,), (1,)), ((), ())), preferred_element_type=F32)


def _ada_kernel(c_ref, w_ref, b_ref, o_ref):
    s = _silu(c_ref[...]).astype(BF16)
    o_ref[0] = _dot(s, w_ref[0].astype(BF16)) + b_ref[0]


def ada_modulation(cc, w_ada, b_ada):
    depth, d, n = w_ada.shape
    tn = 1024 if n % 1024 == 0 else n
    return pl.pallas_call(
        _ada_kernel,
        name="ada_mod",
        grid=(depth, n // tn),
        in_specs=[pl.BlockSpec((MOD_ROWS, d), lambda l, j: (0, 0)),
                  pl.BlockSpec((1, d, tn), lambda l, j: (l, 0, j)),
                  pl.BlockSpec((1, 1, tn), lambda l, j: (l, 0, j))],
        out_specs=pl.BlockSpec((1, MOD_ROWS, tn), lambda l, j: (l, 0, j)),
        out_shape=jax.ShapeDtypeStruct((depth, MOD_ROWS, n), F32),
        compiler_params=_params(("arbitrary", "arbitrary")),
    )(cc, w_ada, b_ada.reshape(depth, 1, n))


def _rmsnorm_mod(x, gain, shift, scale):
    y = x * lax.rsqrt(jnp.mean(x * x, axis=-1, keepdims=True) + EPS) * gain
    return y * (1.0 + scale) + shift


def _swap_halves(a):
    lane = lax.broadcasted_iota(I32, a.shape, 1)
    return jnp.where((lane & 32) == 0, pltpu.roll(a, 96, 1), pltpu.roll(a, 32, 1))


def _inproj_kernel(x_ref, mod_ref, g_ref, w_ref, ct_ref, st_ref, o_ref, h_scr, *, d, tn, kinds):
    j = pl.program_id(1)

    @pl.when(j == 0)
    def _():
        m = mod_ref[0]
        h_scr[...] = _rmsnorm_mod(x_ref[...], g_ref[...], m[:, 0:d], m[:, d:2 * d]).astype(BF16)

    acc = _dot(h_scr[...], w_ref[...])
    for jj in range(len(kinds) * LANES // tn):
        @pl.when(j == jj)
        def _(jj=jj):
            for s in range(tn // LANES):
                rope, scale = kinds[jj * (tn // LANES) + s]
                a = acc[:, s * LANES:(s + 1) * LANES]
                if rope:
                    a = a * ct_ref[...] + _swap_halves(a) * st_ref[...]
                if scale != 1.0:
                    a = a * scale
                o_ref[:, s * LANES:(s + 1) * LANES] = a.astype(BF16)


def in_projection(xa, mod, gain, w_bf, ct, st, n_x_tiles, tiles_per_sample, n_samples, kinds, tm):
    r, d = xa.shape
    n = w_bf.shape[1]
    tn = 1024 if n % 1024 == 0 else 512
    mod_idx = lambda i, j: (jnp.where(i < n_x_tiles, i // tiles_per_sample, n_samples), 0, 0)
    return pl.pallas_call(
        functools.partial(_inproj_kernel, d=d, tn=tn, kinds=kinds),
        name="in_proj",
        grid=(r // tm, n // tn),
        in_specs=[pl.BlockSpec((tm, d), lambda i, j: (i, 0)),
                  pl.BlockSpec((1, 1, N_MOD * d), mod_idx),
                  pl.BlockSpec((1, d), lambda i, j: (0, 0)),
                  pl.BlockSpec((d, tn), lambda i, j: (0, j)),
                  pl.BlockSpec((tm, LANES), lambda i, j: (i, 0)),
                  pl.BlockSpec((tm, LANES), lambda i, j: (i, 0))],
        out_specs=pl.BlockSpec((tm, tn), lambda i, j: (i, j)),
        out_shape=jax.ShapeDtypeStruct((r, n), BF16),
        scratch_shapes=[pltpu.VMEM((tm, d), BF16)],
        compiler_params=_params(("arbitrary", "arbitrary")),
    )(xa, mod, gain, w_bf, ct, st)


def _ret_kernel(lg_ref, qx_ref, kx_ref, vx_ref, gx_ref, qc_ref, kc_ref, vc_ref, gc_ref, gain_ref,
                *rest, lx, lc, ctx_out):
    if ctx_out:
        ox_ref, oc_ref, accx, accc = rest
    else:
        ox_ref, accx = rest
    pair = pl.program_id(1)
    lane = lax.broadcasted_iota(I32, (CHUNK, LANES), 1)
    lane_c = lax.broadcasted_iota(I32, (lc, LANES), 1)
    dif = (lax.broadcasted_iota(I32, (CHUNK, CHUNK), 0) - lax.broadcasted_iota(I32, (CHUNK, CHUNK), 1)).astype(F32)
    pos = lax.broadcasted_iota(I32, (CHUNK, 1), 0).astype(F32)
    cpos = lax.broadcasted_iota(I32, (lc, 1), 0).astype(F32)
    c_len = float(CHUNK)

    for hh in range(2):
        head = 2 * pair + hh
        lgf = lg_ref[0, head]
        lgb = lg_ref[1, head]
        hm = (lane >= QK_DIM * hh) & (lane < QK_DIM * (hh + 1))
        hm_c = (lane_c >= QK_DIM * hh) & (lane_c < QK_DIM * (hh + 1))
        vsl = slice(hh * V_DIM, (hh + 1) * V_DIM)

        dec_f = jnp.where(dif >= 0, jnp.exp(lgf * jnp.maximum(dif, 0.0)), 0.0)
        dec_b = jnp.where(dif < 0, jnp.exp(lgb * jnp.maximum(-dif, 0.0)), 0.0)
        qd_f = jnp.exp(lgf * (pos + 1.0))
        kd_f = jnp.exp(lgf * (c_len - 1.0 - pos))
        cd_f = jnp.exp(lgf * c_len)
        qd_b = jnp.exp(lgb * (c_len - pos))
        kd_b = jnp.exp(lgb * pos)
        cd_b = jnp.exp(lgb * c_len)

        kc = jnp.where(hm_c, kc_ref[...], 0).astype(F32)
        vc = vc_ref[:, vsl]
        s_f = _dot((kc * jnp.exp(lgf * (lc - 1.0 - cpos))).T.astype(BF16), vc)
        s_b = _dot((kc * jnp.exp(lgb * cpos)).T.astype(BF16), vc)

        def scan(q_ref, k_ref, v_ref, acc_ref, n, s0, dec, qd, kd, cd, reverse, first):
            def body(t, s):
                c = (n - 1 - t) if reverse else t
                rows = pl.ds(pl.multiple_of(c * CHUNK, CHUNK), CHUNK)
                q = jnp.where(hm, q_ref[rows, :], 0)
                k = jnp.where(hm, k_ref[rows, :], 0)
                v = v_ref[rows, vsl]
                sc = _dot_nt(q, k) * dec
                o = _dot(sc.astype(BF16), v) + _dot(q, s.astype(BF16)) * qd
                if first:
                    acc_ref[rows, vsl] = o
                else:
                    acc_ref[rows, vsl] += o
                kdm = (k.astype(F32) * kd).T.astype(BF16)
                return s * cd + _dot(kdm, v)
            lax.fori_loop(0, n, body, s0)

        def finish(acc_ref, g_ref, o_ref, n):
            def body(c, carry):
                rows = pl.ds(pl.multiple_of(c * CHUNK, CHUNK), CHUNK)
                y = acc_ref[rows, vsl]
                yc = y - jnp.mean(y, axis=-1, keepdims=True)
                out = yc * lax.rsqrt(jnp.mean(yc * yc, axis=-1, keepdims=True) + EPS) * gain_ref[:, vsl]
                o_ref[rows, vsl] = (_silu(g_ref[rows, vsl].astype(F32)) * out).astype(BF16)
                return carry
            lax.fori_loop(0, n, body, 0)

        scan(qx_ref, kx_ref, vx_ref, accx, lx // CHUNK, s_f, dec_f, qd_f, kd_f, cd_f, False, True)
        scan(qx_ref, kx_ref, vx_ref, accx, lx // CHUNK, s_b, dec_b, qd_b, kd_b, cd_b, True, False)
        finish(accx, gx_ref, ox_ref, lx // CHUNK)
        if ctx_out:
            zero = jnp.zeros((LANES, V_DIM), F32)
            scan(qc_ref, kc_ref, vc_ref, accc, lc // CHUNK, zero, dec_f, qd_f, kd_f, cd_f, False, True)
            scan(qc_ref, kc_ref, vc_ref, accc, lc // CHUNK, zero, dec_b, qd_b, kd_b, cd_b, True, False)
            finish(accc, gc_ref, oc_ref, lc // CHUNK)


def retention(p, lg, gain, n_samples, lx, lc, heads, ctx_out):
    pairs = heads // 2
    half = heads // 2
    cb = n_samples * lx // lc
    w2 = 2 * V_DIM
    in_specs = [
        pl.BlockSpec(memory_space=pltpu.SMEM),
        pl.BlockSpec((lx, LANES), lambda b, h: (b, h)),
        pl.BlockSpec((lx, LANES), lambda b, h: (b, half + h)),
        pl.BlockSpec((lx, w2), lambda b, h: (b, half + h)),
        pl.BlockSpec((lx, w2), lambda b, h: (b, heads + h)),
        pl.BlockSpec((lc, LANES), lambda b, h: (cb + b, h)),
        pl.BlockSpec((lc, LANES), lambda b, h: (cb + b, half + h)),
        pl.BlockSpec((lc, w2), lambda b, h: (cb + b, half + h)),
        pl.BlockSpec((lc, w2), lambda b, h: (cb + b, heads + h)),
        pl.BlockSpec((1, w2), lambda b, h: (0, h)),
    ]
    out_specs = [pl.BlockSpec((lx, w2), lambda b, h: (b, h))]
    out_shape = [jax.ShapeDtypeStruct((n_samples * lx, heads * V_DIM), BF16)]
    scratch = [pltpu.VMEM((lx, w2), F32)]
    if ctx_out:
        out_specs.append(pl.BlockSpec((lc, w2), lambda b, h: (b, h)))
        out_shape.append(jax.ShapeDtypeStruct((n_samples * lc, heads * V_DIM), BF16))
        scratch.append(pltpu.VMEM((lc, w2), F32))
    return pl.pallas_call(
        functools.partial(_ret_kernel, lx=lx, lc=lc, ctx_out=ctx_out),
        name="retention",
        grid=(n_samples, pairs),
        in_specs=in_specs, out_specs=out_specs, out_shape=out_shape, scratch_shapes=scratch,
        compiler_params=_params(("arbitrary", "arbitrary")),
    )(lg, p, p, p, p, p, p, p, p, gain)


def _diff_kernel(lam_ref, q_ref, kc_ref, vc_ref, *rest, lam_init, has_x):
    if has_x:
        kx_ref, vx_ref, gain_ref, o_ref = rest
    else:
        gain_ref, o_ref = rest
    lv = lam_ref[...]
    lam = (jnp.exp(jnp.sum(lv[0:1] * lv[1:2], axis=-1, keepdims=True))
           - jnp.exp(jnp.sum(lv[2:3] * lv[3:4], axis=-1, keepdims=True)) + lam_init)
    q = q_ref[...]
    lane = lax.broadcasted_iota(I32, q.shape, 1)
    parts = []
    for t in range(2):
        qm = jnp.where((lane >= QK_DIM * t) & (lane < QK_DIM * (t + 1)), q, 0)
        sc = _dot_nt(qm, kc_ref[...])
        m = jnp.max(sc, axis=-1, keepdims=True)
        if has_x:
            sx = _dot_nt(qm, kx_ref[...])
            m = jnp.maximum(m, jnp.max(sx, axis=-1, keepdims=True))
        pc = jnp.exp(sc - m)
        l = jnp.sum(pc, axis=-1, keepdims=True)
        px = None
        if has_x:
            px = jnp.exp(sx - m)
            l = l + jnp.sum(px, axis=-1, keepdims=True)
        parts.append((pc, px, 1.0 / l))
    (pc0, px0, r0), (pc1, px1, r1) = parts
    w1 = lam * r1
    out = _dot((pc0 * r0 - pc1 * w1).astype(BF16), vc_ref[...])
    if has_x:
        out = out + _dot((px0 * r0 - px1 * w1).astype(BF16), vx_ref[...])
    y = out * lax.rsqrt(jnp.mean(out * out, axis=-1, keepdims=True) + EPS) * gain_ref[...]
    o_ref[...] = (y * (1.0 - lam_init)).astype(BF16)


def diff_attention(p, lam_vec, gain, n_samples, lx, lc, ret_heads, heads, lam_init, latent, tq):
    qo = 3 * ret_heads
    ko = qo + heads
    vo = ko + heads
    cb = n_samples * lx // lc
    lq = lx if latent else lc
    nq = lq // tq
    q_row = (lambda b, t: b * nq + t) if latent else (lambda b, t: n_samples * lx // tq + b * nq + t)
    in_specs = [
        pl.BlockSpec((4, QK_DIM), lambda b, h, t: (0, 0)),
        pl.BlockSpec((tq, LANES), lambda b, h, t: (q_row(b, t), qo + h)),
        pl.BlockSpec((lc, LANES), lambda b, h, t: (cb + b, ko + h)),
        pl.BlockSpec((lc, LANES), lambda b, h, t: (cb + b, vo + h)),
    ]
    args = [lam_vec, p, p, p]
    if latent:
        in_specs += [pl.BlockSpec((lx, LANES), lambda b, h, t: (b, ko + h)),
                     pl.BlockSpec((lx, LANES), lambda b, h, t: (b, vo + h))]
        args += [p, p]
    in_specs.append(pl.BlockSpec((1, LANES), lambda b, h, t: (0, h)))
    args.append(gain)
    return pl.pallas_call(
        functools.partial(_diff_kernel, lam_init=lam_init, has_x=latent),
        name="diff_attn_x" if latent else "diff_attn_ctx",
        grid=(n_samples, heads, nq),
        in_specs=in_specs,
        out_specs=pl.BlockSpec((tq, LANES), lambda b, h, t: (b * nq + t, h)),
        out_shape=jax.ShapeDtypeStruct((n_samples * lq, heads * V_DIM), BF16),
        compiler_params=_params(("arbitrary", "arbitrary", "arbitrary")),
    )(*args)


def _pack_pair(a, b):
    ua = pltpu.bitcast(a.astype(BF16).astype(F32), I32)
    ub = pltpu.bitcast(b.astype(BF16).astype(F32), I32)
    return ua | lax.shift_right_logical(ub, 16)


def _unpack_pair(w):
    return (pltpu.bitcast(w & -65536, F32).astype(BF16), pltpu.bitcast(w << 16, F32).astype(BF16))


def _store_packed_rows(ref, row0, n_rows, val):
    nw = val.shape[1] // (2 * LANES)
    for j in range(nw):
        w = _pack_pair(val[:, j * LANES:(j + 1) * LANES], val[:, (nw + j) * LANES:(nw + j + 1) * LANES])
        ref[pl.ds(row0 * nw + j, n_rows, stride=nw), :] = w


def _load_packed_rows(ref, row0, n_rows, nw, dst, dst_row0):
    for j in range(nw):
        hi, lo = _unpack_pair(ref[pl.ds(row0 * nw + j, n_rows, stride=nw), :])
        dst[dst_row0:dst_row0 + n_rows, j * LANES:(j + 1) * LANES] = hi
        dst[dst_row0:dst_row0 + n_rows, (nw + j) * LANES:(nw + j + 1) * LANES] = lo


def _outproj_kernel(ret_ref, dif_ref, w_ref, x_ref, mod_ref, g_ref, wr_ref, xm_ref, h_ref, aff_ref, *, d, wr):
    y = _dot(ret_ref[...], w_ref[0:wr, :]) + _dot(dif_ref[...], w_ref[wr:, :])
    m = mod_ref[0]
    xm = x_ref[...] + m[:, 2 * d:3 * d] * y
    xm_ref[...] = xm
    h = _rmsnorm_mod(xm, g_ref[...], m[:, 3 * d:4 * d], m[:, 4 * d:5 * d])
    _store_packed_rows(h_ref, 0, h.shape[0], h)
    logits = lax.dot_general(wr_ref[...], h, (((1,), (1,)), ((), ())),
                             precision=lax.Precision.HIGHEST, preferred_element_type=F32)
    e = jnp.exp(logits - jnp.max(logits, axis=0, keepdims=True))
    aff_ref[...] = e / jnp.sum(e, axis=0, keepdims=True)


def out_projection(ret, dif, w_bf, xa, mod, gain, w_router_t, rows, n_x_tiles, tiles_per_sample, n_samples, tm):
    d = xa.shape[1]
    wr = ret.shape[1]
    e = w_router_t.shape[0]
    nw = d // (2 * LANES)
    mod_idx = lambda i: (jnp.where(i < n_x_tiles, i // tiles_per_sample, n_samples), 0, 0)
    return pl.pallas_call(
        functools.partial(_outproj_kernel, d=d, wr=wr),
        name="out_proj_router",
        grid=(rows // tm,),
        in_specs=[pl.BlockSpec((tm, wr), lambda i: (i, 0)),
                  pl.BlockSpec((tm, dif.shape[1]), lambda i: (i, 0)),
                  pl.BlockSpec(w_bf.shape, lambda i: (0, 0)),
                  pl.BlockSpec((tm, d), lambda i: (i, 0)),
                  pl.BlockSpec((1, 1, N_MOD * d), mod_idx),
                  pl.BlockSpec((1, d), lambda i: (0, 0)),
                  pl.BlockSpec((e, d), lambda i: (0, 0))],
        out_specs=[pl.BlockSpec((tm, d), lambda i: (i, 0)),
                   pl.BlockSpec((tm * nw, LANES), lambda i: (i, 0)),
                   pl.BlockSpec((e, tm), lambda i: (0, i))],
        out_shape=[jax.ShapeDtypeStruct((rows, d), F32),
                   jax.ShapeDtypeStruct((rows * nw, LANES), I32),
                   jax.ShapeDtypeStruct((e, rows), F32)],
        compiler_params=_params(("arbitrary",)),
    )(ret, dif, w_bf, xa, mod, gain, w_router_t)


def _ones(mask, dtype):
    return jnp.where(mask, 1.0, 0.0).astype(dtype)


def _prefix_lanes(m):
    upper = _ones(lax.broadcasted_iota(I32, (LANES, LANES), 0) < lax.broadcasted_iota(I32, (LANES, LANES), 1), BF16)
    run = jnp.zeros((m.shape[0], 1), F32)
    outs = []
    for s in range(m.shape[1] // LANES):
        seg = m[:, s * LANES:(s + 1) * LANES]
        outs.append(_dot(seg, upper) + run)
        run = run + jnp.sum(seg.astype(F32), axis=1, keepdims=True)
    return jnp.concatenate(outs, axis=1)


def _topk_kernel(a_ref, idx_ref, gate_ref, dest_ref, start_ref, cnt_ref, posm_scr, dest_scr,
                 *, lg, cap, row_base, ent_base):
    g = pl.program_id(0)
    a = a_ref[...]
    n_e = a.shape[0]
    bits = pltpu.bitcast(a, I32)

    def bisect(_, lohi):
        lo, hi = lohi
        mid = lo + ((hi - lo) >> 1)
        ge = jnp.sum(_ones(bits >= mid, F32), axis=1, keepdims=True) >= cap
        return jnp.where(ge, mid, lo), jnp.where(ge, hi, mid)

    thr, _ = lax.fori_loop(0, 31, bisect,
                           (jnp.zeros((n_e, 1), I32), jnp.full((n_e, 1), 0x7F800000, I32)))
    gt = bits > thr
    eq = bits == thr
    need = cap - jnp.sum(_ones(gt, F32), axis=1, keepdims=True)
    sel = gt | (eq & (_prefix_lanes(_ones(eq, BF16)) < need))
    selb = _ones(sel, BF16)
    pos = _prefix_lanes(selb)
    start_t = jnp.sum(pos, axis=0, keepdims=True)
    lower = _ones(lax.broadcasted_iota(I32, (n_e, n_e), 0) > lax.broadcasted_iota(I32, (n_e, n_e), 1), BF16)
    base = (ent_base + g * (n_e * cap)).astype(F32)
    start_ref[...] = (start_t + base).astype(I32)
    cnt_ref[...] = jnp.sum(_ones(sel, F32), axis=0, keepdims=True).astype(I32)
    posm_scr[...] = jnp.where(sel, pos, -1.0)
    dest_scr[...] = start_t + base + _dot(lower, selb)

    tvals = (row_base + g * lg + lax.broadcasted_iota(I32, (1, lg), 1)).astype(F32)
    slot = lax.broadcasted_iota(I32, (cap, 1), 0).astype(F32)
    col = lax.broadcasted_iota(I32, (cap, n_e), 1)

    def compact(e, carry):
        ia, ga, da = carry
        m = posm_scr[pl.ds(e, 1), :] == slot
        i_e = jnp.sum(jnp.where(m, tvals, 0.0), axis=1, keepdims=True)
        g_e = jnp.sum(jnp.where(m, a_ref[pl.ds(e, 1), :], 0.0), axis=1, keepdims=True)
        d_e = jnp.sum(jnp.where(m, dest_scr[pl.ds(e, 1), :], 0.0), axis=1, keepdims=True)
        here = col == e
        return jnp.where(here, i_e, ia), jnp.where(here, g_e, ga), jnp.where(here, d_e, da)

    z = jnp.zeros((cap, n_e), F32)
    ia, ga, da = lax.fori_loop(0, n_e, compact, (z, z, z))
    idx_ref[0] = ia.astype(I32)
    gate_ref[0] = ga
    dest_ref[0] = da.astype(I32)


def expert_choice(aff_t, groups, lg, first_block, cap, row_base, ent_base):
    e = aff_t.shape[0]
    return pl.pallas_call(
        functools.partial(_topk_kernel, lg=lg, cap=cap, row_base=row_base, ent_base=ent_base),
        name="expert_choice",
        grid=(groups,),
        in_specs=[pl.BlockSpec((e, lg), lambda g: (0, first_block + g))],
        out_specs=[pl.BlockSpec((1, cap, e), lambda g: (g, 0, 0)),
                   pl.BlockSpec((1, cap, e), lambda g: (g, 0, 0)),
                   pl.BlockSpec((1, cap, e), lambda g: (g, 0, 0)),
                   pl.BlockSpec((1, lg), lambda g: (0, g)),
                   pl.BlockSpec((1, lg), lambda g: (0, g))],
        out_shape=[jax.ShapeDtypeStruct((groups, cap, e), I32),
                   jax.ShapeDtypeStruct((groups, cap, e), F32),
                   jax.ShapeDtypeStruct((groups, cap, e), I32),
                   jax.ShapeDtypeStruct((1, groups * lg), I32),
                   jax.ShapeDtypeStruct((1, groups * lg), I32)],
        scratch_shapes=[pltpu.VMEM((e, lg), F32), pltpu.VMEM((e, lg), F32)],
        compiler_params=_params(("arbitrary",)),
    )(aff_t)


DMA_WINDOW = 64


def _rowperm_kernel(si_ref, di_ref, src_ref, dst_ref, sem, *, n, nw):
    def row_copy(i):
        s = pl.multiple_of(si_ref[i] * nw, nw)
        t = pl.multiple_of(di_ref[i] * nw, nw)
        return pltpu.make_async_copy(src_ref.at[pl.ds(s, nw)], dst_ref.at[pl.ds(t, nw)], sem)

    def issue(i, carry):
        row_copy(i).start()

        @pl.when(i >= DMA_WINDOW)
        def _():
            row_copy(i - DMA_WINDOW).wait()
        return carry

    def drain(i, carry):
        row_copy(i).wait()
        return carry

    lax.fori_loop(0, n, issue, 0)
    lax.fori_loop(n - DMA_WINDOW, n, drain, 0)


def row_permute(src, si, di, n_dst, nw):
    n = si.shape[0]
    assert n == n_dst and n >= DMA_WINDOW
    return pl.pallas_call(
        functools.partial(_rowperm_kernel, n=n, nw=nw),
        name="row_permute",
        grid_spec=pltpu.PrefetchScalarGridSpec(
            num_scalar_prefetch=2, grid=(1,),
            in_specs=[pl.BlockSpec(memory_space=pl.ANY)],
            out_specs=pl.BlockSpec(memory_space=pl.ANY),
            scratch_shapes=[pltpu.SemaphoreType.DMA(())]),
        out_shape=jax.ShapeDtypeStruct((n_dst * nw, src.shape[1]), src.dtype),
        compiler_params=_params(("arbitrary",)),
    )(si, di, src)


def _ffn_kernel(xs_ref, wg_ref, wu_ref, wd_ref, gate_ref, ys_ref, xb_scr, acc_scr, *, ts, nw):
    f = pl.program_id(1)
    s = xb_scr.shape[0]

    @pl.when(f == 0)
    def _():
        _load_packed_rows(xs_ref.at[0], 0, s, nw, xb_scr, 0)

    wg = wg_ref[0, 0].astype(BF16)
    wu = wu_ref[0, 0].astype(BF16)
    wd = wd_ref[0, 0].astype(BF16)
    for r in range(s // ts):
        rows = slice(r * ts, (r + 1) * ts)
        xb = xb_scr[rows, :]
        hid = _silu(_dot(xb, wg)) * _dot(xb, wu)
        y = _dot(hid.astype(BF16), wd)

        @pl.when(f == 0)
        def _():
            acc_scr[rows, :] = y

        @pl.when(f > 0)
        def _():
            acc_scr[rows, :] += y

    @pl.when(f == pl.num_programs(1) - 1)
    def _():
        for r in range(s // ts):
            rows = slice(r * ts, (r + 1) * ts)
            _store_packed_rows(ys_ref.at[0], r * ts, ts, acc_scr[rows, :] * gate_ref[0, rows, :])


def expert_ffn(xs, w_gate, w_up, w_down, layer, gates, d):
    nw = d // (2 * LANES)
    e = xs.shape[0]
    s = xs.shape[1] // nw
    ff = w_gate.shape[3]
    tf = 256 if ff % 256 == 0 else ff
    once = pl.Buffered(1)
    ts = max(t for t in range(8, 577, 8) if s % t == 0)
    return pl.pallas_call(
        functools.partial(_ffn_kernel, ts=ts, nw=nw),
        name="expert_ffn",
        grid=(e, ff // tf),
        in_specs=[pl.BlockSpec((1, s * nw, LANES), lambda i, f: (i, 0, 0)),
                  pl.BlockSpec((1, 1, d, tf), lambda i, f: (layer, i, 0, f)),
                  pl.BlockSpec((1, 1, d, tf), lambda i, f: (layer, i, 0, f)),
                  pl.BlockSpec((1, 1, tf, d), lambda i, f: (layer, i, f, 0)),
                  pl.BlockSpec((1, s, 1), lambda i, f: (i, 0, 0))],
        out_specs=pl.BlockSpec((1, s * nw, LANES), lambda i, f: (i, 0, 0), pipeline_mode=once),
        out_shape=jax.ShapeDtypeStruct((e, s * nw, LANES), I32),
        scratch_shapes=[pltpu.VMEM((s, d), BF16), pltpu.VMEM((s, d), F32)],
        compiler_params=_params(("arbitrary", "arbitrary")),
    )(xs, w_gate, w_up, w_down, gates)


COMB_ROWS = 128
COMB_ENT = 256


def _combine_kernel(kb_ref, ch_ref, flag_ref, ys_ref, xm_ref, start_ref, cnt_ref, mod_ref, *rest, d, final):
    if final:
        gf_ref, o_ref, chunk_scr = rest
    else:
        o_ref, chunk_scr = rest
    p = pl.program_id(0)
    flags = flag_ref[p]

    @pl.when((flags & 2) != 0)
    def _():
        o_ref[...] = jnp.zeros_like(o_ref)

    @pl.when((flags & 8) != 0)
    def _():
        _load_packed_rows(ys_ref, 0, COMB_ENT, d // (2 * LANES), chunk_scr, 0)

    @pl.when((flags & 1) != 0)
    def _():
        gpos = ch_ref[p] * COMB_ENT + lax.broadcasted_iota(I32, (COMB_ROWS, COMB_ENT), 1)
        st = start_ref[...]
        onehot = _ones((gpos >= st) & (gpos < st + cnt_ref[...]), BF16)
        o_ref[...] += _dot(onehot, chunk_scr[...])

    @pl.when((flags & 4) != 0)
    def _():
        m = mod_ref[0]
        out = xm_ref[...] + m[:, 5 * d:6 * d] * o_ref[...]
        if final:
            out = out * lax.rsqrt(jnp.mean(out * out, axis=-1, keepdims=True) + EPS) * gf_ref[...]
        o_ref[...] = out


def combine(ys_sorted, xm, start_col, cnt_col, mod, kb, ch, flags, rows, tiles_per_sample, n_x_tiles,
            n_samples, final_gain):
    d = xm.shape[1]
    nw = d // (2 * LANES)
    n_pairs = kb.shape[0]
    mod_idx = lambda p, kb, ch, fl: (jnp.where(kb[p] < n_x_tiles, kb[p] // tiles_per_sample, n_samples), 0, 0)
    in_specs = [pl.BlockSpec((COMB_ENT * nw, LANES), lambda p, kb, ch, fl: (ch[p], 0)),
                pl.BlockSpec((COMB_ROWS, d), lambda p, kb, ch, fl: (kb[p], 0)),
                pl.BlockSpec((COMB_ROWS, 1), lambda p, kb, ch, fl: (kb[p], 0)),
                pl.BlockSpec((COMB_ROWS, 1), lambda p, kb, ch, fl: (kb[p], 0)),
                pl.BlockSpec((1, 1, N_MOD * d), mod_idx)]
    args = [ys_sorted, xm, start_col, cnt_col, mod]
    if final_gain is not None:
        in_specs.append(pl.BlockSpec((1, d), lambda p, kb, ch, fl: (0, 0)))
        args.append(final_gain)
    return pl.pallas_call(
        functools.partial(_combine_kernel, d=d, final=final_gain is not None),
        name="combine",
        grid_spec=pltpu.PrefetchScalarGridSpec(
            num_scalar_prefetch=3, grid=(n_pairs,),
            in_specs=in_specs,
            out_specs=pl.BlockSpec((COMB_ROWS, d), lambda p, kb, ch, fl: (kb[p], 0)),
            scratch_shapes=[pltpu.VMEM((COMB_ENT, d), BF16)]),
        out_shape=jax.ShapeDtypeStruct((rows, d), F32),
        compiler_params=_params(("arbitrary",)),
    )(kb, ch, flags, *args)


def _combine_schedule(start_row, n_blocks, n_chunks):
    blk_start = start_row[::COMB_ROWS]
    blk_end = jnp.concatenate([blk_start[1:], jnp.full((1,), n_chunks * COMB_ENT, I32)])
    lo = jnp.minimum(blk_start // COMB_ENT, n_chunks - 1)
    hi = jnp.maximum(lo, (blk_end - 1) // COMB_ENT)
    cnt = hi - lo + 1
    off = jnp.cumsum(cnt) - cnt
    total = off[-1] + cnt[-1]
    n_pairs = n_blocks + n_chunks
    pidx = jnp.arange(n_pairs, dtype=I32)
    kb = jnp.sum((off[None, :] <= pidx[:, None]).astype(I32), axis=1) - 1
    valid = pidx < total
    ch = jnp.where(valid, lo[kb] + pidx - off[kb], hi[n_blocks - 1]).astype(I32)
    first = valid & (pidx == off[kb])
    last = valid & (pidx == off[kb] + cnt[kb] - 1)
    fresh = jnp.concatenate([jnp.ones((1,), bool), ch[1:] != ch[:-1]])
    flags = valid.astype(I32) + 2 * first.astype(I32) + 4 * last.astype(I32) + 8 * fresh.astype(I32)
    return kb, ch, flags


def _rope_tables(lx, n_ctx_rows, n_samples):
    n_freq = QK_DIM // 4
    freqs = ROPE_BASE ** (-jnp.arange(n_freq, dtype=F32) / n_freq)
    rows = lx // GRID_W
    row = jnp.repeat(jnp.arange(rows, dtype=F32), GRID_W)
    col = jnp.tile(jnp.arange(GRID_W, dtype=F32), rows)
    ang = jnp.concatenate([row[:, None] * freqs, col[:, None] * freqs], axis=-1)
    cos, sin = jnp.cos(ang), jnp.sin(ang)
    ct = jnp.tile(jnp.concatenate([cos, cos], axis=-1), (n_samples, LANES // QK_DIM))
    st = jnp.tile(jnp.concatenate([-sin, sin], axis=-1), (n_samples, LANES // QK_DIM))
    ct = jnp.concatenate([ct, jnp.ones((n_ctx_rows, LANES), F32)], axis=0)
    st = jnp.concatenate([st, jnp.zeros((n_ctx_rows, LANES), F32)], axis=0)
    return ct, st


def kernel(x, c, ctx, c_ctx, w_ada, b_ada, norm_mix, norm_ffn, w_in, w_out, ret_log_decay, ret_norm,
           diff_lambda, diff_norm, w_router, w_gate, w_up, w_down, norm_final):
    n_b, lx, d = x.shape
    lc = ctx.shape[1]
    depth = w_ada.shape[0]
    heads = ret_log_decay.shape[-1]
    dheads = diff_norm.shape[-1] // V_DIM
    n_e = w_router.shape[-1]
    rx, rc = n_b * lx, n_b * lc
    cap_x = CAPACITY_FACTOR * lx // n_e
    cap_c = CAPACITY_FACTOR * lc // n_e
    tm = 512 if (lx % 512 == 0 and rx % 512 == 0 and rc % 512 == 0) else 256
    tmo = 256
    assert lx % lc == 0 and lc % tmo == 0 and heads % 2 == 0 and n_b < MOD_ROWS

    qk_scale = QK_DIM ** -0.5
    n_rq = heads * QK_DIM // LANES
    kinds = ([(True, qk_scale)] * n_rq + [(True, 1.0)] * n_rq + [(False, 1.0)] * (2 * heads)
             + [(True, qk_scale)] * dheads + [(True, 1.0)] * dheads + [(False, 1.0)] * dheads)

    xa = jnp.concatenate([x.reshape(rx, d), ctx.reshape(rc, d)], axis=0)
    cc = jnp.concatenate([c, c_ctx[None], jnp.zeros((MOD_ROWS - n_b - 1, d), F32)], axis=0)
    mods = ada_modulation(cc, w_ada, b_ada)
    ct, st = _rope_tables(lx, rc, n_b)
    log_g = jnp.log1p(-jnp.exp(ret_log_decay.astype(F32)))
    out = None

    for layer in range(depth):
        last = layer == depth - 1
        lam_init = 0.8 - 0.6 * math.exp(-0.3 * layer)
        mod = mods[layer].reshape(MOD_ROWS, 1, N_MOD * d)
        p = in_projection(xa, mod, norm_mix[layer][None], w_in[layer].astype(BF16), ct, st,
                          rx // tm, lx // tm, n_b, kinds, tm)

        ret = retention(p, log_g[layer], ret_norm[layer][None], n_b, lx, lc, heads, not last)
        dif_x = diff_attention(p, diff_lambda[layer], diff_norm[layer][None], n_b, lx, lc, heads, dheads,
                               lam_init, True, 256)
        if last:
            rows = rx
            ret_a, dif_a = ret[0], dif_x
        else:
            rows = rx + rc
            dif_c = diff_attention(p, diff_lambda[layer], diff_norm[layer][None], n_b, lx, lc, heads, dheads,
                                   lam_init, False, lc)
            ret_a = jnp.concatenate([ret[0], ret[1]], axis=0)
            dif_a = jnp.concatenate([dif_x, dif_c], axis=0)

        xm, h2, aff_t = out_projection(ret_a, dif_a, w_out[layer].astype(BF16), xa, mod, norm_ffn[layer][None],
                                       w_router[layer].T, rows, rx // tmo, lx // tmo, n_b, tmo)

        sel = [expert_choice(aff_t, n_b, lx, 0, cap_x, 0, 0)]
        if not last:
            sel.append(expert_choice(aff_t, n_b, lc, rx // lc, cap_c, rx, n_b * n_e * cap_x))
        to_es = lambda t: jnp.transpose(t, (2, 0, 1)).reshape(n_e, -1)
        idx = jnp.concatenate([to_es(s[0]) for s in sel], axis=1)
        gates = jnp.concatenate([to_es(s[1]) for s in sel], axis=1)
        dest = jnp.concatenate([to_es(s[2]) for s in sel], axis=1)
        start_row = jnp.concatenate([s[3][0] for s in sel])
        cnt_row = jnp.concatenate([s[4][0] for s in sel])
        slots = idx.shape[1]
        n_ent = n_e * slots

        nw = d // (2 * LANES)
        ident = jnp.arange(n_ent, dtype=I32)
        xs = row_permute(h2, idx.reshape(-1), ident, n_ent, nw)
        ys = expert_ffn(xs.reshape(n_e, slots * nw, LANES), w_gate, w_up, w_down, layer,
                        gates.reshape(n_e, slots, 1), d)
        ys_sorted = row_permute(ys.reshape(n_ent * nw, LANES), ident, dest.reshape(-1), n_ent, nw)

        kb, ch, flags = _combine_schedule(start_row, rows // COMB_ROWS, n_ent // COMB_ENT)
        xa_new = combine(ys_sorted, xm, start_row[:, None], cnt_row[:, None], mod, kb, ch, flags, rows,
                         lx // COMB_ROWS, rx // COMB_ROWS, n_b, norm_final[None] if last else None)
        if last:
            out = xa_new
        else:
            xa = xa_new
    return out.reshape(n_b, lx, d)
```

```python
import functools
import math

import jax
import jax.numpy as jnp
from jax import lax
from jax.experimental import pallas as pl
from jax.experimental.pallas import tpu as pltpu

F32 = jnp.float32
BF16 = jnp.bfloat16
I32 = jnp.int32

EPS = 1e-6
N_MOD = 6
GRID_W = 64
ROPE_BASE = 10000.0
QK_DIM = 64
V_DIM = 128
LANES = 128
CHUNK = 128
CAPACITY_FACTOR = 2
MOD_ROWS = 8
VMEM_LIMIT = 56 * 1024 * 1024


def _params(sem, vmem=VMEM_LIMIT):
    return pltpu.CompilerParams(dimension_semantics=sem, vmem_limit_bytes=vmem)


def _silu(x):
    return x / (1.0 + jnp.exp(-x))


def _dot(a, b):
    return jnp.dot(a, b, preferred_element_type=F32)


def _dot_nt(a, b):
    return lax.dot_general(a, b, (((1,), (1,)), ((), ())), preferred_element_type=F32)


def _ada_kernel(c_ref, w_ref, b_ref, o_ref):
    s = _silu(c_ref[...]).astype(BF16)
    o_ref[0] = _dot(s, w_ref[0].astype(BF16)) + b_ref[0]


def ada_modulation(cc, w_ada, b_ada):
    depth, d, n = w_ada.shape
    tn = 1024 if n % 1024 == 0 else n
    return pl.pallas_call(
        _ada_kernel,
        name="ada_mod",
        grid=(depth, n // tn),
        in_specs=[pl.BlockSpec((MOD_ROWS, d), lambda l, j: (0, 0)),
                  pl.BlockSpec((1, d, tn), lambda l, j: (l, 0, j)),
                  pl.BlockSpec((1, 1, tn), lambda l, j: (l, 0, j))],
        out_specs=pl.BlockSpec((1, MOD_ROWS, tn), lambda l, j: (l, 0, j)),
        out_shape=jax.ShapeDtypeStruct((depth, MOD_ROWS, n), F32),
        compiler_params=_params(("arbitrary", "arbitrary")),
    )(cc, w_ada, b_ada.reshape(depth, 1, n))


def _rmsnorm_mod(x, gain, shift, scale):
    y = x * lax.rsqrt(jnp.mean(x * x, axis=-1, keepdims=True) + EPS) * gain
    return y * (1.0 + scale) + shift


def _swap_halves(a):
    lane = lax.broadcasted_iota(I32, a.shape, 1)
    return jnp.where((lane & 32) == 0, pltpu.roll(a, 96, 1), pltpu.roll(a, 32, 1))


def _inproj_kernel(x_ref, mod_ref, g_ref, w_ref, ct_ref, st_ref, o_ref, h_scr, *, d, tn, kinds):
    j = pl.program_id(1)

    @pl.when(j == 0)
    def _():
        m = mod_ref[0]
        h_scr[...] = _rmsnorm_mod(x_ref[...], g_ref[...], m[:, 0:d], m[:, d:2 * d]).astype(BF16)

    acc = _dot(h_scr[...], w_ref[...])
    for jj in range(len(kinds) * LANES // tn):
        @pl.when(j == jj)
        def _(jj=jj):
            for s in range(tn // LANES):
                rope, scale = kinds[jj * (tn // LANES) + s]
                a = acc[:, s * LANES:(s + 1) * LANES]
                if rope:
                    a = a * ct_ref[...] + _swap_halves(a) * st_ref[...]
                if scale != 1.0:
                    a = a * scale
                o_ref[:, s * LANES:(s + 1) * LANES] = a.astype(BF16)


def in_projection(xa, mod, gain, w_bf, ct, st, n_x_tiles, tiles_per_sample, n_samples, kinds, tm):
    r, d = xa.shape
    n = w_bf.shape[1]
    tn = 1024 if n % 1024 == 0 else 512
    mod_idx = lambda i, j: (jnp.where(i < n_x_tiles, i // tiles_per_sample, n_samples), 0, 0)
    return pl.pallas_call(
        functools.partial(_inproj_kernel, d=d, tn=tn, kinds=kinds),
        name="in_proj",
        grid=(r // tm, n // tn),
        in_specs=[pl.BlockSpec((tm, d), lambda i, j: (i, 0)),
                  pl.BlockSpec((1, 1, N_MOD * d), mod_idx),
                  pl.BlockSpec((1, d), lambda i, j: (0, 0)),
                  pl.BlockSpec((d, tn), lambda i, j: (0, j)),
                  pl.BlockSpec((tm, LANES), lambda i, j: (i, 0)),
                  pl.BlockSpec((tm, LANES), lambda i, j: (i, 0))],
        out_specs=pl.BlockSpec((tm, tn), lambda i, j: (i, j)),
        out_shape=jax.ShapeDtypeStruct((r, n), BF16),
        scratch_shapes=[pltpu.VMEM((tm, d), BF16)],
        compiler_params=_params(("arbitrary", "arbitrary")),
    )(xa, mod, gain, w_bf, ct, st)


def _ret_kernel(lg_ref, qx_ref, kx_ref, vx_ref, gx_ref, qc_ref, kc_ref, vc_ref, gc_ref, gain_ref,
                *rest, lx, lc, ctx_out):
    if ctx_out:
        ox_ref, oc_ref, accx, accc = rest
    else:
        ox_ref, accx = rest
    pair = pl.program_id(1)
    w2 = 2 * V_DIM
    lane = lax.broadcasted_iota(I32, (CHUNK, LANES), 1)
    dif = (lax.broadcasted_iota(I32, (CHUNK, CHUNK), 0) - lax.broadcasted_iota(I32, (CHUNK, CHUNK), 1)).astype(F32)
    pos = lax.broadcasted_iota(I32, (CHUNK, 1), 0).astype(F32)
    cpos = lax.broadcasted_iota(I32, (lc, 1), 0).astype(F32)
    c_len = float(CHUNK)
    lgf = [lg_ref[0, 2 * pair + hh] for hh in range(2)]
    lgb = [lg_ref[1, 2 * pair + hh] for hh in range(2)]

    def per_head(fn):
        vals = [fn(hh) for hh in range(2)]
        return jnp.concatenate([jnp.broadcast_to(v, (v.shape[0], V_DIM)) for v in vals], axis=1)

    decay = per_head(lambda hh: jnp.where(dif >= 0, jnp.exp(lgf[hh] * jnp.maximum(dif, 0.0)),
                                          jnp.exp(lgb[hh] * jnp.maximum(-dif, 0.0))))
    qd_f = per_head(lambda hh: jnp.exp(lgf[hh] * (pos + 1.0)))
    kd_f = per_head(lambda hh: jnp.exp(lgf[hh] * (c_len - 1.0 - pos)))
    qd_b = per_head(lambda hh: jnp.exp(lgb[hh] * (c_len - pos)))
    kd_b = per_head(lambda hh: jnp.exp(lgb[hh] * pos))
    one = jnp.ones((1, 1), F32)
    cd_f = per_head(lambda hh: jnp.exp(lgf[hh] * c_len) * one)
    cd_b = per_head(lambda hh: jnp.exp(lgb[hh] * c_len) * one)
    srow = lax.broadcasted_iota(I32, (LANES, w2), 0)
    scol = lax.broadcasted_iota(I32, (LANES, w2), 1)
    diag = (srow < QK_DIM) == (scol < V_DIM)
    zeros_v = jnp.zeros((CHUNK, V_DIM), BF16)

    def kv_state(k, v_scaled):
        return jnp.where(diag, _dot(k.astype(F32).T.astype(BF16), v_scaled.astype(BF16)), 0.0)

    kc = kc_ref[...]
    vc = vc_ref[...].astype(F32)
    s_f = kv_state(kc, vc * per_head(lambda hh: jnp.exp(lgf[hh] * (lc - 1.0 - cpos))))
    s_b = kv_state(kc, vc * per_head(lambda hh: jnp.exp(lgb[hh] * cpos)))

    def sweeps(q_ref, k_ref, v_ref, g_ref, o_ref, acc_ref, n, s_f0, s_b0):
        def rows_of(c):
            return pl.ds(pl.multiple_of(c * CHUNK, CHUNK), CHUNK)

        def forward(c, s):
            rows = rows_of(c)
            q, k, v = q_ref[rows, :], k_ref[rows, :], v_ref[rows, :]
            k_bd = jnp.concatenate([jnp.where(lane < QK_DIM, k, 0), jnp.where(lane >= QK_DIM, k, 0)], axis=0)
            v_bd = jnp.concatenate([jnp.concatenate([v[:, :V_DIM], zeros_v], axis=1),
                                    jnp.concatenate([zeros_v, v[:, V_DIM:]], axis=1)], axis=0)
            sc = _dot_nt(q, k_bd) * decay
            acc_ref[rows, :] = _dot(sc.astype(BF16), v_bd) + _dot(q, s.astype(BF16)) * qd_f
            return s * cd_f + kv_state(k, v.astype(F32) * kd_f)

        def backward(t, s):
            rows = rows_of(n - 1 - t)
            q, k, v = q_ref[rows, :], k_ref[rows, :], v_ref[rows, :]
            y = acc_ref[rows, :] + _dot(q, s.astype(BF16)) * qd_b
            for hh in range(2):
                vsl = slice(hh * V_DIM, (hh + 1) * V_DIM)
                yh = y[:, vsl]
                yc = yh - jnp.mean(yh, axis=-1, keepdims=True)
                out = yc * lax.rsqrt(jnp.mean(yc * yc, axis=-1, keepdims=True) + EPS) * gain_ref[:, vsl]
                o_ref[rows, vsl] = (_silu(g_ref[rows, vsl].astype(F32)) * out).astype(BF16)
            return s * cd_b + kv_state(k, v.astype(F32) * kd_b)

        unroll = 4 if n % 4 == 0 else n
        lax.fori_loop(0, n, forward, s_f0, unroll=unroll)
        lax.fori_loop(0, n, backward, s_b0, unroll=unroll)

    sweeps(qx_ref, kx_ref, vx_ref, gx_ref, ox_ref, accx, lx // CHUNK, s_f, s_b)
    if ctx_out:
        zero = jnp.zeros((LANES, w2), F32)
        sweeps(qc_ref, kc_ref, vc_ref, gc_ref, oc_ref, accc, lc // CHUNK, zero, zero)


def retention(p, lg, gain, n_samples, lx, lc, heads, ctx_out):
    pairs = heads // 2
    half = heads // 2
    cb = n_samples * lx // lc
    w2 = 2 * V_DIM
    in_specs = [
        pl.BlockSpec(memory_space=pltpu.SMEM),
        pl.BlockSpec((lx, LANES), lambda b, h: (b, h)),
        pl.BlockSpec((lx, LANES), lambda b, h: (b, half + h)),
        pl.BlockSpec((lx, w2), lambda b, h: (b, half + h)),
        pl.BlockSpec((lx, w2), lambda b, h: (b, heads + h)),
        pl.BlockSpec((lc, LANES), lambda b, h: (cb + b, h)),
        pl.BlockSpec((lc, LANES), lambda b, h: (cb + b, half + h)),
        pl.BlockSpec((lc, w2), lambda b, h: (cb + b, half + h)),
        pl.BlockSpec((lc, w2), lambda b, h: (cb + b, heads + h)),
        pl.BlockSpec((1, w2), lambda b, h: (0, h)),
    ]
    out_specs = [pl.BlockSpec((lx, w2), lambda b, h: (b, h))]
    out_shape = [jax.ShapeDtypeStruct((n_samples * lx, heads * V_DIM), BF16)]
    scratch = [pltpu.VMEM((lx, w2), F32)]
    if ctx_out:
        out_specs.append(pl.BlockSpec((lc, w2), lambda b, h: (b, h)))
        out_shape.append(jax.ShapeDtypeStruct((n_samples * lc, heads * V_DIM), BF16))
        scratch.append(pltpu.VMEM((lc, w2), F32))
    return pl.pallas_call(
        functools.partial(_ret_kernel, lx=lx, lc=lc, ctx_out=ctx_out),
        name="retention",
        grid=(n_samples, pairs),
        in_specs=in_specs, out_specs=out_specs, out_shape=out_shape, scratch_shapes=scratch,
        compiler_params=_params(("arbitrary", "arbitrary")),
    )(lg, p, p, p, p, p, p, p, p, gain)


def _diff_kernel(lam_ref, q_ref, kc_ref, vc_ref, *rest, lam_init, has_x):
    if has_x:
        kx_ref, vx_ref, gain_ref, o_ref = rest
    else:
        gain_ref, o_ref = rest
    lv = lam_ref[...]
    lam = (jnp.exp(jnp.sum(lv[0:1] * lv[1:2], axis=-1, keepdims=True))
           - jnp.exp(jnp.sum(lv[2:3] * lv[3:4], axis=-1, keepdims=True)) + lam_init)
    q = q_ref[...]
    lane = lax.broadcasted_iota(I32, q.shape, 1)
    parts = []
    for t in range(2):
        qm = jnp.where((lane >= QK_DIM * t) & (lane < QK_DIM * (t + 1)), q, 0)
        sc = _dot_nt(qm, kc_ref[...])
        m = jnp.max(sc, axis=-1, keepdims=True)
        if has_x:
            sx = _dot_nt(qm, kx_ref[...])
            m = jnp.maximum(m, jnp.max(sx, axis=-1, keepdims=True))
        pc = jnp.exp(sc - m)
        l = jnp.sum(pc, axis=-1, keepdims=True)
        px = None
        if has_x:
            px = jnp.exp(sx - m)
            l = l + jnp.sum(px, axis=-1, keepdims=True)
        parts.append((pc, px, 1.0 / l))
    (pc0, px0, r0), (pc1, px1, r1) = parts
    w1 = lam * r1
    out = _dot((pc0 * r0 - pc1 * w1).astype(BF16), vc_ref[...])
    if has_x:
        out = out + _dot((px0 * r0 - px1 * w1).astype(BF16), vx_ref[...])
    y = out * lax.rsqrt(jnp.mean(out * out, axis=-1, keepdims=True) + EPS) * gain_ref[...]
    o_ref[...] = (y * (1.0 - lam_init)).astype(BF16)


def diff_attention(p, lam_vec, gain, n_samples, lx, lc, ret_heads, heads, lam_init, latent, tq):
    qo = 3 * ret_heads
    ko = qo + heads
    vo = ko + heads
    cb = n_samples * lx // lc
    lq = lx if latent else lc
    nq = lq // tq
    q_row = (lambda b, t: b * nq + t) if latent else (lambda b, t: n_samples * lx // tq + b * nq + t)
    in_specs = [
        pl.BlockSpec((4, QK_DIM), lambda b, h, t: (0, 0)),
        pl.BlockSpec((tq, LANES), lambda b, h, t: (q_row(b, t), qo + h)),
        pl.BlockSpec((lc, LANES), lambda b, h, t: (cb + b, ko + h)),
        pl.BlockSpec((lc, LANES), lambda b, h, t: (cb + b, vo + h)),
    ]
    args = [lam_vec, p, p, p]
    if latent:
        in_specs += [pl.BlockSpec((lx, LANES), lambda b, h, t: (b, ko + h)),
                     pl.BlockSpec((lx, LANES), lambda b, h, t: (b, vo + h))]
        args += [p, p]
    in_specs.append(pl.BlockSpec((1, LANES), lambda b, h, t: (0, h)))
    args.append(gain)
    return pl.pallas_call(
        functools.partial(_diff_kernel, lam_init=lam_init, has_x=latent),
        name="diff_attn_x" if latent else "diff_attn_ctx",
        grid=(n_samples, heads, nq),
        in_specs=in_specs,
        out_specs=pl.BlockSpec((tq, LANES), lambda b, h, t: (b * nq + t, h)),
        out_shape=jax.ShapeDtypeStruct((n_samples * lq, heads * V_DIM), BF16),
        compiler_params=_params(("arbitrary", "arbitrary", "arbitrary")),
    )(*args)


def _pack_pair(a, b):
    ua = pltpu.bitcast(a.astype(BF16).astype(F32), I32)
    ub = pltpu.bitcast(b.astype(BF16).astype(F32), I32)
    return ua | lax.shift_right_logical(ub, 16)


def _unpack_pair(w):
    return (pltpu.bitcast(w & -65536, F32).astype(BF16), pltpu.bitcast(w << 16, F32).astype(BF16))


def _store_packed_rows(ref, row0, n_rows, val):
    nw = val.shape[1] // (2 * LANES)
    for j in range(nw):
        w = _pack_pair(val[:, j * LANES:(j + 1) * LANES], val[:, (nw + j) * LANES:(nw + j + 1) * LANES])
        ref[pl.ds(row0 * nw + j, n_rows, stride=nw), :] = w


def _load_packed_rows(ref, row0, n_rows, nw, dst, dst_row0):
    for j in range(nw):
        hi, lo = _unpack_pair(ref[pl.ds(row0 * nw + j, n_rows, stride=nw), :])
        dst[dst_row0:dst_row0 + n_rows, j * LANES:(j + 1) * LANES] = hi
        dst[dst_row0:dst_row0 + n_rows, (nw + j) * LANES:(nw + j + 1) * LANES] = lo


def _outproj_kernel(ret_ref, dif_ref, w_ref, x_ref, mod_ref, g_ref, wr_ref, xm_ref, h_ref, aff_ref, *, d, wr):
    y = _dot(ret_ref[...], w_ref[0:wr, :]) + _dot(dif_ref[...], w_ref[wr:, :])
    m = mod_ref[0]
    xm = x_ref[...] + m[:, 2 * d:3 * d] * y
    xm_ref[...] = xm
    h = _rmsnorm_mod(xm, g_ref[...], m[:, 3 * d:4 * d], m[:, 4 * d:5 * d])
    _store_packed_rows(h_ref, 0, h.shape[0], h)
    logits = lax.dot_general(wr_ref[...], h, (((1,), (1,)), ((), ())),
                             precision=lax.Precision.HIGHEST, preferred_element_type=F32)
    e = jnp.exp(logits - jnp.max(logits, axis=0, keepdims=True))
    aff_ref[...] = e / jnp.sum(e, axis=0, keepdims=True)


def out_projection(ret, dif, w_bf, xa, mod, gain, w_router_t, rows, n_x_tiles, tiles_per_sample, n_samples, tm):
    d = xa.shape[1]
    wr = ret.shape[1]
    e = w_router_t.shape[0]
    nw = d // (2 * LANES)
    mod_idx = lambda i: (jnp.where(i < n_x_tiles, i // tiles_per_sample, n_samples), 0, 0)
    return pl.pallas_call(
        functools.partial(_outproj_kernel, d=d, wr=wr),
        name="out_proj_router",
        grid=(rows // tm,),
        in_specs=[pl.BlockSpec((tm, wr), lambda i: (i, 0)),
                  pl.BlockSpec((tm, dif.shape[1]), lambda i: (i, 0)),
                  pl.BlockSpec(w_bf.shape, lambda i: (0, 0)),
                  pl.BlockSpec((tm, d), lambda i: (i, 0)),
                  pl.BlockSpec((1, 1, N_MOD * d), mod_idx),
                  pl.BlockSpec((1, d), lambda i: (0, 0)),
                  pl.BlockSpec((e, d), lambda i: (0, 0))],
        out_specs=[pl.BlockSpec((tm, d), lambda i: (i, 0)),
                   pl.BlockSpec((tm * nw, LANES), lambda i: (i, 0)),
                   pl.BlockSpec((e, tm), lambda i: (0, i))],
        out_shape=[jax.ShapeDtypeStruct((rows, d), F32),
                   jax.ShapeDtypeStruct((rows * nw, LANES), I32),
                   jax.ShapeDtypeStruct((e, rows), F32)],
        compiler_params=_params(("arbitrary",)),
    )(ret, dif, w_bf, xa, mod, gain, w_router_t)


def _ones(mask, dtype):
    return jnp.where(mask, 1.0, 0.0).astype(dtype)


def _prefix_lanes(m):
    upper = _ones(lax.broadcasted_iota(I32, (LANES, LANES), 0) < lax.broadcasted_iota(I32, (LANES, LANES), 1), BF16)
    run = jnp.zeros((m.shape[0], 1), F32)
    outs = []
    for s in range(m.shape[1] // LANES):
        seg = m[:, s * LANES:(s + 1) * LANES]
        outs.append(_dot(seg, upper) + run)
        run = run + jnp.sum(seg.astype(F32), axis=1, keepdims=True)
    return jnp.concatenate(outs, axis=1)


def _topk_kernel(a_ref, idx_ref, gate_ref, dest_ref, start_ref, cnt_ref, posm_scr, dest_scr,
                 *, lg, cap, row_base, ent_base):
    g = pl.program_id(0)
    a = a_ref[...]
    n_e = a.shape[0]
    bits = pltpu.bitcast(a, I32)

    def bisect(_, lohi):
        lo, hi = lohi
        mid = lo + ((hi - lo) >> 1)
        ge = jnp.sum(_ones(bits >= mid, F32), axis=1, keepdims=True) >= cap
        return jnp.where(ge, mid, lo), jnp.where(ge, hi, mid)

    thr, _ = lax.fori_loop(0, 31, bisect,
                           (jnp.zeros((n_e, 1), I32), jnp.full((n_e, 1), 0x7F800000, I32)))
    gt = bits > thr
    eq = bits == thr
    need = cap - jnp.sum(_ones(gt, F32), axis=1, keepdims=True)
    sel = gt | (eq & (_prefix_lanes(_ones(eq, BF16)) < need))
    selb = _ones(sel, BF16)
    pos = _prefix_lanes(selb)
    start_t = jnp.sum(pos, axis=0, keepdims=True)
    lower = _ones(lax.broadcasted_iota(I32, (n_e, n_e), 0) > lax.broadcasted_iota(I32, (n_e, n_e), 1), BF16)
    base = (ent_base + g * (n_e * cap)).astype(F32)
    start_ref[...] = (start_t + base).astype(I32)
    cnt_ref[...] = jnp.sum(_ones(sel, F32), axis=0, keepdims=True).astype(I32)
    posm_scr[...] = jnp.where(sel, pos, -1.0)
    dest_scr[...] = start_t + base + _dot(lower, selb)

    tvals = (row_base + g * lg + lax.broadcasted_iota(I32, (1, lg), 1)).astype(F32)
    slot = lax.broadcasted_iota(I32, (cap, 1), 0).astype(F32)
    col = lax.broadcasted_iota(I32, (cap, n_e), 1)

    def compact(e, carry):
        ia, ga, da = carry
        m = posm_scr[pl.ds(e, 1), :] == slot
        i_e = jnp.sum(jnp.where(m, tvals, 0.0), axis=1, keepdims=True)
        g_e = jnp.sum(jnp.where(m, a_ref[pl.ds(e, 1), :], 0.0), axis=1, keepdims=True)
        d_e = jnp.sum(jnp.where(m, dest_scr[pl.ds(e, 1), :], 0.0), axis=1, keepdims=True)
        here = col == e
        return jnp.where(here, i_e, ia), jnp.where(here, g_e, ga), jnp.where(here, d_e, da)

    z = jnp.zeros((cap, n_e), F32)
    ia, ga, da = lax.fori_loop(0, n_e, compact, (z, z, z))
    idx_ref[0] = ia.astype(I32)
    gate_ref[0] = ga
    dest_ref[0] = da.astype(I32)


def expert_choice(aff_t, groups, lg, first_block, cap, row_base, ent_base):
    e = aff_t.shape[0]
    return pl.pallas_call(
        functools.partial(_topk_kernel, lg=lg, cap=cap, row_base=row_base, ent_base=ent_base),
        name="expert_choice",
        grid=(groups,),
        in_specs=[pl.BlockSpec((e, lg), lambda g: (0, first_block + g))],
        out_specs=[pl.BlockSpec((1, cap, e), lambda g: (g, 0, 0)),
                   pl.BlockSpec((1, cap, e), lambda g: (g, 0, 0)),
                   pl.BlockSpec((1, cap, e), lambda g: (g, 0, 0)),
                   pl.BlockSpec((1, lg), lambda g: (0, g)),
                   pl.BlockSpec((1, lg), lambda g: (0, g))],
        out_shape=[jax.ShapeDtypeStruct((groups, cap, e), I32),
                   jax.ShapeDtypeStruct((groups, cap, e), F32),
                   jax.ShapeDtypeStruct((groups, cap, e), I32),
                   jax.ShapeDtypeStruct((1, groups * lg), I32),
                   jax.ShapeDtypeStruct((1, groups * lg), I32)],
        scratch_shapes=[pltpu.VMEM((e, lg), F32), pltpu.VMEM((e, lg), F32)],
        compiler_params=_params(("arbitrary",)),
    )(aff_t)


ROWS_PER_STEP = 512


def _gather_kernel(si_ref, src_ref, o_ref, sem, *, nw, per):
    base = pl.program_id(0) * per

    def row_copy(i):
        s = pl.multiple_of(si_ref[base + i] * nw, nw)
        return pltpu.make_async_copy(src_ref.at[pl.ds(s, nw)], o_ref.at[pl.ds(pl.multiple_of(i * nw, nw), nw)], sem)

    def issue(i, carry):
        row_copy(i).start()
        return carry

    def drain(i, carry):
        row_copy(i).wait()
        return carry

    lax.fori_loop(0, per, issue, 0, unroll=8)
    lax.fori_loop(0, per, drain, 0, unroll=8)


def gather_rows(src, si, nw):
    n = si.shape[0]
    per = ROWS_PER_STEP
    assert n % per == 0
    return pl.pallas_call(
        functools.partial(_gather_kernel, nw=nw, per=per),
        name="gather_rows",
        grid_spec=pltpu.PrefetchScalarGridSpec(
            num_scalar_prefetch=1, grid=(n // per,),
            in_specs=[pl.BlockSpec(memory_space=pl.ANY)],
            out_specs=pl.BlockSpec((per * nw, src.shape[1]), lambda g, si: (g, 0)),
            scratch_shapes=[pltpu.SemaphoreType.DMA(())]),
        out_shape=jax.ShapeDtypeStruct((n * nw, src.shape[1]), src.dtype),
        compiler_params=_params(("arbitrary",)),
    )(si, src)


def _scatter_kernel(di_ref, x_ref, dst_ref, sem, *, nw, per):
    base = pl.program_id(0) * per

    def row_copy(i):
        t = pl.multiple_of(di_ref[base + i] * nw, nw)
        return pltpu.make_async_copy(x_ref.at[pl.ds(pl.multiple_of(i * nw, nw), nw)], dst_ref.at[pl.ds(t, nw)], sem)

    def issue(i, carry):
        row_copy(i).start()
        return carry

    def drain(i, carry):
        row_copy(i).wait()
        return carry

    lax.fori_loop(0, per, issue, 0, unroll=8)
    lax.fori_loop(0, per, drain, 0, unroll=8)


def scatter_rows(x, di, nw):
    n = di.shape[0]
    per = ROWS_PER_STEP
    assert n % per == 0
    return pl.pallas_call(
        functools.partial(_scatter_kernel, nw=nw, per=per),
        name="scatter_rows",
        grid_spec=pltpu.PrefetchScalarGridSpec(
            num_scalar_prefetch=1, grid=(n // per,),
            in_specs=[pl.BlockSpec((per * nw, x.shape[1]), lambda g, di: (g, 0))],
            out_specs=pl.BlockSpec(memory_space=pl.ANY),
            scratch_shapes=[pltpu.SemaphoreType.DMA(())]),
        out_shape=jax.ShapeDtypeStruct((n * nw, x.shape[1]), x.dtype),
        compiler_params=_params(("arbitrary",)),
    )(di, x)


def _ffn_kernel(xs_ref, wg_ref, wu_ref, wd_ref, gate_ref, ys_ref, xb_scr, acc_scr, *, ts, nw):
    f = pl.program_id(1)
    s = xb_scr.shape[0]

    @pl.when(f == 0)
    def _():
        _load_packed_rows(xs_ref.at[0], 0, s, nw, xb_scr, 0)

    wg = wg_ref[0, 0].astype(BF16)
    wu = wu_ref[0, 0].astype(BF16)
    wd = wd_ref[0, 0].astype(BF16)
    for r in range(s // ts):
        rows = slice(r * ts, (r + 1) * ts)
        xb = xb_scr[rows, :]
        hid = _silu(_dot(xb, wg)) * _dot(xb, wu)
        y = _dot(hid.astype(BF16), wd)

        @pl.when(f == 0)
        def _():
            acc_scr[rows, :] = y

        @pl.when(f > 0)
        def _():
            acc_scr[rows, :] += y

    @pl.when(f == pl.num_programs(1) - 1)
    def _():
        for r in range(s // ts):
            rows = slice(r * ts, (r + 1) * ts)
            _store_packed_rows(ys_ref.at[0], r * ts, ts, acc_scr[rows, :] * gate_ref[0, rows, :])


def expert_ffn(xs, w_gate, w_up, w_down, layer, gates, d):
    nw = d // (2 * LANES)
    e = xs.shape[0]
    s = xs.shape[1] // nw
    ff = w_gate.shape[3]
    tf = 256 if ff % 256 == 0 else ff
    once = pl.Buffered(1)
    ts = max(t for t in range(8, 577, 8) if s % t == 0)
    return pl.pallas_call(
        functools.partial(_ffn_kernel, ts=ts, nw=nw),
        name="expert_ffn",
        grid=(e, ff // tf),
        in_specs=[pl.BlockSpec((1, s * nw, LANES), lambda i, f: (i, 0, 0)),
                  pl.BlockSpec((1, 1, d, tf), lambda i, f: (layer, i, 0, f)),
                  pl.BlockSpec((1, 1, d, tf), lambda i, f: (layer, i, 0, f)),
                  pl.BlockSpec((1, 1, tf, d), lambda i, f: (layer, i, f, 0)),
                  pl.BlockSpec((1, s, 1), lambda i, f: (i, 0, 0))],
        out_specs=pl.BlockSpec((1, s * nw, LANES), lambda i, f: (i, 0, 0), pipeline_mode=once),
        out_shape=jax.ShapeDtypeStruct((e, s * nw, LANES), I32),
        scratch_shapes=[pltpu.VMEM((s, d), BF16), pltpu.VMEM((s, d), F32)],
        compiler_params=_params(("arbitrary", "arbitrary")),
    )(xs, w_gate, w_up, w_down, gates)


COMB_ROWS = 128
COMB_ENT = 256


def _combine_kernel(kb_ref, ch_ref, flag_ref, ys_ref, xm_ref, start_ref, cnt_ref, mod_ref, *rest, d, final):
    if final:
        gf_ref, o_ref, chunk_scr = rest
    else:
        o_ref, chunk_scr = rest
    p = pl.program_id(0)
    flags = flag_ref[p]

    @pl.when((flags & 2) != 0)
    def _():
        o_ref[...] = jnp.zeros_like(o_ref)

    @pl.when((flags & 8) != 0)
    def _():
        _load_packed_rows(ys_ref, 0, COMB_ENT, d // (2 * LANES), chunk_scr, 0)

    @pl.when((flags & 1) != 0)
    def _():
        gpos = ch_ref[p] * COMB_ENT + lax.broadcasted_iota(I32, (COMB_ROWS, COMB_ENT), 1)
        st = start_ref[...]
        onehot = _ones((gpos >= st) & (gpos < st + cnt_ref[...]), BF16)
        o_ref[...] += _dot(onehot, chunk_scr[...])

    @pl.when((flags & 4) != 0)
    def _():
        m = mod_ref[0]
        out = xm_ref[...] + m[:, 5 * d:6 * d] * o_ref[...]
        if final:
            out = out * lax.rsqrt(jnp.mean(out * out, axis=-1, keepdims=True) + EPS) * gf_ref[...]
        o_ref[...] = out


def combine(ys_sorted, xm, start_col, cnt_col, mod, kb, ch, flags, rows, tiles_per_sample, n_x_tiles,
            n_samples, final_gain):
    d = xm.shape[1]
    nw = d // (2 * LANES)
    n_pairs = kb.shape[0]
    mod_idx = lambda p, kb, ch, fl: (jnp.where(kb[p] < n_x_tiles, kb[p] // tiles_per_sample, n_samples), 0, 0)
    in_specs = [pl.BlockSpec((COMB_ENT * nw, LANES), lambda p, kb, ch, fl: (ch[p], 0)),
                pl.BlockSpec((COMB_ROWS, d), lambda p, kb, ch, fl: (kb[p], 0)),
                pl.BlockSpec((COMB_ROWS, 1), lambda p, kb, ch, fl: (kb[p], 0)),
                pl.BlockSpec((COMB_ROWS, 1), lambda p, kb, ch, fl: (kb[p], 0)),
                pl.BlockSpec((1, 1, N_MOD * d), mod_idx)]
    args = [ys_sorted, xm, start_col, cnt_col, mod]
    if final_gain is not None:
        in_specs.append(pl.BlockSpec((1, d), lambda p, kb, ch, fl: (0, 0)))
        args.append(final_gain)
    return pl.pallas_call(
        functools.partial(_combine_kernel, d=d, final=final_gain is not None),
        name="combine",
        grid_spec=pltpu.PrefetchScalarGridSpec(
            num_scalar_prefetch=3, grid=(n_pairs,),
            in_specs=in_specs,
            out_specs=pl.BlockSpec((COMB_ROWS, d), lambda p, kb, ch, fl: (kb[p], 0)),
            scratch_shapes=[pltpu.VMEM((COMB_ENT, d), BF16)]),
        out_shape=jax.ShapeDtypeStruct((rows, d), F32),
        compiler_params=_params(("arbitrary",)),
    )(kb, ch, flags, *args)


def _combine_schedule(start_row, n_blocks, n_chunks):
    blk_start = start_row[::COMB_ROWS]
    blk_end = jnp.concatenate([blk_start[1:], jnp.full((1,), n_chunks * COMB_ENT, I32)])
    lo = jnp.minimum(blk_start // COMB_ENT, n_chunks - 1)
    hi = jnp.maximum(lo, (blk_end - 1) // COMB_ENT)
    cnt = hi - lo + 1
    off = jnp.cumsum(cnt) - cnt
    total = off[-1] + cnt[-1]
    n_pairs = n_blocks + n_chunks
    pidx = jnp.arange(n_pairs, dtype=I32)
    kb = jnp.sum((off[None, :] <= pidx[:, None]).astype(I32), axis=1) - 1
    valid = pidx < total
    ch = jnp.where(valid, lo[kb] + pidx - off[kb], hi[n_blocks - 1]).astype(I32)
    first = valid & (pidx == off[kb])
    last = valid & (pidx == off[kb] + cnt[kb] - 1)
    fresh = jnp.concatenate([jnp.ones((1,), bool), ch[1:] != ch[:-1]])
    flags = valid.astype(I32) + 2 * first.astype(I32) + 4 * last.astype(I32) + 8 * fresh.astype(I32)
    return kb, ch, flags


def _rope_tables(lx, n_ctx_rows, n_samples):
    n_freq = QK_DIM // 4
    freqs = ROPE_BASE ** (-jnp.arange(n_freq, dtype=F32) / n_freq)
    rows = lx // GRID_W
    row = jnp.repeat(jnp.arange(rows, dtype=F32), GRID_W)
    col = jnp.tile(jnp.arange(GRID_W, dtype=F32), rows)
    ang = jnp.concatenate([row[:, None] * freqs, col[:, None] * freqs], axis=-1)
    cos, sin = jnp.cos(ang), jnp.sin(ang)
    ct = jnp.tile(jnp.concatenate([cos, cos], axis=-1), (n_samples, LANES // QK_DIM))
    st = jnp.tile(jnp.concatenate([-sin, sin], axis=-1), (n_samples, LANES // QK_DIM))
    ct = jnp.concatenate([ct, jnp.ones((n_ctx_rows, LANES), F32)], axis=0)
    st = jnp.concatenate([st, jnp.zeros((n_ctx_rows, LANES), F32)], axis=0)
    return ct, st


def kernel(x, c, ctx, c_ctx, w_ada, b_ada, norm_mix, norm_ffn, w_in, w_out, ret_log_decay, ret_norm,
           diff_lambda, diff_norm, w_router, w_gate, w_up, w_down, norm_final):
    n_b, lx, d = x.shape
    lc = ctx.shape[1]
    depth = w_ada.shape[0]
    heads = ret_log_decay.shape[-1]
    dheads = diff_norm.shape[-1] // V_DIM
    n_e = w_router.shape[-1]
    rx, rc = n_b * lx, n_b * lc
    cap_x = CAPACITY_FACTOR * lx // n_e
    cap_c = CAPACITY_FACTOR * lc // n_e
    tm = 512 if (lx % 512 == 0 and rx % 512 == 0 and rc % 512 == 0) else 256
    tmo = 256
    assert lx % lc == 0 and lc % tmo == 0 and heads % 2 == 0 and n_b < MOD_ROWS

    qk_scale = QK_DIM ** -0.5
    n_rq = heads * QK_DIM // LANES
    kinds = ([(True, qk_scale)] * n_rq + [(True, 1.0)] * n_rq + [(False, 1.0)] * (2 * heads)
             + [(True, qk_scale)] * dheads + [(True, 1.0)] * dheads + [(False, 1.0)] * dheads)

    xa = jnp.concatenate([x.reshape(rx, d), ctx.reshape(rc, d)], axis=0)
    cc = jnp.concatenate([c, c_ctx[None], jnp.zeros((MOD_ROWS - n_b - 1, d), F32)], axis=0)
    mods = ada_modulation(cc, w_ada, b_ada)
    ct, st = _rope_tables(lx, rc, n_b)
    log_g = jnp.log1p(-jnp.exp(ret_log_decay.astype(F32)))
    out = None

    for layer in range(depth):
        last = layer == depth - 1
        lam_init = 0.8 - 0.6 * math.exp(-0.3 * layer)
        mod = mods[layer].reshape(MOD_ROWS, 1, N_MOD * d)
        p = in_projection(xa, mod, norm_mix[layer][None], w_in[layer].astype(BF16), ct, st,
                          rx // tm, lx // tm, n_b, kinds, tm)

        ret = retention(p, log_g[layer], ret_norm[layer][None], n_b, lx, lc, heads, not last)
        dif_x = diff_attention(p, diff_lambda[layer], diff_norm[layer][None], n_b, lx, lc, heads, dheads,
                               lam_init, True, 256)
        if last:
            rows = rx
            ret_a, dif_a = ret[0], dif_x
        else:
            rows = rx + rc
            dif_c = diff_attention(p, diff_lambda[layer], diff_norm[layer][None], n_b, lx, lc, heads, dheads,
                                   lam_init, False, lc)
            ret_a = jnp.concatenate([ret[0], ret[1]], axis=0)
            dif_a = jnp.concatenate([dif_x, dif_c], axis=0)

        xm, h2, aff_t = out_projection(ret_a, dif_a, w_out[layer].astype(BF16), xa, mod, norm_ffn[layer][None],
                                       w_router[layer].T, rows, rx // tmo, lx // tmo, n_b, tmo)

        sel = [expert_choice(aff_t, n_b, lx, 0, cap_x, 0, 0)]
        if not last:
            sel.append(expert_choice(aff_t, n_b, lc, rx // lc, cap_c, rx, n_b * n_e * cap_x))
        to_es = lambda t: jnp.transpose(t, (2, 0, 1)).reshape(n_e, -1)
        idx = jnp.concatenate([to_es(s[0]) for s in sel], axis=1)
        gates = jnp.concatenate([to_es(s[1]) for s in sel], axis=1)
        dest = jnp.concatenate([to_es(s[2]) for s in sel], axis=1)
        start_row = jnp.concatenate([s[3][0] for s in sel])
        cnt_row = jnp.concatenate([s[4][0] for s in sel])
        slots = idx.shape[1]
        n_ent = n_e * slots

        nw = d // (2 * LANES)
        xs = gather_rows(h2, idx.reshape(-1), nw)
        ys = expert_ffn(xs.reshape(n_e, slots * nw, LANES), w_gate, w_up, w_down, layer,
                        gates.reshape(n_e, slots, 1), d)
        ys_sorted = scatter_rows(ys.reshape(n_ent * nw, LANES), dest.reshape(-1), nw)

        kb, ch, flags = _combine_schedule(start_row, rows // COMB_ROWS, n_ent // COMB_ENT)
        xa_new = combine(ys_sorted, xm, start_row[:, None], cnt_row[:, None], mod, kb, ch, flags, rows,
                         lx // COMB_ROWS, rx // COMB_ROWS, n_b, norm_final[None] if last else None)
        if last:
            out = xa_new
        else:
            xa = xa_new
    return out.reshape(n_b, lx, d)
```

```python
import functools
import math

import jax
import jax.numpy as jnp
from jax import lax
from jax.experimental import pallas as pl
from jax.experimental.pallas import tpu as pltpu

F32 = jnp.float32
BF16 = jnp.bfloat16
I32 = jnp.int32

EPS = 1e-6
N_MOD = 6
GRID_W = 64
ROPE_BASE = 10000.0
QK_DIM = 64
V_DIM = 128
LANES = 128
CHUNK = 128
CAPACITY_FACTOR = 2
MOD_ROWS = 8
VMEM_LIMIT = 56 * 1024 * 1024


def _params(sem, vmem=VMEM_LIMIT):
    return pltpu.CompilerParams(dimension_semantics=sem, vmem_limit_bytes=vmem)


def _silu(x):
    return x / (1.0 + jnp.exp(-x))


def _dot(a, b):
    return jnp.dot(a, b, preferred_element_type=F32)


def _dot_nt(a, b):
    return lax.dot_general(a, b, (((1,), (1,)), ((), ())), preferred_element_type=F32)


def _ada_kernel(c_ref, w_ref, b_ref, o_ref):
    s = _silu(c_ref[...]).astype(BF16)
    o_ref[0] = _dot(s, w_ref[0].astype(BF16)) + b_ref[0]


def ada_modulation(cc, w_ada, b_ada):
    depth, d, n = w_ada.shape
    tn = 1024 if n % 1024 == 0 else n
    return pl.pallas_call(
        _ada_kernel,
        name="ada_mod",
        grid=(depth, n // tn),
        in_specs=[pl.BlockSpec((MOD_ROWS, d), lambda l, j: (0, 0)),
                  pl.BlockSpec((1, d, tn), lambda l, j: (l, 0, j)),
                  pl.BlockSpec((1, 1, tn), lambda l, j: (l, 0, j))],
        out_specs=pl.BlockSpec((1, MOD_ROWS, tn), lambda l, j: (l, 0, j)),
        out_shape=jax.ShapeDtypeStruct((depth, MOD_ROWS, n), F32),
        compiler_params=_params(("arbitrary", "arbitrary")),
    )(cc, w_ada, b_ada.reshape(depth, 1, n))


def _rmsnorm_mod(x, gain, shift, scale):
    y = x * lax.rsqrt(jnp.mean(x * x, axis=-1, keepdims=True) + EPS) * gain
    return y * (1.0 + scale) + shift


MXU_N = 256


def _group_of(lane):
    return (lane >> 5) & 1


def _inproj_kernel(x_ref, mod_ref, g_ref, w_ref, ct_ref, st_ref, o_ref, *, d, rope):
    m = mod_ref[0]
    h = _rmsnorm_mod(x_ref[...], g_ref[...], m[:, 0:d], m[:, d:2 * d]).astype(BF16)
    per = MXU_N // LANES
    for s in range(w_ref.shape[1] // MXU_N):
        acc = _dot(h, w_ref[:, s * MXU_N:(s + 1) * MXU_N])
        for t in range(per):
            a = acc[:, t * LANES:(t + 1) * LANES]
            if rope[s * per + t]:
                a = a * ct_ref[...] + pltpu.roll(a, LANES // 2, 1) * st_ref[...]
            c0 = s * MXU_N + t * LANES
            o_ref[:, c0:c0 + LANES] = a.astype(BF16)


def in_projection(xa, mod, gain, w_bf, ct, st, n_x_tiles, tiles_per_sample, n_samples, rope, tm):
    r, d = xa.shape
    n = w_bf.shape[1]
    mod_idx = lambda i: (jnp.where(i < n_x_tiles, i // tiles_per_sample, n_samples), 0, 0)
    return pl.pallas_call(
        functools.partial(_inproj_kernel, d=d, rope=rope),
        name="in_proj",
        grid=(r // tm,),
        in_specs=[pl.BlockSpec((tm, d), lambda i: (i, 0)),
                  pl.BlockSpec((1, 1, N_MOD * d), mod_idx),
                  pl.BlockSpec((1, d), lambda i: (0, 0)),
                  pl.BlockSpec((d, n), lambda i: (0, 0), pipeline_mode=pl.Buffered(1)),
                  pl.BlockSpec((tm, LANES), lambda i: (i, 0)),
                  pl.BlockSpec((tm, LANES), lambda i: (i, 0))],
        out_specs=pl.BlockSpec((tm, n), lambda i: (i, 0)),
        out_shape=jax.ShapeDtypeStruct((r, n), BF16),
        compiler_params=_params(("arbitrary",)),
    )(xa, mod, gain, w_bf, ct, st)


def _prep_w_in(w_in, rope, scale_slabs, scale):
    depth, d, n = w_in.shape
    q = LANES // 4
    w = w_in.reshape(depth, d, n // LANES, 2, 2, q)
    is_rope = jnp.asarray(rope)[None, None, :, None, None, None]
    w = jnp.where(is_rope, jnp.swapaxes(w, 3, 4), w)
    col_scale = jnp.where(jnp.asarray(scale_slabs), scale, 1.0).astype(F32)[None, None, :, None, None, None]
    return (w * col_scale).reshape(depth, d, n).astype(BF16)


def _ret_kernel(lg_ref, qx_ref, kx_ref, vx_ref, gx_ref, qc_ref, kc_ref, vc_ref, gc_ref, gain_ref,
                *rest, lx, lc, ctx_out):
    if ctx_out:
        ox_ref, oc_ref, accx, accc = rest
    else:
        ox_ref, accx = rest
    pair = pl.program_id(1)
    w2 = 2 * V_DIM
    lane = lax.broadcasted_iota(I32, (CHUNK, LANES), 1)
    dif = (lax.broadcasted_iota(I32, (CHUNK, CHUNK), 0) - lax.broadcasted_iota(I32, (CHUNK, CHUNK), 1)).astype(F32)
    pos = lax.broadcasted_iota(I32, (CHUNK, 1), 0).astype(F32)
    cpos = lax.broadcasted_iota(I32, (lc, 1), 0).astype(F32)
    c_len = float(CHUNK)
    lgf = [lg_ref[0, 2 * pair + hh] for hh in range(2)]
    lgb = [lg_ref[1, 2 * pair + hh] for hh in range(2)]

    def per_head(fn):
        vals = [fn(hh) for hh in range(2)]
        return jnp.concatenate([jnp.broadcast_to(v, (v.shape[0], V_DIM)) for v in vals], axis=1)

    decay = per_head(lambda hh: jnp.where(dif >= 0, jnp.exp(lgf[hh] * jnp.maximum(dif, 0.0)),
                                          jnp.exp(lgb[hh] * jnp.maximum(-dif, 0.0))))
    qd_f = per_head(lambda hh: jnp.exp(lgf[hh] * (pos + 1.0)))
    kd_f = per_head(lambda hh: jnp.exp(lgf[hh] * (c_len - 1.0 - pos)))
    qd_b = per_head(lambda hh: jnp.exp(lgb[hh] * (c_len - pos)))
    kd_b = per_head(lambda hh: jnp.exp(lgb[hh] * pos))
    one = jnp.ones((1, 1), F32)
    cd_f = per_head(lambda hh: jnp.exp(lgf[hh] * c_len) * one)
    cd_b = per_head(lambda hh: jnp.exp(lgb[hh] * c_len) * one)
    srow = lax.broadcasted_iota(I32, (LANES, w2), 0)
    scol = lax.broadcasted_iota(I32, (LANES, w2), 1)
    diag = (_group_of(srow) == 0) == (scol < V_DIM)
    zeros_v = jnp.zeros((CHUNK, V_DIM), BF16)

    def kv_state(k, v_scaled):
        return jnp.where(diag, _dot(k.astype(F32).T.astype(BF16), v_scaled.astype(BF16)), 0.0)

    kc = kc_ref[...]
    vc = vc_ref[...].astype(F32)
    s_f = kv_state(kc, vc * per_head(lambda hh: jnp.exp(lgf[hh] * (lc - 1.0 - cpos))))
    s_b = kv_state(kc, vc * per_head(lambda hh: jnp.exp(lgb[hh] * cpos)))

    def sweeps(q_ref, k_ref, v_ref, g_ref, o_ref, acc_ref, n, s_f0, s_b0):
        def rows_of(c):
            return pl.ds(pl.multiple_of(c * CHUNK, CHUNK), CHUNK)

        def forward(c, s):
            rows = rows_of(c)
            q, k, v = q_ref[rows, :], k_ref[rows, :], v_ref[rows, :]
            k_bd = jnp.concatenate([jnp.where(_group_of(lane) == 0, k, 0), jnp.where(_group_of(lane) == 1, k, 0)],
                                   axis=0)
            v_bd = jnp.concatenate([jnp.concatenate([v[:, :V_DIM], zeros_v], axis=1),
                                    jnp.concatenate([zeros_v, v[:, V_DIM:]], axis=1)], axis=0)
            sc = _dot_nt(q, k_bd) * decay
            acc_ref[rows, :] = _dot(sc.astype(BF16), v_bd) + _dot(q, s.astype(BF16)) * qd_f
            return s * cd_f + kv_state(k, v.astype(F32) * kd_f)

        def backward(t, s):
            rows = rows_of(n - 1 - t)
            q, k, v = q_ref[rows, :], k_ref[rows, :], v_ref[rows, :]
            y = acc_ref[rows, :] + _dot(q, s.astype(BF16)) * qd_b
            for hh in range(2):
                vsl = slice(hh * V_DIM, (hh + 1) * V_DIM)
                yh = y[:, vsl]
                yc = yh - jnp.mean(yh, axis=-1, keepdims=True)
                out = yc * lax.rsqrt(jnp.mean(yc * yc, axis=-1, keepdims=True) + EPS) * gain_ref[:, vsl]
                o_ref[rows, vsl] = (_silu(g_ref[rows, vsl].astype(F32)) * out).astype(BF16)
            return s * cd_b + kv_state(k, v.astype(F32) * kd_b)

        unroll = 4 if n % 4 == 0 else n
        lax.fori_loop(0, n, forward, s_f0, unroll=unroll)
        lax.fori_loop(0, n, backward, s_b0, unroll=unroll)

    sweeps(qx_ref, kx_ref, vx_ref, gx_ref, ox_ref, accx, lx // CHUNK, s_f, s_b)
    if ctx_out:
        zero = jnp.zeros((LANES, w2), F32)
        sweeps(qc_ref, kc_ref, vc_ref, gc_ref, oc_ref, accc, lc // CHUNK, zero, zero)


def retention(p, lg, gain, n_samples, lx, lc, heads, ctx_out):
    pairs = heads // 2
    half = heads // 2
    cb = n_samples * lx // lc
    w2 = 2 * V_DIM
    in_specs = [
        pl.BlockSpec(memory_space=pltpu.SMEM),
        pl.BlockSpec((lx, LANES), lambda b, h: (b, h)),
        pl.BlockSpec((lx, LANES), lambda b, h: (b, half + h)),
        pl.BlockSpec((lx, w2), lambda b, h: (b, half + h)),
        pl.BlockSpec((lx, w2), lambda b, h: (b, heads + h)),
        pl.BlockSpec((lc, LANES), lambda b, h: (cb + b, h)),
        pl.BlockSpec((lc, LANES), lambda b, h: (cb + b, half + h)),
        pl.BlockSpec((lc, w2), lambda b, h: (cb + b, half + h)),
        pl.BlockSpec((lc, w2), lambda b, h: (cb + b, heads + h)),
        pl.BlockSpec((1, w2), lambda b, h: (0, h)),
    ]
    out_specs = [pl.BlockSpec((lx, w2), lambda b, h: (b, h))]
    out_shape = [jax.ShapeDtypeStruct((n_samples * lx, heads * V_DIM), BF16)]
    scratch = [pltpu.VMEM((lx, w2), F32)]
    if ctx_out:
        out_specs.append(pl.BlockSpec((lc, w2), lambda b, h: (b, h)))
        out_shape.append(jax.ShapeDtypeStruct((n_samples * lc, heads * V_DIM), BF16))
        scratch.append(pltpu.VMEM((lc, w2), F32))
    return pl.pallas_call(
        functools.partial(_ret_kernel, lx=lx, lc=lc, ctx_out=ctx_out),
        name="retention",
        grid=(n_samples, pairs),
        in_specs=in_specs, out_specs=out_specs, out_shape=out_shape, scratch_shapes=scratch,
        compiler_params=_params(("arbitrary", "arbitrary")),
    )(lg, p, p, p, p, p, p, p, p, gain)


def _diff_kernel(lam_ref, q_ref, kc_ref, vc_ref, *rest, lam_init, has_x, kb):
    if has_x:
        kx_ref, vx_ref, gain_ref, o_ref = rest
    else:
        gain_ref, o_ref = rest
    lv = lam_ref[...]
    lam = (jnp.exp(jnp.sum(lv[0:1] * lv[1:2], axis=-1, keepdims=True))
           - jnp.exp(jnp.sum(lv[2:3] * lv[3:4], axis=-1, keepdims=True)) + lam_init)
    q = q_ref[...]
    tq = q.shape[0]
    lane = lax.broadcasted_iota(I32, q.shape, 1)
    blocks = [(kc_ref, vc_ref, 0, kc_ref.shape[0])]
    if has_x:
        blocks += [(kx_ref, vx_ref, r0, kb) for r0 in range(0, kx_ref.shape[0], kb)]
    v_ext = [jnp.concatenate([v_ref[r0:r0 + n, :], jnp.ones((n, V_DIM), BF16)], axis=1) for _, v_ref, r0, n in blocks]
    heads_out = []
    for t in range(2):
        qm = jnp.where(_group_of(lane) == t, q, 0)
        m = jnp.full((tq, 1), -jnp.inf, F32)
        acc = jnp.zeros((tq, 2 * V_DIM), F32)
        for (k_ref, _, r0, n), ve in zip(blocks, v_ext):
            s = _dot_nt(qm, k_ref[r0:r0 + n, :])
            m_new = jnp.maximum(m, jnp.max(s, axis=-1, keepdims=True))
            acc = acc * jnp.exp(m - m_new) + _dot(jnp.exp(s - m_new).astype(BF16), ve)
            m = m_new
        heads_out.append(acc[:, :V_DIM] * (1.0 / acc[:, V_DIM:V_DIM + 1]))
    out = heads_out[0] - lam * heads_out[1]
    y = out * lax.rsqrt(jnp.mean(out * out, axis=-1, keepdims=True) + EPS) * gain_ref[...]
    o_ref[...] = (y * (1.0 - lam_init)).astype(BF16)


def diff_attention(p, lam_vec, gain, n_samples, lx, lc, ret_heads, heads, lam_init, latent, tq):
    qo = 3 * ret_heads
    ko = qo + heads
    vo = ko + heads
    cb = n_samples * lx // lc
    lq = lx if latent else lc
    nq = lq // tq
    q_row = (lambda b, t: b * nq + t) if latent else (lambda b, t: n_samples * lx // tq + b * nq + t)
    in_specs = [
        pl.BlockSpec((4, QK_DIM), lambda b, h, t: (0, 0)),
        pl.BlockSpec((tq, LANES), lambda b, h, t: (q_row(b, t), qo + h)),
        pl.BlockSpec((lc, LANES), lambda b, h, t: (cb + b, ko + h)),
        pl.BlockSpec((lc, LANES), lambda b, h, t: (cb + b, vo + h)),
    ]
    args = [lam_vec, p, p, p]
    if latent:
        in_specs += [pl.BlockSpec((lx, LANES), lambda b, h, t: (b, ko + h)),
                     pl.BlockSpec((lx, LANES), lambda b, h, t: (b, vo + h))]
        args += [p, p]
    in_specs.append(pl.BlockSpec((1, LANES), lambda b, h, t: (0, h)))
    args.append(gain)
    return pl.pallas_call(
        functools.partial(_diff_kernel, lam_init=lam_init, has_x=latent, kb=min(512, lx)),
        name="diff_attn_x" if latent else "diff_attn_ctx",
        grid=(n_samples, heads, nq),
        in_specs=in_specs,
        out_specs=pl.BlockSpec((tq, LANES), lambda b, h, t: (b * nq + t, h)),
        out_shape=jax.ShapeDtypeStruct((n_samples * lq, heads * V_DIM), BF16),
        compiler_params=_params(("arbitrary", "arbitrary", "arbitrary")),
    )(*args)


def _pack_pair(a, b):
    ua = pltpu.bitcast(a.astype(BF16).astype(F32), I32)
    ub = pltpu.bitcast(b.astype(BF16).astype(F32), I32)
    return ua | lax.shift_right_logical(ub, 16)


def _unpack_pair(w):
    return (pltpu.bitcast(w & -65536, F32).astype(BF16), pltpu.bitcast(w << 16, F32).astype(BF16))


def _store_packed_rows(ref, row0, n_rows, val):
    nw = val.shape[1] // (2 * LANES)
    for j in range(nw):
        w = _pack_pair(val[:, j * LANES:(j + 1) * LANES], val[:, (nw + j) * LANES:(nw + j + 1) * LANES])
        ref[pl.ds(row0 * nw + j, n_rows, stride=nw), :] = w


def _load_packed_rows(ref, row0, n_rows, nw, dst, dst_row0):
    for j in range(nw):
        hi, lo = _unpack_pair(ref[pl.ds(row0 * nw + j, n_rows, stride=nw), :])
        dst[dst_row0:dst_row0 + n_rows, j * LANES:(j + 1) * LANES] = hi
        dst[dst_row0:dst_row0 + n_rows, (nw + j) * LANES:(nw + j + 1) * LANES] = lo


def _outproj_kernel(ret_ref, dif_ref, w_ref, x_ref, mod_ref, g_ref, wr_ref, xm_ref, h_ref, aff_ref, *, d, wr):
    y = _dot(ret_ref[...], w_ref[0:wr, :]) + _dot(dif_ref[...], w_ref[wr:, :])
    m = mod_ref[0]
    xm = x_ref[...] + m[:, 2 * d:3 * d] * y
    xm_ref[...] = xm
    h = _rmsnorm_mod(xm, g_ref[...], m[:, 3 * d:4 * d], m[:, 4 * d:5 * d])
    _store_packed_rows(h_ref, 0, h.shape[0], h)
    n_e = wr_ref.shape[1] // 2
    h_hi = h.astype(BF16)
    h_lo = (h - h_hi.astype(F32)).astype(BF16)
    both = _dot(h_hi, wr_ref[...])
    logits = both[:, :n_e] + both[:, n_e:] + _dot(h_lo, wr_ref[...])[:, :n_e]
    e = jnp.exp(logits - jnp.max(logits, axis=-1, keepdims=True))
    aff_ref[...] = e / jnp.sum(e, axis=-1, keepdims=True)


def out_projection(ret, dif, w_bf, xa, mod, gain, w_router, rows, n_x_tiles, tiles_per_sample, n_samples, tm):
    d = xa.shape[1]
    wr = ret.shape[1]
    e = w_router.shape[1]
    nw = d // (2 * LANES)
    wr_hi = w_router.astype(BF16)
    wr_split = jnp.concatenate([wr_hi, (w_router - wr_hi.astype(F32)).astype(BF16)], axis=1)
    mod_idx = lambda i: (jnp.where(i < n_x_tiles, i // tiles_per_sample, n_samples), 0, 0)
    return pl.pallas_call(
        functools.partial(_outproj_kernel, d=d, wr=wr),
        name="out_proj_router",
        grid=(rows // tm,),
        in_specs=[pl.BlockSpec((tm, wr), lambda i: (i, 0)),
                  pl.BlockSpec((tm, dif.shape[1]), lambda i: (i, 0)),
                  pl.BlockSpec(w_bf.shape, lambda i: (0, 0)),
                  pl.BlockSpec((tm, d), lambda i: (i, 0)),
                  pl.BlockSpec((1, 1, N_MOD * d), mod_idx),
                  pl.BlockSpec((1, d), lambda i: (0, 0)),
                  pl.BlockSpec((d, 2 * e), lambda i: (0, 0))],
        out_specs=[pl.BlockSpec((tm, d), lambda i: (i, 0)),
                   pl.BlockSpec((tm * nw, LANES), lambda i: (i, 0)),
                   pl.BlockSpec((tm, e), lambda i: (i, 0))],
        out_shape=[jax.ShapeDtypeStruct((rows, d), F32),
                   jax.ShapeDtypeStruct((rows * nw, LANES), I32),
                   jax.ShapeDtypeStruct((rows, e), F32)],
        compiler_params=_params(("arbitrary",)),
    )(ret, dif, w_bf, xa, mod, gain, wr_split)


def _ones(mask, dtype):
    return jnp.where(mask, 1.0, 0.0).astype(dtype)


def _prefix_lanes(m):
    upper = _ones(lax.broadcasted_iota(I32, (LANES, LANES), 0) < lax.broadcasted_iota(I32, (LANES, LANES), 1), BF16)
    run = jnp.zeros((m.shape[0], 1), F32)
    outs = []
    for s in range(m.shape[1] // LANES):
        seg = m[:, s * LANES:(s + 1) * LANES]
        outs.append(_dot(seg, upper) + run)
        run = run + jnp.sum(seg.astype(F32), axis=1, keepdims=True)
    return jnp.concatenate(outs, axis=1)


def _topk_kernel(a_ref, idx_ref, gate_ref, dest_ref, start_ref, cnt_ref, posm_scr, dest_scr,
                 *, lg, cap, row_base, ent_base):
    g = pl.program_id(0)
    a = a_ref[...]
    n_e = a.shape[0]
    bits = pltpu.bitcast(a, I32)

    def bisect(_, lohi):
        lo, hi = lohi
        mid = lo + ((hi - lo) >> 1)
        ge = jnp.sum(_ones(bits >= mid, F32), axis=1, keepdims=True) >= cap
        return jnp.where(ge, mid, lo), jnp.where(ge, hi, mid)

    thr, _ = lax.fori_loop(0, 31, bisect,
                           (jnp.zeros((n_e, 1), I32), jnp.full((n_e, 1), 0x7F800000, I32)))
    gt = bits > thr
    eq = bits == thr
    need = cap - jnp.sum(_ones(gt, F32), axis=1, keepdims=True)
    sel = gt | (eq & (_prefix_lanes(_ones(eq, BF16)) < need))
    selb = _ones(sel, BF16)
    pos = _prefix_lanes(selb)
    start_t = jnp.sum(pos, axis=0, keepdims=True)
    lower = _ones(lax.broadcasted_iota(I32, (n_e, n_e), 0) > lax.broadcasted_iota(I32, (n_e, n_e), 1), BF16)
    base = (ent_base + g * (n_e * cap)).astype(F32)
    start_ref[...] = (start_t + base).astype(I32)
    cnt_ref[...] = jnp.sum(_ones(sel, F32), axis=0, keepdims=True).astype(I32)
    posm_scr[...] = jnp.where(sel, pos, -1.0)
    dest_scr[...] = start_t + base + _dot(lower, selb)

    tvals = (row_base + g * lg + lax.broadcasted_iota(I32, (1, lg), 1)).astype(F32)
    slot = lax.broadcasted_iota(I32, (cap, 1), 0).astype(F32)
    col = lax.broadcasted_iota(I32, (cap, n_e), 1)

    def compact(e, carry):
        ia, ga, da = carry
        m = posm_scr[pl.ds(e, 1), :] == slot
        i_e = jnp.sum(jnp.where(m, tvals, 0.0), axis=1, keepdims=True)
        g_e = jnp.sum(jnp.where(m, a_ref[pl.ds(e, 1), :], 0.0), axis=1, keepdims=True)
        d_e = jnp.sum(jnp.where(m, dest_scr[pl.ds(e, 1), :], 0.0), axis=1, keepdims=True)
        here = col == e
        return jnp.where(here, i_e, ia), jnp.where(here, g_e, ga), jnp.where(here, d_e, da)

    z = jnp.zeros((cap, n_e), F32)
    ia, ga, da = lax.fori_loop(0, n_e, compact, (z, z, z))
    idx_ref[0] = ia.astype(I32)
    gate_ref[0] = ga
    dest_ref[0] = da.astype(I32)


def expert_choice(aff_t, groups, lg, first_block, cap, row_base, ent_base):
    e = aff_t.shape[0]
    return pl.pallas_call(
        functools.partial(_topk_kernel, lg=lg, cap=cap, row_base=row_base, ent_base=ent_base),
        name="expert_choice",
        grid=(groups,),
        in_specs=[pl.BlockSpec((e, lg), lambda g: (0, first_block + g))],
        out_specs=[pl.BlockSpec((1, cap, e), lambda g: (g, 0, 0)),
                   pl.BlockSpec((1, cap, e), lambda g: (g, 0, 0)),
                   pl.BlockSpec((1, cap, e), lambda g: (g, 0, 0)),
                   pl.BlockSpec((1, lg), lambda g: (0, g)),
                   pl.BlockSpec((1, lg), lambda g: (0, g))],
        out_shape=[jax.ShapeDtypeStruct((groups, cap, e), I32),
                   jax.ShapeDtypeStruct((groups, cap, e), F32),
                   jax.ShapeDtypeStruct((groups, cap, e), I32),
                   jax.ShapeDtypeStruct((1, groups * lg), I32),
                   jax.ShapeDtypeStruct((1, groups * lg), I32)],
        scratch_shapes=[pltpu.VMEM((e, lg), F32), pltpu.VMEM((e, lg), F32)],
        compiler_params=_params(("arbitrary",)),
    )(aff_t)


ROWS_PER_STEP = 512


def _gather_kernel(si_ref, src_ref, o_ref, sem, *, nw, per):
    base = pl.program_id(0) * per

    def row_copy(i):
        s = pl.multiple_of(si_ref[base + i] * nw, nw)
        return pltpu.make_async_copy(src_ref.at[pl.ds(s, nw)], o_ref.at[pl.ds(pl.multiple_of(i * nw, nw), nw)], sem)

    def issue(i, carry):
        row_copy(i).start()
        return carry

    def drain(i, carry):
        row_copy(i).wait()
        return carry

    lax.fori_loop(0, per, issue, 0, unroll=8)
    lax.fori_loop(0, per, drain, 0, unroll=8)


def gather_rows(src, si, nw):
    n = si.shape[0]
    per = ROWS_PER_STEP
    assert n % per == 0
    return pl.pallas_call(
        functools.partial(_gather_kernel, nw=nw, per=per),
        name="gather_rows",
        grid_spec=pltpu.PrefetchScalarGridSpec(
            num_scalar_prefetch=1, grid=(n // per,),
            in_specs=[pl.BlockSpec(memory_space=pl.ANY)],
            out_specs=pl.BlockSpec((per * nw, src.shape[1]), lambda g, si: (g, 0)),
            scratch_shapes=[pltpu.SemaphoreType.DMA(())]),
        out_shape=jax.ShapeDtypeStruct((n * nw, src.shape[1]), src.dtype),
        compiler_params=_params(("arbitrary",)),
    )(si, src)


def _scatter_kernel(di_ref, x_ref, dst_ref, sem, *, nw, per):
    base = pl.program_id(0) * per

    def row_copy(i):
        t = pl.multiple_of(di_ref[base + i] * nw, nw)
        return pltpu.make_async_copy(x_ref.at[pl.ds(pl.multiple_of(i * nw, nw), nw)], dst_ref.at[pl.ds(t, nw)], sem)

    def issue(i, carry):
        row_copy(i).start()
        return carry

    def drain(i, carry):
        row_copy(i).wait()
        return carry

    lax.fori_loop(0, per, issue, 0, unroll=8)
    lax.fori_loop(0, per, drain, 0, unroll=8)


def scatter_rows(x, di, nw):
    n = di.shape[0]
    per = ROWS_PER_STEP
    assert n % per == 0
    return pl.pallas_call(
        functools.partial(_scatter_kernel, nw=nw, per=per),
        name="scatter_rows",
        grid_spec=pltpu.PrefetchScalarGridSpec(
            num_scalar_prefetch=1, grid=(n // per,),
            in_specs=[pl.BlockSpec((per * nw, x.shape[1]), lambda g, di: (g, 0))],
            out_specs=pl.BlockSpec(memory_space=pl.ANY),
            scratch_shapes=[pltpu.SemaphoreType.DMA(())]),
        out_shape=jax.ShapeDtypeStruct((n * nw, x.shape[1]), x.dtype),
        compiler_params=_params(("arbitrary",)),
    )(di, x)


def _ffn_kernel(xs_ref, wg_ref, wu_ref, wd_ref, gate_ref, ys_ref, xb_scr, hid_scr, wd_scr, *, ts, nw, tf):
    f = pl.program_id(1)
    s = xb_scr.shape[0]

    @pl.when(f == 0)
    def _():
        _load_packed_rows(xs_ref.at[0], 0, s, nw, xb_scr, 0)

    wg = wg_ref[0, 0].astype(BF16)
    wu = wu_ref[0, 0].astype(BF16)
    for r in range(s // ts):
        rows = slice(r * ts, (r + 1) * ts)
        xb = xb_scr[rows, :]
        hid_scr[f, rows, :] = (_silu(_dot(xb, wg)) * _dot(xb, wu)).astype(BF16)
    wd_scr[pl.ds(pl.multiple_of(f * tf, tf), tf), :] = wd_ref[0, 0].astype(BF16)

    @pl.when(f == pl.num_programs(1) - 1)
    def _():
        for r in range(s // ts):
            rows = slice(r * ts, (r + 1) * ts)
            hid = jnp.concatenate([hid_scr[c, rows, :] for c in range(hid_scr.shape[0])], axis=1)
            y = _dot(hid, wd_scr[...])
            _store_packed_rows(ys_ref.at[0], r * ts, ts, y * gate_ref[0, rows, :])


def expert_ffn(xs, w_gate, w_up, w_down, layer, gates, d):
    nw = d // (2 * LANES)
    e = xs.shape[0]
    s = xs.shape[1] // nw
    ff = w_gate.shape[3]
    tf = 256 if ff % 256 == 0 else ff
    once = pl.Buffered(1)
    ts = max(t for t in range(8, 577, 8) if s % t == 0)
    return pl.pallas_call(
        functools.partial(_ffn_kernel, ts=ts, nw=nw, tf=tf),
        name="expert_ffn",
        grid=(e, ff // tf),
        in_specs=[pl.BlockSpec((1, s * nw, LANES), lambda i, f: (i, 0, 0)),
                  pl.BlockSpec((1, 1, d, tf), lambda i, f: (layer, i, 0, f)),
                  pl.BlockSpec((1, 1, d, tf), lambda i, f: (layer, i, 0, f)),
                  pl.BlockSpec((1, 1, tf, d), lambda i, f: (layer, i, f, 0)),
                  pl.BlockSpec((1, s, 1), lambda i, f: (i, 0, 0))],
        out_specs=pl.BlockSpec((1, s * nw, LANES), lambda i, f: (i, 0, 0), pipeline_mode=once),
        out_shape=jax.ShapeDtypeStruct((e, s * nw, LANES), I32),
        scratch_shapes=[pltpu.VMEM((s, d), BF16), pltpu.VMEM((ff // tf, s, tf), BF16),
                        pltpu.VMEM((ff, d), BF16)],
        compiler_params=_params(("arbitrary", "arbitrary")),
    )(xs, w_gate, w_up, w_down, gates)


COMB_ROWS = 128
COMB_ENT = 256


def _combine_kernel(kb_ref, ch_ref, flag_ref, ys_ref, xm_ref, start_ref, cnt_ref, mod_ref, *rest, d, final):
    if final:
        gf_ref, o_ref, chunk_scr = rest
    else:
        o_ref, chunk_scr = rest
    p = pl.program_id(0)
    flags = flag_ref[p]

    @pl.when((flags & 2) != 0)
    def _():
        o_ref[...] = jnp.zeros_like(o_ref)

    @pl.when((flags & 8) != 0)
    def _():
        _load_packed_rows(ys_ref, 0, COMB_ENT, d // (2 * LANES), chunk_scr, 0)

    @pl.when((flags & 1) != 0)
    def _():
        gpos = ch_ref[p] * COMB_ENT + lax.broadcasted_iota(I32, (COMB_ROWS, COMB_ENT), 1)
        st = start_ref[...]
        onehot = _ones((gpos >= st) & (gpos < st + cnt_ref[...]), BF16)
        o_ref[...] += _dot(onehot, chunk_scr[...])

    @pl.when((flags & 4) != 0)
    def _():
        m = mod_ref[0]
        out = xm_ref[...] + m[:, 5 * d:6 * d] * o_ref[...]
        if final:
            out = out * lax.rsqrt(jnp.mean(out * out, axis=-1, keepdims=True) + EPS) * gf_ref[...]
        o_ref[...] = out


def combine(ys_sorted, xm, start_col, cnt_col, mod, kb, ch, flags, rows, tiles_per_sample, n_x_tiles,
            n_samples, final_gain):
    d = xm.shape[1]
    nw = d // (2 * LANES)
    n_pairs = kb.shape[0]
    mod_idx = lambda p, kb, ch, fl: (jnp.where(kb[p] < n_x_tiles, kb[p] // tiles_per_sample, n_samples), 0, 0)
    in_specs = [pl.BlockSpec((COMB_ENT * nw, LANES), lambda p, kb, ch, fl: (ch[p], 0)),
                pl.BlockSpec((COMB_ROWS, d), lambda p, kb, ch, fl: (kb[p], 0)),
                pl.BlockSpec((COMB_ROWS, 1), lambda p, kb, ch, fl: (kb[p], 0)),
                pl.BlockSpec((COMB_ROWS, 1), lambda p, kb, ch, fl: (kb[p], 0)),
                pl.BlockSpec((1, 1, N_MOD * d), mod_idx)]
    args = [ys_sorted, xm, start_col, cnt_col, mod]
    if final_gain is not None:
        in_specs.append(pl.BlockSpec((1, d), lambda p, kb, ch, fl: (0, 0)))
        args.append(final_gain)
    return pl.pallas_call(
        functools.partial(_combine_kernel, d=d, final=final_gain is not None),
        name="combine",
        grid_spec=pltpu.PrefetchScalarGridSpec(
            num_scalar_prefetch=3, grid=(n_pairs,),
            in_specs=in_specs,
            out_specs=pl.BlockSpec((COMB_ROWS, d), lambda p, kb, ch, fl: (kb[p], 0)),
            scratch_shapes=[pltpu.VMEM((COMB_ENT, d), BF16)]),
        out_shape=jax.ShapeDtypeStruct((rows, d), F32),
        compiler_params=_params(("arbitrary",)),
    )(kb, ch, flags, *args)


def _combine_schedule(start_row, n_blocks, n_chunks):
    blk_start = start_row[::COMB_ROWS]
    blk_end = jnp.concatenate([blk_start[1:], jnp.full((1,), n_chunks * COMB_ENT, I32)])
    lo = jnp.minimum(blk_start // COMB_ENT, n_chunks - 1)
    hi = jnp.maximum(lo, (blk_end - 1) // COMB_ENT)
    cnt = hi - lo + 1
    off = jnp.cumsum(cnt) - cnt
    total = off[-1] + cnt[-1]
    n_pairs = n_blocks + n_chunks
    pidx = jnp.arange(n_pairs, dtype=I32)
    kb = jnp.sum((off[None, :] <= pidx[:, None]).astype(I32), axis=1) - 1
    valid = pidx < total
    ch = jnp.where(valid, lo[kb] + pidx - off[kb], hi[n_blocks - 1]).astype(I32)
    first = valid & (pidx == off[kb])
    last = valid & (pidx == off[kb] + cnt[kb] - 1)
    fresh = jnp.concatenate([jnp.ones((1,), bool), ch[1:] != ch[:-1]])
    flags = valid.astype(I32) + 2 * first.astype(I32) + 4 * last.astype(I32) + 8 * fresh.astype(I32)
    return kb, ch, flags


def _rope_tables(lx, n_ctx_rows, n_samples):
    n_freq = QK_DIM // 4
    freqs = ROPE_BASE ** (-jnp.arange(n_freq, dtype=F32) / n_freq)
    rows = lx // GRID_W
    row = jnp.repeat(jnp.arange(rows, dtype=F32), GRID_W)
    col = jnp.tile(jnp.arange(GRID_W, dtype=F32), rows)
    ang = jnp.concatenate([row[:, None] * freqs, col[:, None] * freqs], axis=-1)
    cos, sin = jnp.cos(ang), jnp.sin(ang)
    ct = jnp.tile(jnp.concatenate([cos, cos, cos, cos], axis=-1), (n_samples, 1))
    st = jnp.tile(jnp.concatenate([-sin, -sin, sin, sin], axis=-1), (n_samples, 1))
    ct = jnp.concatenate([ct, jnp.ones((n_ctx_rows, LANES), F32)], axis=0)
    st = jnp.concatenate([st, jnp.zeros((n_ctx_rows, LANES), F32)], axis=0)
    return ct, st


def kernel(x, c, ctx, c_ctx, w_ada, b_ada, norm_mix, norm_ffn, w_in, w_out, ret_log_decay, ret_norm,
           diff_lambda, diff_norm, w_router, w_gate, w_up, w_down, norm_final):
    n_b, lx, d = x.shape
    lc = ctx.shape[1]
    depth = w_ada.shape[0]
    heads = ret_log_decay.shape[-1]
    dheads = diff_norm.shape[-1] // V_DIM
    n_e = w_router.shape[-1]
    rx, rc = n_b * lx, n_b * lc
    cap_x = CAPACITY_FACTOR * lx // n_e
    cap_c = CAPACITY_FACTOR * lc // n_e
    tm = 512 if (lx % 512 == 0 and rx % 512 == 0 and rc % 512 == 0) else 256
    tmo = 256
    assert lx % lc == 0 and lc % tmo == 0 and heads % 2 == 0 and n_b < MOD_ROWS

    qk_scale = QK_DIM ** -0.5
    assert math.frexp(qk_scale)[0] == 0.5
    n_rq = heads * QK_DIM // LANES
    rope = tuple([True] * (2 * n_rq) + [False] * (2 * heads) + [True] * (2 * dheads) + [False] * dheads)
    is_q = tuple([True] * n_rq + [False] * (n_rq + 2 * heads) + [True] * dheads + [False] * (2 * dheads))
    w_in_bf = _prep_w_in(w_in, rope, is_q, qk_scale)

    xa = jnp.concatenate([x.reshape(rx, d), ctx.reshape(rc, d)], axis=0)
    cc = jnp.concatenate([c, c_ctx[None], jnp.zeros((MOD_ROWS - n_b - 1, d), F32)], axis=0)
    mods = ada_modulation(cc, w_ada, b_ada)
    ct, st = _rope_tables(lx, rc, n_b)
    log_g = jnp.log1p(-jnp.exp(ret_log_decay.astype(F32)))
    out = None

    for layer in range(depth):
        last = layer == depth - 1
        lam_init = 0.8 - 0.6 * math.exp(-0.3 * layer)
        mod = mods[layer].reshape(MOD_ROWS, 1, N_MOD * d)
        p = in_projection(xa, mod, norm_mix[layer][None], w_in_bf[layer], ct, st,
                          rx // tm, lx // tm, n_b, rope, tm)

        ret = retention(p, log_g[layer], ret_norm[layer][None], n_b, lx, lc, heads, not last)
        dif_x = diff_attention(p, diff_lambda[layer], diff_norm[layer][None], n_b, lx, lc, heads, dheads,
                               lam_init, True, 512 if lx % 512 == 0 else 256)
        if last:
            rows = rx
            ret_a, dif_a = ret[0], dif_x
        else:
            rows = rx + rc
            dif_c = diff_attention(p, diff_lambda[layer], diff_norm[layer][None], n_b, lx, lc, heads, dheads,
                                   lam_init, False, lc)
            ret_a = jnp.concatenate([ret[0], ret[1]], axis=0)
            dif_a = jnp.concatenate([dif_x, dif_c], axis=0)

        xm, h2, aff = out_projection(ret_a, dif_a, w_out[layer].astype(BF16), xa, mod, norm_ffn[layer][None],
                                     w_router[layer], rows, rx // tmo, lx // tmo, n_b, tmo)
        aff_t = aff.T

        sel = [expert_choice(aff_t, n_b, lx, 0, cap_x, 0, 0)]
        if not last:
            sel.append(expert_choice(aff_t, n_b, lc, rx // lc, cap_c, rx, n_b * n_e * cap_x))
        to_es = lambda t: jnp.transpose(t, (2, 0, 1)).reshape(n_e, -1)
        idx = jnp.concatenate([to_es(s[0]) for s in sel], axis=1)
        gates = jnp.concatenate([to_es(s[1]) for s in sel], axis=1)
        dest = jnp.concatenate([to_es(s[2]) for s in sel], axis=1)
        start_row = jnp.concatenate([s[3][0] for s in sel])
        cnt_row = jnp.concatenate([s[4][0] for s in sel])
        slots = idx.shape[1]
        n_ent = n_e * slots

        nw = d // (2 * LANES)
        xs = gather_rows(h2, idx.reshape(-1), nw)
        ys = expert_ffn(xs.reshape(n_e, slots * nw, LANES), w_gate, w_up, w_down, layer,
                        gates.reshape(n_e, slots, 1), d)
        ys_sorted = scatter_rows(ys.reshape(n_ent * nw, LANES), dest.reshape(-1), nw)

        kb, ch, flags = _combine_schedule(start_row, rows // COMB_ROWS, n_ent // COMB_ENT)
        xa_new = combine(ys_sorted, xm, start_row[:, None], cnt_row[:, None], mod, kb, ch, flags, rows,
                         lx // COMB_ROWS, rx // COMB_ROWS, n_b, norm_final[None] if last else None)
        if last:
            out = xa_new
        else:
            xa = xa_new
    return out.reshape(n_b, lx, d)
```

```python
import functools
import math

import jax
import jax.numpy as jnp
from jax import lax
from jax.experimental import pallas as pl
from jax.experimental.pallas import tpu as pltpu

F32 = jnp.float32
BF16 = jnp.bfloat16
I32 = jnp.int32

EPS = 1e-6
N_MOD = 6
GRID_W = 64
ROPE_BASE = 10000.0
QK_DIM = 64
V_DIM = 128
LANES = 128
CHUNK = 128
CAPACITY_FACTOR = 2
MOD_ROWS = 8
VMEM_LIMIT = 56 * 1024 * 1024


def _params(sem, vmem=VMEM_LIMIT):
    return pltpu.CompilerParams(dimension_semantics=sem, vmem_limit_bytes=vmem)


def _silu(x):
    return x / (1.0 + jnp.exp(-x))


def _dot(a, b):
    return jnp.dot(a, b, preferred_element_type=F32)


def _dot_nt(a, b):
    return lax.dot_general(a, b, (((1,), (1,)), ((), ())), preferred_element_type=F32)


def _ada_kernel(c_ref, w_ref, b_ref, o_ref):
    s = _silu(c_ref[...]).astype(BF16)
    o_ref[0] = _dot(s, w_ref[0].astype(BF16)) + b_ref[0]


def ada_modulation(cc, w_ada, b_ada):
    depth, d, n = w_ada.shape
    tn = 1024 if n % 1024 == 0 else n
    return pl.pallas_call(
        _ada_kernel,
        name="ada_mod",
        grid=(depth, n // tn),
        in_specs=[pl.BlockSpec((MOD_ROWS, d), lambda l, j: (0, 0)),
                  pl.BlockSpec((1, d, tn), lambda l, j: (l, 0, j)),
                  pl.BlockSpec((1, 1, tn), lambda l, j: (l, 0, j))],
        out_specs=pl.BlockSpec((1, MOD_ROWS, tn), lambda l, j: (l, 0, j)),
        out_shape=jax.ShapeDtypeStruct((depth, MOD_ROWS, n), F32),
        compiler_params=_params(("arbitrary", "arbitrary")),
    )(cc, w_ada, b_ada.reshape(depth, 1, n))


def _rmsnorm_mod(x, gain, shift, scale):
    y = x * lax.rsqrt(jnp.mean(x * x, axis=-1, keepdims=True) + EPS) * gain
    return y * (1.0 + scale) + shift


MXU_N = 256


def _group_of(lane):
    return (lane >> 6) & 1


def _swap_halves(a):
    lane = lax.broadcasted_iota(I32, a.shape, 1)
    return jnp.where((lane & 32) == 0, pltpu.roll(a, 96, 1), pltpu.roll(a, 32, 1))


def _inproj_kernel(x_ref, mod_ref, g_ref, w_ref, ct_ref, st_ref, o_ref, *, d, rope):
    m = mod_ref[0]
    h = _rmsnorm_mod(x_ref[...], g_ref[...], m[:, 0:d], m[:, d:2 * d]).astype(BF16)
    per = MXU_N // LANES
    for s in range(w_ref.shape[1] // MXU_N):
        acc = _dot(h, w_ref[:, s * MXU_N:(s + 1) * MXU_N])
        for t in range(per):
            a = acc[:, t * LANES:(t + 1) * LANES]
            if rope[s * per + t]:
                a = a * ct_ref[...] + _swap_halves(a) * st_ref[...]
            c0 = s * MXU_N + t * LANES
            o_ref[:, c0:c0 + LANES] = a.astype(BF16)


def in_projection(xa, mod, gain, w_bf, ct, st, n_x_tiles, tiles_per_sample, n_samples, rope, tm):
    r, d = xa.shape
    n = w_bf.shape[1]
    mod_idx = lambda i: (jnp.where(i < n_x_tiles, i // tiles_per_sample, n_samples), 0, 0)
    return pl.pallas_call(
        functools.partial(_inproj_kernel, d=d, rope=rope),
        name="in_proj",
        grid=(r // tm,),
        in_specs=[pl.BlockSpec((tm, d), lambda i: (i, 0)),
                  pl.BlockSpec((1, 1, N_MOD * d), mod_idx),
                  pl.BlockSpec((1, d), lambda i: (0, 0)),
                  pl.BlockSpec((d, n), lambda i: (0, 0), pipeline_mode=pl.Buffered(1)),
                  pl.BlockSpec((tm, LANES), lambda i: (i, 0)),
                  pl.BlockSpec((tm, LANES), lambda i: (i, 0))],
        out_specs=pl.BlockSpec((tm, n), lambda i: (i, 0)),
        out_shape=jax.ShapeDtypeStruct((r, n), BF16),
        compiler_params=_params(("arbitrary",)),
    )(xa, mod, gain, w_bf, ct, st)


def _prep_w_in(w_in, slab_scale):
    col_scale = jnp.repeat(jnp.asarray(slab_scale, F32), LANES)
    return (w_in * col_scale).astype(BF16)


def _ret_kernel(lg_ref, qx_ref, kx_ref, vx_ref, gx_ref, qc_ref, kc_ref, vc_ref, gc_ref, gain_ref,
                *rest, lx, lc, ctx_out):
    if ctx_out:
        ox_ref, oc_ref, accx, accc = rest
    else:
        ox_ref, accx = rest
    pair = pl.program_id(1)
    w2 = 2 * V_DIM
    lane = lax.broadcasted_iota(I32, (CHUNK, LANES), 1)
    dif = (lax.broadcasted_iota(I32, (CHUNK, CHUNK), 0) - lax.broadcasted_iota(I32, (CHUNK, CHUNK), 1)).astype(F32)
    pos = lax.broadcasted_iota(I32, (CHUNK, 1), 0).astype(F32)
    cpos = lax.broadcasted_iota(I32, (lc, 1), 0).astype(F32)
    c_len = float(CHUNK)
    lgf = [lg_ref[0, 2 * pair + hh] for hh in range(2)]
    lgb = [lg_ref[1, 2 * pair + hh] for hh in range(2)]

    def per_head(fn):
        vals = [fn(hh) for hh in range(2)]
        return jnp.concatenate([jnp.broadcast_to(v, (v.shape[0], V_DIM)) for v in vals], axis=1)

    decay = per_head(lambda hh: jnp.where(dif >= 0, jnp.exp(lgf[hh] * jnp.maximum(dif, 0.0)),
                                          jnp.exp(lgb[hh] * jnp.maximum(-dif, 0.0))))
    qd_f = per_head(lambda hh: jnp.exp(lgf[hh] * (pos + 1.0)))
    kd_f = per_head(lambda hh: jnp.exp(lgf[hh] * (c_len - 1.0 - pos)))
    qd_b = per_head(lambda hh: jnp.exp(lgb[hh] * (c_len - pos)))
    kd_b = per_head(lambda hh: jnp.exp(lgb[hh] * pos))
    one = jnp.ones((1, 1), F32)
    cd_f = per_head(lambda hh: jnp.exp(lgf[hh] * c_len) * one)
    cd_b = per_head(lambda hh: jnp.exp(lgb[hh] * c_len) * one)
    srow = lax.broadcasted_iota(I32, (LANES, w2), 0)
    scol = lax.broadcasted_iota(I32, (LANES, w2), 1)
    diag = (_group_of(srow) == 0) == (scol < V_DIM)
    zeros_v = jnp.zeros((CHUNK, V_DIM), BF16)

    def kv_state(k, v_scaled):
        return jnp.where(diag, _dot(k.astype(F32).T.astype(BF16), v_scaled.astype(BF16)), 0.0)

    kc = kc_ref[...]
    vc = vc_ref[...].astype(F32)
    s_f = kv_state(kc, vc * per_head(lambda hh: jnp.exp(lgf[hh] * (lc - 1.0 - cpos))))
    s_b = kv_state(kc, vc * per_head(lambda hh: jnp.exp(lgb[hh] * cpos)))

    def sweeps(q_ref, k_ref, v_ref, g_ref, o_ref, acc_ref, n, s_f0, s_b0):
        def rows_of(c):
            return pl.ds(pl.multiple_of(c * CHUNK, CHUNK), CHUNK)

        def forward(c, s):
            rows = rows_of(c)
            q, k, v = q_ref[rows, :], k_ref[rows, :], v_ref[rows, :]
            k_bd = jnp.concatenate([jnp.where(_group_of(lane) == 0, k, 0), jnp.where(_group_of(lane) == 1, k, 0)],
                                   axis=0)
            v_bd = jnp.concatenate([jnp.concatenate([v[:, :V_DIM], zeros_v], axis=1),
                                    jnp.concatenate([zeros_v, v[:, V_DIM:]], axis=1)], axis=0)
            sc = _dot_nt(q, k_bd) * decay
            acc_ref[rows, :] = _dot(sc.astype(BF16), v_bd) + _dot(q, s.astype(BF16)) * qd_f
            return s * cd_f + kv_state(k, v.astype(F32) * kd_f)

        def backward(t, s):
            rows = rows_of(n - 1 - t)
            q, k, v = q_ref[rows, :], k_ref[rows, :], v_ref[rows, :]
            y = acc_ref[rows, :] + _dot(q, s.astype(BF16)) * qd_b
            for hh in range(2):
                vsl = slice(hh * V_DIM, (hh + 1) * V_DIM)
                yh = y[:, vsl]
                yc = yh - jnp.mean(yh, axis=-1, keepdims=True)
                out = yc * lax.rsqrt(jnp.mean(yc * yc, axis=-1, keepdims=True) + EPS) * gain_ref[:, vsl]
                o_ref[rows, vsl] = (_silu(g_ref[rows, vsl].astype(F32)) * out).astype(BF16)
            return s * cd_b + kv_state(k, v.astype(F32) * kd_b)

        unroll = 4 if n % 4 == 0 else n
        lax.fori_loop(0, n, forward, s_f0, unroll=unroll)
        lax.fori_loop(0, n, backward, s_b0, unroll=unroll)

    sweeps(qx_ref, kx_ref, vx_ref, gx_ref, ox_ref, accx, lx // CHUNK, s_f, s_b)
    if ctx_out:
        zero = jnp.zeros((LANES, w2), F32)
        sweeps(qc_ref, kc_ref, vc_ref, gc_ref, oc_ref, accc, lc // CHUNK, zero, zero)


def retention(p, lg, gain, n_samples, lx, lc, heads, ctx_out):
    pairs = heads // 2
    half = heads // 2
    cb = n_samples * lx // lc
    w2 = 2 * V_DIM
    in_specs = [
        pl.BlockSpec(memory_space=pltpu.SMEM),
        pl.BlockSpec((lx, LANES), lambda b, h: (b, h)),
        pl.BlockSpec((lx, LANES), lambda b, h: (b, half + h)),
        pl.BlockSpec((lx, w2), lambda b, h: (b, half + h)),
        pl.BlockSpec((lx, w2), lambda b, h: (b, heads + h)),
        pl.BlockSpec((lc, LANES), lambda b, h: (cb + b, h)),
        pl.BlockSpec((lc, LANES), lambda b, h: (cb + b, half + h)),
        pl.BlockSpec((lc, w2), lambda b, h: (cb + b, half + h)),
        pl.BlockSpec((lc, w2), lambda b, h: (cb + b, heads + h)),
        pl.BlockSpec((1, w2), lambda b, h: (0, h)),
    ]
    out_specs = [pl.BlockSpec((lx, w2), lambda b, h: (b, h))]
    out_shape = [jax.ShapeDtypeStruct((n_samples * lx, heads * V_DIM), BF16)]
    scratch = [pltpu.VMEM((lx, w2), F32)]
    if ctx_out:
        out_specs.append(pl.BlockSpec((lc, w2), lambda b, h: (b, h)))
        out_shape.append(jax.ShapeDtypeStruct((n_samples * lc, heads * V_DIM), BF16))
        scratch.append(pltpu.VMEM((lc, w2), F32))
    return pl.pallas_call(
        functools.partial(_ret_kernel, lx=lx, lc=lc, ctx_out=ctx_out),
        name="retention",
        grid=(n_samples, pairs),
        in_specs=in_specs, out_specs=out_specs, out_shape=out_shape, scratch_shapes=scratch,
        compiler_params=_params(("arbitrary", "arbitrary")),
    )(lg, p, p, p, p, p, p, p, p, gain)


def _diff_kernel(lam_ref, q_ref, kc_ref, vc_ref, *rest, lam_init, has_x, kb):
    if has_x:
        kx_ref, vx_ref, gain_ref, o_ref = rest
    else:
        gain_ref, o_ref = rest
    lv = lam_ref[...]
    lam = (jnp.exp(jnp.sum(lv[0:1] * lv[1:2], axis=-1, keepdims=True))
           - jnp.exp(jnp.sum(lv[2:3] * lv[3:4], axis=-1, keepdims=True)) + lam_init)
    q = q_ref[...]
    tq = q.shape[0]
    lane = lax.broadcasted_iota(I32, q.shape, 1)
    blocks = [(kc_ref, vc_ref, 0, kc_ref.shape[0])]
    if has_x:
        blocks += [(kx_ref, vx_ref, r0, kb) for r0 in range(0, kx_ref.shape[0], kb)]
    v_ext = [jnp.concatenate([v_ref[r0:r0 + n, :], jnp.ones((n, V_DIM), BF16)], axis=1) for _, v_ref, r0, n in blocks]
    heads_out = []
    for t in range(2):
        qm = jnp.where(_group_of(lane) == t, q, 0)
        m = jnp.full((tq, 1), -jnp.inf, F32)
        acc = jnp.zeros((tq, 2 * V_DIM), F32)
        for (k_ref, _, r0, n), ve in zip(blocks, v_ext):
            s = _dot_nt(qm, k_ref[r0:r0 + n, :])
            m_new = jnp.maximum(m, jnp.max(s, axis=-1, keepdims=True))
            acc = acc * jnp.exp2(m - m_new) + _dot(jnp.exp2(s - m_new).astype(BF16), ve)
            m = m_new
        heads_out.append(acc[:, :V_DIM] * (1.0 / acc[:, V_DIM:V_DIM + 1]))
    out = heads_out[0] - lam * heads_out[1]
    y = out * lax.rsqrt(jnp.mean(out * out, axis=-1, keepdims=True) + EPS) * gain_ref[...]
    o_ref[...] = (y * (1.0 - lam_init)).astype(BF16)


def diff_attention(p, lam_vec, gain, n_samples, lx, lc, ret_heads, heads, lam_init, latent, tq):
    qo = 3 * ret_heads
    ko = qo + heads
    vo = ko + heads
    cb = n_samples * lx // lc
    lq = lx if latent else lc
    nq = lq // tq
    q_row = (lambda b, t: b * nq + t) if latent else (lambda b, t: n_samples * lx // tq + b * nq + t)
    in_specs = [
        pl.BlockSpec((4, QK_DIM), lambda b, h, t: (0, 0)),
        pl.BlockSpec((tq, LANES), lambda b, h, t: (q_row(b, t), qo + h)),
        pl.BlockSpec((lc, LANES), lambda b, h, t: (cb + b, ko + h)),
        pl.BlockSpec((lc, LANES), lambda b, h, t: (cb + b, vo + h)),
    ]
    args = [lam_vec, p, p, p]
    if latent:
        in_specs += [pl.BlockSpec((lx, LANES), lambda b, h, t: (b, ko + h)),
                     pl.BlockSpec((lx, LANES), lambda b, h, t: (b, vo + h))]
        args += [p, p]
    in_specs.append(pl.BlockSpec((1, LANES), lambda b, h, t: (0, h)))
    args.append(gain)
    return pl.pallas_call(
        functools.partial(_diff_kernel, lam_init=lam_init, has_x=latent, kb=min(1024, lx)),
        name="diff_attn_x" if latent else "diff_attn_ctx",
        grid=(n_samples, heads, nq),
        in_specs=in_specs,
        out_specs=pl.BlockSpec((tq, LANES), lambda b, h, t: (b * nq + t, h)),
        out_shape=jax.ShapeDtypeStruct((n_samples * lq, heads * V_DIM), BF16),
        compiler_params=_params(("arbitrary", "arbitrary", "arbitrary")),
    )(*args)


def _pack_pair(a, b):
    ua = pltpu.bitcast(a.astype(BF16).astype(F32), I32)
    ub = pltpu.bitcast(b.astype(BF16).astype(F32), I32)
    return ua | lax.shift_right_logical(ub, 16)


def _unpack_pair(w):
    return (pltpu.bitcast(w & -65536, F32).astype(BF16), pltpu.bitcast(w << 16, F32).astype(BF16))


def _store_packed_rows(ref, row0, n_rows, val):
    nw = val.shape[1] // (2 * LANES)
    for j in range(nw):
        w = _pack_pair(val[:, j * LANES:(j + 1) * LANES], val[:, (nw + j) * LANES:(nw + j + 1) * LANES])
        ref[pl.ds(row0 * nw + j, n_rows, stride=nw), :] = w


def _load_packed_rows(ref, row0, n_rows, nw, dst, dst_row0):
    for j in range(nw):
        hi, lo = _unpack_pair(ref[pl.ds(row0 * nw + j, n_rows, stride=nw), :])
        dst[dst_row0:dst_row0 + n_rows, j * LANES:(j + 1) * LANES] = hi
        dst[dst_row0:dst_row0 + n_rows, (nw + j) * LANES:(nw + j + 1) * LANES] = lo


def _outproj_kernel(ret_ref, dif_ref, w_ref, x_ref, mod_ref, g_ref, wr_ref, xm_ref, h_ref, aff_ref, *, d, wr):
    y = _dot(ret_ref[...], w_ref[0:wr, :]) + _dot(dif_ref[...], w_ref[wr:, :])
    m = mod_ref[0]
    xm = x_ref[...] + m[:, 2 * d:3 * d] * y
    xm_ref[...] = xm
    h = _rmsnorm_mod(xm, g_ref[...], m[:, 3 * d:4 * d], m[:, 4 * d:5 * d])
    _store_packed_rows(h_ref, 0, h.shape[0], h)
    n_e = wr_ref.shape[1] // 2
    h_hi = h.astype(BF16)
    h_lo = (h - h_hi.astype(F32)).astype(BF16)
    both = _dot(h_hi, wr_ref[...])
    logits = both[:, :n_e] + both[:, n_e:] + _dot(h_lo, wr_ref[...])[:, :n_e]
    e = jnp.exp(logits - jnp.max(logits, axis=-1, keepdims=True))
    aff_ref[...] = e / jnp.sum(e, axis=-1, keepdims=True)


def out_projection(ret, dif, w_bf, xa, mod, gain, w_router, rows, n_x_tiles, tiles_per_sample, n_samples, tm):
    d = xa.shape[1]
    wr = ret.shape[1]
    e = w_router.shape[1]
    nw = d // (2 * LANES)
    wr_hi = w_router.astype(BF16)
    wr_split = jnp.concatenate([wr_hi, (w_router - wr_hi.astype(F32)).astype(BF16)], axis=1)
    mod_idx = lambda i: (jnp.where(i < n_x_tiles, i // tiles_per_sample, n_samples), 0, 0)
    return pl.pallas_call(
        functools.partial(_outproj_kernel, d=d, wr=wr),
        name="out_proj_router",
        grid=(rows // tm,),
        in_specs=[pl.BlockSpec((tm, wr), lambda i: (i, 0)),
                  pl.BlockSpec((tm, dif.shape[1]), lambda i: (i, 0)),
                  pl.BlockSpec(w_bf.shape, lambda i: (0, 0)),
                  pl.BlockSpec((tm, d), lambda i: (i, 0)),
                  pl.BlockSpec((1, 1, N_MOD * d), mod_idx),
                  pl.BlockSpec((1, d), lambda i: (0, 0)),
                  pl.BlockSpec((d, 2 * e), lambda i: (0, 0))],
        out_specs=[pl.BlockSpec((tm, d), lambda i: (i, 0)),
                   pl.BlockSpec((tm * nw, LANES), lambda i: (i, 0)),
                   pl.BlockSpec((tm, e), lambda i: (i, 0))],
        out_shape=[jax.ShapeDtypeStruct((rows, d), F32),
                   jax.ShapeDtypeStruct((rows * nw, LANES), I32),
                   jax.ShapeDtypeStruct((rows, e), F32)],
        compiler_params=_params(("arbitrary",)),
    )(ret, dif, w_bf, xa, mod, gain, wr_split)


def _ones(mask, dtype):
    return jnp.where(mask, 1.0, 0.0).astype(dtype)


def _prefix_lanes(m):
    upper = _ones(lax.broadcasted_iota(I32, (LANES, LANES), 0) < lax.broadcasted_iota(I32, (LANES, LANES), 1), BF16)
    run = jnp.zeros((m.shape[0], 1), F32)
    outs = []
    for s in range(m.shape[1] // LANES):
        seg = m[:, s * LANES:(s + 1) * LANES]
        outs.append(_dot(seg, upper) + run)
        run = run + jnp.sum(seg.astype(F32), axis=1, keepdims=True)
    return jnp.concatenate(outs, axis=1)


def _topk_kernel(a_ref, idx_ref, gate_ref, dest_ref, start_ref, cnt_ref, posm_scr, dest_scr,
                 *, lg, cap, row_base, ent_base):
    g = pl.program_id(0)
    a = a_ref[...]
    n_e = a.shape[0]
    bits = pltpu.bitcast(a, I32)

    def bisect(_, lohi):
        lo, hi = lohi
        mid = lo + ((hi - lo) >> 1)
        ge = jnp.sum(_ones(bits >= mid, F32), axis=1, keepdims=True) >= cap
        return jnp.where(ge, mid, lo), jnp.where(ge, hi, mid)

    thr, _ = lax.fori_loop(0, 31, bisect,
                           (jnp.zeros((n_e, 1), I32), jnp.full((n_e, 1), 0x7F800000, I32)))
    gt = bits > thr
    eq = bits == thr
    need = cap - jnp.sum(_ones(gt, F32), axis=1, keepdims=True)
    sel = gt | (eq & (_prefix_lanes(_ones(eq, BF16)) < need))
    selb = _ones(sel, BF16)
    pos = _prefix_lanes(selb)
    start_t = jnp.sum(pos, axis=0, keepdims=True)
    lower = _ones(lax.broadcasted_iota(I32, (n_e, n_e), 0) > lax.broadcasted_iota(I32, (n_e, n_e), 1), BF16)
    base = (ent_base + g * (n_e * cap)).astype(F32)
    start_ref[...] = (start_t + base).astype(I32)
    cnt_ref[...] = jnp.sum(_ones(sel, F32), axis=0, keepdims=True).astype(I32)
    posm_scr[...] = jnp.where(sel, pos, -1.0)
    dest_scr[...] = start_t + base + _dot(lower, selb)

    tvals = (row_base + g * lg + lax.broadcasted_iota(I32, (1, lg), 1)).astype(F32)
    slot = lax.broadcasted_iota(I32, (cap, 1), 0).astype(F32)
    col = lax.broadcasted_iota(I32, (cap, n_e), 1)

    def compact(e, carry):
        ia, ga, da = carry
        m = posm_scr[pl.ds(e, 1), :] == slot
        i_e = jnp.sum(jnp.where(m, tvals, 0.0), axis=1, keepdims=True)
        g_e = jnp.sum(jnp.where(m, a_ref[pl.ds(e, 1), :], 0.0), axis=1, keepdims=True)
        d_e = jnp.sum(jnp.where(m, dest_scr[pl.ds(e, 1), :], 0.0), axis=1, keepdims=True)
        here = col == e
        return jnp.where(here, i_e, ia), jnp.where(here, g_e, ga), jnp.where(here, d_e, da)

    z = jnp.zeros((cap, n_e), F32)
    ia, ga, da = lax.fori_loop(0, n_e, compact, (z, z, z))
    idx_ref[0] = ia.astype(I32)
    gate_ref[0] = ga
    dest_ref[0] = da.astype(I32)


def expert_choice(aff_t, groups, lg, first_block, cap, row_base, ent_base):
    e = aff_t.shape[0]
    return pl.pallas_call(
        functools.partial(_topk_kernel, lg=lg, cap=cap, row_base=row_base, ent_base=ent_base),
        name="expert_choice",
        grid=(groups,),
        in_specs=[pl.BlockSpec((e, lg), lambda g: (0, first_block + g))],
        out_specs=[pl.BlockSpec((1, cap, e), lambda g: (g, 0, 0)),
                   pl.BlockSpec((1, cap, e), lambda g: (g, 0, 0)),
                   pl.BlockSpec((1, cap, e), lambda g: (g, 0, 0)),
                   pl.BlockSpec((1, lg), lambda g: (0, g)),
                   pl.BlockSpec((1, lg), lambda g: (0, g))],
        out_shape=[jax.ShapeDtypeStruct((groups, cap, e), I32),
                   jax.ShapeDtypeStruct((groups, cap, e), F32),
                   jax.ShapeDtypeStruct((groups, cap, e), I32),
                   jax.ShapeDtypeStruct((1, groups * lg), I32),
                   jax.ShapeDtypeStruct((1, groups * lg), I32)],
        scratch_shapes=[pltpu.VMEM((e, lg), F32), pltpu.VMEM((e, lg), F32)],
        compiler_params=_params(("arbitrary",)),
    )(aff_t)


ROWS_PER_STEP = 512


def _gather_kernel(si_ref, src_ref, o_ref, sem, *, nw, per):
    base = pl.program_id(0) * per

    def row_copy(i):
        s = pl.multiple_of(si_ref[base + i] * nw, nw)
        return pltpu.make_async_copy(src_ref.at[pl.ds(s, nw)], o_ref.at[pl.ds(pl.multiple_of(i * nw, nw), nw)], sem)

    def issue(i, carry):
        row_copy(i).start()
        return carry

    def drain(i, carry):
        row_copy(i).wait()
        return carry

    lax.fori_loop(0, per, issue, 0, unroll=8)
    lax.fori_loop(0, per, drain, 0, unroll=8)


def gather_rows(src, si, nw):
    n = si.shape[0]
    per = ROWS_PER_STEP
    assert n % per == 0
    return pl.pallas_call(
        functools.partial(_gather_kernel, nw=nw, per=per),
        name="gather_rows",
        grid_spec=pltpu.PrefetchScalarGridSpec(
            num_scalar_prefetch=1, grid=(n // per,),
            in_specs=[pl.BlockSpec(memory_space=pl.ANY)],
            out_specs=pl.BlockSpec((per * nw, src.shape[1]), lambda g, si: (g, 0)),
            scratch_shapes=[pltpu.SemaphoreType.DMA(())]),
        out_shape=jax.ShapeDtypeStruct((n * nw, src.shape[1]), src.dtype),
        compiler_params=_params(("arbitrary",)),
    )(si, src)


def _scatter_kernel(di_ref, x_ref, dst_ref, sem, *, nw, per):
    base = pl.program_id(0) * per

    def row_copy(i):
        t = pl.multiple_of(di_ref[base + i] * nw, nw)
        return pltpu.make_async_copy(x_ref.at[pl.ds(pl.multiple_of(i * nw, nw), nw)], dst_ref.at[pl.ds(t, nw)], sem)

    def issue(i, carry):
        row_copy(i).start()
        return carry

    def drain(i, carry):
        row_copy(i).wait()
        return carry

    lax.fori_loop(0, per, issue, 0, unroll=8)
    lax.fori_loop(0, per, drain, 0, unroll=8)


def scatter_rows(x, di, nw):
    n = di.shape[0]
    per = ROWS_PER_STEP
    assert n % per == 0
    return pl.pallas_call(
        functools.partial(_scatter_kernel, nw=nw, per=per),
        name="scatter_rows",
        grid_spec=pltpu.PrefetchScalarGridSpec(
            num_scalar_prefetch=1, grid=(n // per,),
            in_specs=[pl.BlockSpec((per * nw, x.shape[1]), lambda g, di: (g, 0))],
            out_specs=pl.BlockSpec(memory_space=pl.ANY),
            scratch_shapes=[pltpu.SemaphoreType.DMA(())]),
        out_shape=jax.ShapeDtypeStruct((n * nw, x.shape[1]), x.dtype),
        compiler_params=_params(("arbitrary",)),
    )(di, x)


def _ffn_kernel(xs_ref, wg_ref, wu_ref, wd_ref, gate_ref, ys_ref, xb_scr, hid_scr, wd_scr, *, ts, nw, tf):
    f = pl.program_id(1)
    s = xb_scr.shape[0]

    @pl.when(f == 0)
    def _():
        _load_packed_rows(xs_ref.at[0], 0, s, nw, xb_scr, 0)

    wg = wg_ref[0, 0].astype(BF16)
    wu = wu_ref[0, 0].astype(BF16)
    for r in range(s // ts):
        rows = slice(r * ts, (r + 1) * ts)
        xb = xb_scr[rows, :]
        hid_scr[f, rows, :] = (_silu(_dot(xb, wg)) * _dot(xb, wu)).astype(BF16)
    wd_scr[pl.ds(pl.multiple_of(f * tf, tf), tf), :] = wd_ref[0, 0].astype(BF16)

    @pl.when(f == pl.num_programs(1) - 1)
    def _():
        for r in range(s // ts):
            rows = slice(r * ts, (r + 1) * ts)
            hid = jnp.concatenate([hid_scr[c, rows, :] for c in range(hid_scr.shape[0])], axis=1)
            y = _dot(hid, wd_scr[...])
            _store_packed_rows(ys_ref.at[0], r * ts, ts, y * gate_ref[0, rows, :])


def expert_ffn(xs, w_gate, w_up, w_down, layer, gates, d):
    nw = d // (2 * LANES)
    e = xs.shape[0]
    s = xs.shape[1] // nw
    ff = w_gate.shape[3]
    tf = 256 if ff % 256 == 0 else ff
    once = pl.Buffered(1)
    ts = max(t for t in range(8, 577, 8) if s % t == 0)
    return pl.pallas_call(
        functools.partial(_ffn_kernel, ts=ts, nw=nw, tf=tf),
        name="expert_ffn",
        grid=(e, ff // tf),
        in_specs=[pl.BlockSpec((1, s * nw, LANES), lambda i, f: (i, 0, 0)),
                  pl.BlockSpec((1, 1, d, tf), lambda i, f: (layer, i, 0, f)),
                  pl.BlockSpec((1, 1, d, tf), lambda i, f: (layer, i, 0, f)),
                  pl.BlockSpec((1, 1, tf, d), lambda i, f: (layer, i, f, 0)),
                  pl.BlockSpec((1, s, 1), lambda i, f: (i, 0, 0))],
        out_specs=pl.BlockSpec((1, s * nw, LANES), lambda i, f: (i, 0, 0), pipeline_mode=once),
        out_shape=jax.ShapeDtypeStruct((e, s * nw, LANES), I32),
        scratch_shapes=[pltpu.VMEM((s, d), BF16), pltpu.VMEM((ff // tf, s, tf), BF16),
                        pltpu.VMEM((ff, d), BF16)],
        compiler_params=_params(("arbitrary", "arbitrary")),
    )(xs, w_gate, w_up, w_down, gates)


COMB_ROWS = 256
COMB_ENT = 256


def _combine_kernel(kb_ref, ch_ref, flag_ref, ys_ref, xm_ref, start_ref, cnt_ref, mod_ref, *rest, d, final):
    if final:
        gf_ref, o_ref, chunk_scr = rest
    else:
        o_ref, chunk_scr = rest
    p = pl.program_id(0)
    flags = flag_ref[p]

    @pl.when((flags & 2) != 0)
    def _():
        o_ref[...] = jnp.zeros_like(o_ref)

    @pl.when((flags & 8) != 0)
    def _():
        _load_packed_rows(ys_ref, 0, COMB_ENT, d // (2 * LANES), chunk_scr, 0)

    @pl.when((flags & 1) != 0)
    def _():
        gpos = ch_ref[p] * COMB_ENT + lax.broadcasted_iota(I32, (COMB_ROWS, COMB_ENT), 1)
        st = start_ref[...]
        onehot = _ones((gpos >= st) & (gpos < st + cnt_ref[...]), BF16)
        o_ref[...] += _dot(onehot, chunk_scr[...])

    @pl.when((flags & 4) != 0)
    def _():
        m = mod_ref[0]
        out = xm_ref[...] + m[:, 5 * d:6 * d] * o_ref[...]
        if final:
            out = out * lax.rsqrt(jnp.mean(out * out, axis=-1, keepdims=True) + EPS) * gf_ref[...]
        o_ref[...] = out


def combine(ys_sorted, xm, start_col, cnt_col, mod, kb, ch, flags, rows, tiles_per_sample, n_x_tiles,
            n_samples, final_gain):
    d = xm.shape[1]
    nw = d // (2 * LANES)
    n_pairs = kb.shape[0]
    mod_idx = lambda p, kb, ch, fl: (jnp.where(kb[p] < n_x_tiles, kb[p] // tiles_per_sample, n_samples), 0, 0)
    in_specs = [pl.BlockSpec((COMB_ENT * nw, LANES), lambda p, kb, ch, fl: (ch[p], 0)),
                pl.BlockSpec((COMB_ROWS, d), lambda p, kb, ch, fl: (kb[p], 0)),
                pl.BlockSpec((COMB_ROWS, 1), lambda p, kb, ch, fl: (kb[p], 0)),
                pl.BlockSpec((COMB_ROWS, 1), lambda p, kb, ch, fl: (kb[p], 0)),
                pl.BlockSpec((1, 1, N_MOD * d), mod_idx)]
    args = [ys_sorted, xm, start_col, cnt_col, mod]
    if final_gain is not None:
        in_specs.append(pl.BlockSpec((1, d), lambda p, kb, ch, fl: (0, 0)))
        args.append(final_gain)
    return pl.pallas_call(
        functools.partial(_combine_kernel, d=d, final=final_gain is not None),
        name="combine",
        grid_spec=pltpu.PrefetchScalarGridSpec(
            num_scalar_prefetch=3, grid=(n_pairs,),
            in_specs=in_specs,
            out_specs=pl.BlockSpec((COMB_ROWS, d), lambda p, kb, ch, fl: (kb[p], 0)),
            scratch_shapes=[pltpu.VMEM((COMB_ENT, d), BF16)]),
        out_shape=jax.ShapeDtypeStruct((rows, d), F32),
        compiler_params=_params(("arbitrary",)),
    )(kb, ch, flags, *args)


def _combine_schedule(start_row, n_blocks, n_chunks):
    blk_start = start_row[::COMB_ROWS]
    blk_end = jnp.concatenate([blk_start[1:], jnp.full((1,), n_chunks * COMB_ENT, I32)])
    lo = jnp.minimum(blk_start // COMB_ENT, n_chunks - 1)
    hi = jnp.maximum(lo, (blk_end - 1) // COMB_ENT)
    cnt = hi - lo + 1
    off = jnp.cumsum(cnt) - cnt
    total = off[-1] + cnt[-1]
    n_pairs = n_blocks + n_chunks
    pidx = jnp.arange(n_pairs, dtype=I32)
    kb = jnp.sum((off[None, :] <= pidx[:, None]).astype(I32), axis=1) - 1
    valid = pidx < total
    ch = jnp.where(valid, lo[kb] + pidx - off[kb], hi[n_blocks - 1]).astype(I32)
    first = valid & (pidx == off[kb])
    last = valid & (pidx == off[kb] + cnt[kb] - 1)
    fresh = jnp.concatenate([jnp.ones((1,), bool), ch[1:] != ch[:-1]])
    flags = valid.astype(I32) + 2 * first.astype(I32) + 4 * last.astype(I32) + 8 * fresh.astype(I32)
    return kb, ch, flags


def _rope_tables(lx, n_ctx_rows, n_samples):
    n_freq = QK_DIM // 4
    freqs = ROPE_BASE ** (-jnp.arange(n_freq, dtype=F32) / n_freq)
    rows = lx // GRID_W
    row = jnp.repeat(jnp.arange(rows, dtype=F32), GRID_W)
    col = jnp.tile(jnp.arange(GRID_W, dtype=F32), rows)
    ang = jnp.concatenate([row[:, None] * freqs, col[:, None] * freqs], axis=-1)
    cos, sin = jnp.cos(ang), jnp.sin(ang)
    ct = jnp.tile(jnp.concatenate([cos, cos], axis=-1), (n_samples, LANES // QK_DIM))
    st = jnp.tile(jnp.concatenate([-sin, sin], axis=-1), (n_samples, LANES // QK_DIM))
    ct = jnp.concatenate([ct, jnp.ones((n_ctx_rows, LANES), F32)], axis=0)
    st = jnp.concatenate([st, jnp.zeros((n_ctx_rows, LANES), F32)], axis=0)
    return ct, st


def kernel(x, c, ctx, c_ctx, w_ada, b_ada, norm_mix, norm_ffn, w_in, w_out, ret_log_decay, ret_norm,
           diff_lambda, diff_norm, w_router, w_gate, w_up, w_down, norm_final):
    n_b, lx, d = x.shape
    lc = ctx.shape[1]
    depth = w_ada.shape[0]
    heads = ret_log_decay.shape[-1]
    dheads = diff_norm.shape[-1] // V_DIM
    n_e = w_router.shape[-1]
    rx, rc = n_b * lx, n_b * lc
    cap_x = CAPACITY_FACTOR * lx // n_e
    cap_c = CAPACITY_FACTOR * lc // n_e
    tm = 512 if (lx % 512 == 0 and rx % 512 == 0 and rc % 512 == 0) else 256
    tmo = 256
    assert lx % lc == 0 and lc % tmo == 0 and heads % 2 == 0 and n_b < MOD_ROWS

    qk_scale = QK_DIM ** -0.5
    n_rq = heads * QK_DIM // LANES
    rope = tuple([True] * (2 * n_rq) + [False] * (2 * heads) + [True] * (2 * dheads) + [False] * dheads)
    slab_scale = tuple([qk_scale] * n_rq + [1.0] * (n_rq + 2 * heads) + [qk_scale * math.log2(math.e)] * dheads
                       + [1.0] * (2 * dheads))
    w_in_bf = _prep_w_in(w_in, slab_scale)

    xa = jnp.concatenate([x.reshape(rx, d), ctx.reshape(rc, d)], axis=0)
    cc = jnp.concatenate([c, c_ctx[None], jnp.zeros((MOD_ROWS - n_b - 1, d), F32)], axis=0)
    mods = ada_modulation(cc, w_ada, b_ada)
    ct, st = _rope_tables(lx, rc, n_b)
    log_g = jnp.log1p(-jnp.exp(ret_log_decay.astype(F32)))
    out = None

    for layer in range(depth):
        last = layer == depth - 1
        lam_init = 0.8 - 0.6 * math.exp(-0.3 * layer)
        mod = mods[layer].reshape(MOD_ROWS, 1, N_MOD * d)
        p = in_projection(xa, mod, norm_mix[layer][None], w_in_bf[layer], ct, st,
                          rx // tm, lx // tm, n_b, rope, tm)

        ret = retention(p, log_g[layer], ret_norm[layer][None], n_b, lx, lc, heads, not last)
        dif_x = diff_attention(p, diff_lambda[layer], diff_norm[layer][None], n_b, lx, lc, heads, dheads,
                               lam_init, True, 512 if lx % 512 == 0 else 256)
        if last:
            rows = rx
            ret_a, dif_a = ret[0], dif_x
        else:
            rows = rx + rc
            dif_c = diff_attention(p, diff_lambda[layer], diff_norm[layer][None], n_b, lx, lc, heads, dheads,
                                   lam_init, False, lc)
            ret_a = jnp.concatenate([ret[0], ret[1]], axis=0)
            dif_a = jnp.concatenate([dif_x, dif_c], axis=0)

        xm, h2, aff = out_projection(ret_a, dif_a, w_out[layer].astype(BF16), xa, mod, norm_ffn[layer][None],
                                     w_router[layer], rows, rx // tmo, lx // tmo, n_b, tmo)
        aff_t = aff.T

        sel = [expert_choice(aff_t, n_b, lx, 0, cap_x, 0, 0)]
        if not last:
            sel.append(expert_choice(aff_t, n_b, lc, rx // lc, cap_c, rx, n_b * n_e * cap_x))
        to_es = lambda t: jnp.transpose(t, (2, 0, 1)).reshape(n_e, -1)
        idx = jnp.concatenate([to_es(s[0]) for s in sel], axis=1)
        gates = jnp.concatenate([to_es(s[1]) for s in sel], axis=1)
        dest = jnp.concatenate([to_es(s[2]) for s in sel], axis=1)
        start_row = jnp.concatenate([s[3][0] for s in sel])
        cnt_row = jnp.concatenate([s[4][0] for s in sel])
        slots = idx.shape[1]
        n_ent = n_e * slots

        nw = d // (2 * LANES)
        xs = gather_rows(h2, idx.reshape(-1), nw)
        ys = expert_ffn(xs.reshape(n_e, slots * nw, LANES), w_gate, w_up, w_down, layer,
                        gates.reshape(n_e, slots, 1), d)
        ys_sorted = scatter_rows(ys.reshape(n_ent * nw, LANES), dest.reshape(-1), nw)

        kb, ch, flags = _combine_schedule(start_row, rows // COMB_ROWS, n_ent // COMB_ENT)
        xa_new = combine(ys_sorted, xm, start_row[:, None], cnt_row[:, None], mod, kb, ch, flags, rows,
                         lx // COMB_ROWS, rx // COMB_ROWS, n_b, norm_final[None] if last else None)
        if last:
            out = xa_new
        else:
            xa = xa_new
    return out.reshape(n_b, lx, d)
```

```python
import functools
import math

import jax
import jax.numpy as jnp
from jax import lax
from jax.experimental import pallas as pl
from jax.experimental.pallas import tpu as pltpu

F32 = jnp.float32
BF16 = jnp.bfloat16
I32 = jnp.int32

EPS = 1e-6
N_MOD = 6
GRID_W = 64
ROPE_BASE = 10000.0
QK_DIM = 64
V_DIM = 128
LANES = 128
CHUNK = 128
CAPACITY_FACTOR = 2
MOD_ROWS = 8
VMEM_LIMIT = 56 * 1024 * 1024


def _params(sem, vmem=VMEM_LIMIT):
    return pltpu.CompilerParams(dimension_semantics=sem, vmem_limit_bytes=vmem)


def _silu(x):
    return x / (1.0 + jnp.exp(-x))


def _dot(a, b):
    return jnp.dot(a, b, preferred_element_type=F32)


def _dot_nt(a, b):
    return lax.dot_general(a, b, (((1,), (1,)), ((), ())), preferred_element_type=F32)


def _ada_kernel(c_ref, w_ref, b_ref, o_ref):
    s = _silu(c_ref[...]).astype(BF16)
    o_ref[0] = _dot(s, w_ref[0].astype(BF16)) + b_ref[0]


def ada_modulation(cc, w_ada, b_ada):
    depth, d, n = w_ada.shape
    tn = 1024 if n % 1024 == 0 else n
    return pl.pallas_call(
        _ada_kernel,
        name="ada_mod",
        grid=(depth, n // tn),
        in_specs=[pl.BlockSpec((MOD_ROWS, d), lambda l, j: (0, 0)),
                  pl.BlockSpec((1, d, tn), lambda l, j: (l, 0, j)),
                  pl.BlockSpec((1, 1, tn), lambda l, j: (l, 0, j))],
        out_specs=pl.BlockSpec((1, MOD_ROWS, tn), lambda l, j: (l, 0, j)),
        out_shape=jax.ShapeDtypeStruct((depth, MOD_ROWS, n), F32),
        compiler_params=_params(("arbitrary", "arbitrary")),
    )(cc, w_ada, b_ada.reshape(depth, 1, n))


def _rmsnorm_mod(x, gain, shift, scale):
    y = x * lax.rsqrt(jnp.mean(x * x, axis=-1, keepdims=True) + EPS) * gain
    return y * (1.0 + scale) + shift


MXU_N = 256


def _group_of(lane):
    return (lane >> 6) & 1


def _swap_halves(a):
    lane = lax.broadcasted_iota(I32, a.shape, 1)
    return jnp.where((lane & 32) == 0, pltpu.roll(a, 96, 1), pltpu.roll(a, 32, 1))


def _inproj_kernel(x_ref, mod_ref, g_ref, w_ref, ct_ref, st_ref, o_ref, *, d, rope):
    m = mod_ref[0]
    h = _rmsnorm_mod(x_ref[...], g_ref[...], m[:, 0:d], m[:, d:2 * d]).astype(BF16)
    per = MXU_N // LANES
    for s in range(w_ref.shape[1] // MXU_N):
        acc = _dot(h, w_ref[:, s * MXU_N:(s + 1) * MXU_N])
        for t in range(per):
            a = acc[:, t * LANES:(t + 1) * LANES]
            if rope[s * per + t]:
                a = a * ct_ref[...] + _swap_halves(a) * st_ref[...]
            c0 = s * MXU_N + t * LANES
            o_ref[:, c0:c0 + LANES] = a.astype(BF16)


def in_projection(xa, mod, gain, w_bf, ct, st, n_x_tiles, tiles_per_sample, n_samples, rope, tm):
    r, d = xa.shape
    n = w_bf.shape[1]
    mod_idx = lambda i: (jnp.where(i < n_x_tiles, i // tiles_per_sample, n_samples), 0, 0)
    return pl.pallas_call(
        functools.partial(_inproj_kernel, d=d, rope=rope),
        name="in_proj",
        grid=(r // tm,),
        in_specs=[pl.BlockSpec((tm, d), lambda i: (i, 0)),
                  pl.BlockSpec((1, 1, N_MOD * d), mod_idx),
                  pl.BlockSpec((1, d), lambda i: (0, 0)),
                  pl.BlockSpec((d, n), lambda i: (0, 0), pipeline_mode=pl.Buffered(1)),
                  pl.BlockSpec((tm, LANES), lambda i: (i, 0)),
                  pl.BlockSpec((tm, LANES), lambda i: (i, 0))],
        out_specs=pl.BlockSpec((tm, n), lambda i: (i, 0)),
        out_shape=jax.ShapeDtypeStruct((r, n), BF16),
        compiler_params=_params(("arbitrary",)),
    )(xa, mod, gain, w_bf, ct, st)


def _prep_w_in(w_in, slab_scale):
    col_scale = jnp.repeat(jnp.asarray(slab_scale, F32), LANES)
    return (w_in * col_scale).astype(BF16)


def _ret_kernel(lg_ref, qx_ref, kx_ref, vx_ref, gx_ref, qc_ref, kc_ref, vc_ref, gc_ref, gain_ref,
                *rest, lx, lc, ctx_out):
    if ctx_out:
        ox_ref, oc_ref, accx, accc = rest
    else:
        ox_ref, accx = rest
    pair = pl.program_id(1)
    w2 = 2 * V_DIM
    lane = lax.broadcasted_iota(I32, (CHUNK, LANES), 1)
    dif = (lax.broadcasted_iota(I32, (CHUNK, CHUNK), 0) - lax.broadcasted_iota(I32, (CHUNK, CHUNK), 1)).astype(F32)
    pos = lax.broadcasted_iota(I32, (CHUNK, 1), 0).astype(F32)
    cpos = lax.broadcasted_iota(I32, (lc, 1), 0).astype(F32)
    c_len = float(CHUNK)
    lgf = [lg_ref[0, 2 * pair + hh] for hh in range(2)]
    lgb = [lg_ref[1, 2 * pair + hh] for hh in range(2)]

    def per_head(fn):
        vals = [fn(hh) for hh in range(2)]
        return jnp.concatenate([jnp.broadcast_to(v, (v.shape[0], V_DIM)) for v in vals], axis=1)

    decay = per_head(lambda hh: jnp.where(dif >= 0, jnp.exp(lgf[hh] * jnp.maximum(dif, 0.0)),
                                          jnp.exp(lgb[hh] * jnp.maximum(-dif, 0.0))))
    qd_f = per_head(lambda hh: jnp.exp(lgf[hh] * (pos + 1.0)))
    kd_f = per_head(lambda hh: jnp.exp(lgf[hh] * (c_len - 1.0 - pos)))
    qd_b = per_head(lambda hh: jnp.exp(lgb[hh] * (c_len - pos)))
    kd_b = per_head(lambda hh: jnp.exp(lgb[hh] * pos))
    one = jnp.ones((1, 1), F32)
    cd_f = per_head(lambda hh: jnp.exp(lgf[hh] * c_len) * one)
    cd_b = per_head(lambda hh: jnp.exp(lgb[hh] * c_len) * one)
    srow = lax.broadcasted_iota(I32, (LANES, w2), 0)
    scol = lax.broadcasted_iota(I32, (LANES, w2), 1)
    diag = (_group_of(srow) == 0) == (scol < V_DIM)
    zeros_v = jnp.zeros((CHUNK, V_DIM), BF16)

    def kv_state(k, v_scaled):
        return jnp.where(diag, _dot(k.astype(F32).T.astype(BF16), v_scaled.astype(BF16)), 0.0)

    kc = kc_ref[...]
    vc = vc_ref[...].astype(F32)
    s_f = kv_state(kc, vc * per_head(lambda hh: jnp.exp(lgf[hh] * (lc - 1.0 - cpos))))
    s_b = kv_state(kc, vc * per_head(lambda hh: jnp.exp(lgb[hh] * cpos)))

    def sweeps(q_ref, k_ref, v_ref, g_ref, o_ref, acc_ref, n, s_f0, s_b0):
        def rows_of(c):
            return pl.ds(pl.multiple_of(c * CHUNK, CHUNK), CHUNK)

        def forward(c, s):
            rows = rows_of(c)
            q, k, v = q_ref[rows, :], k_ref[rows, :], v_ref[rows, :]
            k_bd = jnp.concatenate([jnp.where(_group_of(lane) == 0, k, 0), jnp.where(_group_of(lane) == 1, k, 0)],
                                   axis=0)
            v_bd = jnp.concatenate([jnp.concatenate([v[:, :V_DIM], zeros_v], axis=1),
                                    jnp.concatenate([zeros_v, v[:, V_DIM:]], axis=1)], axis=0)
            sc = _dot_nt(q, k_bd) * decay
            acc_ref[rows, :] = _dot(sc.astype(BF16), v_bd) + _dot(q, s.astype(BF16)) * qd_f
            return s * cd_f + kv_state(k, v.astype(F32) * kd_f)

        def backward(t, s):
            rows = rows_of(n - 1 - t)
            q, k, v = q_ref[rows, :], k_ref[rows, :], v_ref[rows, :]
            y = acc_ref[rows, :] + _dot(q, s.astype(BF16)) * qd_b
            for hh in range(2):
                vsl = slice(hh * V_DIM, (hh + 1) * V_DIM)
                yh = y[:, vsl]
                yc = yh - jnp.mean(yh, axis=-1, keepdims=True)
                out = yc * lax.rsqrt(jnp.mean(yc * yc, axis=-1, keepdims=True) + EPS) * gain_ref[:, vsl]
                o_ref[rows, vsl] = (_silu(g_ref[rows, vsl].astype(F32)) * out).astype(BF16)
            return s * cd_b + kv_state(k, v.astype(F32) * kd_b)

        unroll = 4 if n % 4 == 0 else n
        lax.fori_loop(0, n, forward, s_f0, unroll=unroll)
        lax.fori_loop(0, n, backward, s_b0, unroll=unroll)

    sweeps(qx_ref, kx_ref, vx_ref, gx_ref, ox_ref, accx, lx // CHUNK, s_f, s_b)
    if ctx_out:
        zero = jnp.zeros((LANES, w2), F32)
        sweeps(qc_ref, kc_ref, vc_ref, gc_ref, oc_ref, accc, lc // CHUNK, zero, zero)


def retention(p, lg, gain, n_samples, lx, lc, heads, ctx_out):
    pairs = heads // 2
    half = heads // 2
    cb = n_samples * lx // lc
    w2 = 2 * V_DIM
    in_specs = [
        pl.BlockSpec(memory_space=pltpu.SMEM),
        pl.BlockSpec((lx, LANES), lambda b, h: (b, h)),
        pl.BlockSpec((lx, LANES), lambda b, h: (b, half + h)),
        pl.BlockSpec((lx, w2), lambda b, h: (b, half + h)),
        pl.BlockSpec((lx, w2), lambda b, h: (b, heads + h)),
        pl.BlockSpec((lc, LANES), lambda b, h: (cb + b, h)),
        pl.BlockSpec((lc, LANES), lambda b, h: (cb + b, half + h)),
        pl.BlockSpec((lc, w2), lambda b, h: (cb + b, half + h)),
        pl.BlockSpec((lc, w2), lambda b, h: (cb + b, heads + h)),
        pl.BlockSpec((1, w2), lambda b, h: (0, h)),
    ]
    out_specs = [pl.BlockSpec((lx, w2), lambda b, h: (b, h))]
    out_shape = [jax.ShapeDtypeStruct((n_samples * lx, heads * V_DIM), BF16)]
    scratch = [pltpu.VMEM((lx, w2), F32)]
    if ctx_out:
        out_specs.append(pl.BlockSpec((lc, w2), lambda b, h: (b, h)))
        out_shape.append(jax.ShapeDtypeStruct((n_samples * lc, heads * V_DIM), BF16))
        scratch.append(pltpu.VMEM((lc, w2), F32))
    return pl.pallas_call(
        functools.partial(_ret_kernel, lx=lx, lc=lc, ctx_out=ctx_out),
        name="retention",
        grid=(n_samples, pairs),
        in_specs=in_specs, out_specs=out_specs, out_shape=out_shape, scratch_shapes=scratch,
        compiler_params=_params(("arbitrary", "arbitrary")),
    )(lg, p, p, p, p, p, p, p, p, gain)


def _diff_kernel(lam_ref, q_ref, kc_ref, vc_ref, *rest, lam_init, has_x, kb):
    if has_x:
        kx_ref, vx_ref, gain_ref, o_ref = rest
    else:
        gain_ref, o_ref = rest
    lv = lam_ref[...]
    lam = (jnp.exp(jnp.sum(lv[0:1] * lv[1:2], axis=-1, keepdims=True))
           - jnp.exp(jnp.sum(lv[2:3] * lv[3:4], axis=-1, keepdims=True)) + lam_init)
    q = q_ref[...]
    tq = q.shape[0]
    lane = lax.broadcasted_iota(I32, q.shape, 1)
    blocks = [(kc_ref, vc_ref, 0, kc_ref.shape[0])]
    if has_x:
        blocks += [(kx_ref, vx_ref, r0, kb) for r0 in range(0, kx_ref.shape[0], kb)]
    v_ext = [jnp.concatenate([v_ref[r0:r0 + n, :], jnp.ones((n, V_DIM), BF16)], axis=1) for _, v_ref, r0, n in blocks]
    heads_out = []
    for t in range(2):
        qm = jnp.where(_group_of(lane) == t, q, 0)
        m = jnp.full((tq, 1), -jnp.inf, F32)
        acc = jnp.zeros((tq, 2 * V_DIM), F32)
        for (k_ref, _, r0, n), ve in zip(blocks, v_ext):
            s = _dot_nt(qm, k_ref[r0:r0 + n, :])
            m_new = jnp.maximum(m, jnp.max(s, axis=-1, keepdims=True))
            acc = acc * jnp.exp2(m - m_new) + _dot(jnp.exp2(s - m_new).astype(BF16), ve)
            m = m_new
        heads_out.append(acc[:, :V_DIM] * (1.0 / acc[:, V_DIM:V_DIM + 1]))
    out = heads_out[0] - lam * heads_out[1]
    y = out * lax.rsqrt(jnp.mean(out * out, axis=-1, keepdims=True) + EPS) * gain_ref[...]
    o_ref[...] = (y * (1.0 - lam_init)).astype(BF16)


def diff_attention(p, lam_vec, gain, n_samples, lx, lc, ret_heads, heads, lam_init, latent, tq):
    qo = 3 * ret_heads
    ko = qo + heads
    vo = ko + heads
    cb = n_samples * lx // lc
    lq = lx if latent else lc
    nq = lq // tq
    q_row = (lambda b, t: b * nq + t) if latent else (lambda b, t: n_samples * lx // tq + b * nq + t)
    in_specs = [
        pl.BlockSpec((4, QK_DIM), lambda b, h, t: (0, 0)),
        pl.BlockSpec((tq, LANES), lambda b, h, t: (q_row(b, t), qo + h)),
        pl.BlockSpec((lc, LANES), lambda b, h, t: (cb + b, ko + h)),
        pl.BlockSpec((lc, LANES), lambda b, h, t: (cb + b, vo + h)),
    ]
    args = [lam_vec, p, p, p]
    if latent:
        in_specs += [pl.BlockSpec((lx, LANES), lambda b, h, t: (b, ko + h)),
                     pl.BlockSpec((lx, LANES), lambda b, h, t: (b, vo + h))]
        args += [p, p]
    in_specs.append(pl.BlockSpec((1, LANES), lambda b, h, t: (0, h)))
    args.append(gain)
    return pl.pallas_call(
        functools.partial(_diff_kernel, lam_init=lam_init, has_x=latent, kb=min(1024, lx)),
        name="diff_attn_x" if latent else "diff_attn_ctx",
        grid=(n_samples, heads, nq),
        in_specs=in_specs,
        out_specs=pl.BlockSpec((tq, LANES), lambda b, h, t: (b * nq + t, h)),
        out_shape=jax.ShapeDtypeStruct((n_samples * lq, heads * V_DIM), BF16),
        compiler_params=_params(("arbitrary", "arbitrary", "arbitrary")),
    )(*args)


def _pack_pair(a, b):
    ua = pltpu.bitcast(a.astype(BF16).astype(F32), I32)
    ub = pltpu.bitcast(b.astype(BF16).astype(F32), I32)
    return ua | lax.shift_right_logical(ub, 16)


def _unpack_pair(w):
    return (pltpu.bitcast(w & -65536, F32).astype(BF16), pltpu.bitcast(w << 16, F32).astype(BF16))


def _store_packed_rows(ref, row0, n_rows, val):
    nw = val.shape[1] // (2 * LANES)
    for j in range(nw):
        w = _pack_pair(val[:, j * LANES:(j + 1) * LANES], val[:, (nw + j) * LANES:(nw + j + 1) * LANES])
        ref[pl.ds(row0 * nw + j, n_rows, stride=nw), :] = w


def _load_packed_rows(ref, row0, n_rows, nw, dst, dst_row0):
    for j in range(nw):
        hi, lo = _unpack_pair(ref[pl.ds(row0 * nw + j, n_rows, stride=nw), :])
        dst[dst_row0:dst_row0 + n_rows, j * LANES:(j + 1) * LANES] = hi
        dst[dst_row0:dst_row0 + n_rows, (nw + j) * LANES:(nw + j + 1) * LANES] = lo


def _outproj_kernel(*refs, d, wr, n_x_tiles, with_ctx):
    if with_ctx:
        retx_ref, difx_ref, retc_ref, difc_ref = refs[:4]
        is_ctx = pl.program_id(0) >= n_x_tiles
        ret = jnp.where(is_ctx, retc_ref[...], retx_ref[...])
        dif = jnp.where(is_ctx, difc_ref[...], difx_ref[...])
        refs = refs[4:]
    else:
        ret, dif = refs[0][...], refs[1][...]
        refs = refs[2:]
    w_ref, x_ref, mod_ref, g_ref, wr_ref, xm_ref, h_ref, aff_ref = refs
    y = _dot(ret, w_ref[0:wr, :]) + _dot(dif, w_ref[wr:, :])
    m = mod_ref[0]
    xm = x_ref[...] + m[:, 2 * d:3 * d] * y
    xm_ref[...] = xm
    h = _rmsnorm_mod(xm, g_ref[...], m[:, 3 * d:4 * d], m[:, 4 * d:5 * d])
    _store_packed_rows(h_ref, 0, h.shape[0], h)
    n_e = wr_ref.shape[1] // 2
    h_hi = h.astype(BF16)
    h_lo = (h - h_hi.astype(F32)).astype(BF16)
    both = _dot(h_hi, wr_ref[...])
    logits = both[:, :n_e] + both[:, n_e:] + _dot(h_lo, wr_ref[...])[:, :n_e]
    e = jnp.exp(logits - jnp.max(logits, axis=-1, keepdims=True))
    aff_ref[...] = e / jnp.sum(e, axis=-1, keepdims=True)


def out_projection(mix_x, mix_c, w_bf, xa, mod, gain, w_router, rows, n_x_tiles, tiles_per_sample, n_samples, tm):
    d = xa.shape[1]
    wr = mix_x[0].shape[1]
    wd = mix_x[1].shape[1]
    e = w_router.shape[1]
    nw = d // (2 * LANES)
    wr_hi = w_router.astype(BF16)
    wr_split = jnp.concatenate([wr_hi, (w_router - wr_hi.astype(F32)).astype(BF16)], axis=1)
    mod_idx = lambda i: (jnp.where(i < n_x_tiles, i // tiles_per_sample, n_samples), 0, 0)
    x_idx = lambda i: (jnp.minimum(i, n_x_tiles - 1), 0)
    c_idx = lambda i: (jnp.maximum(i - n_x_tiles, 0), 0)
    mix_specs = [pl.BlockSpec((tm, wr), x_idx), pl.BlockSpec((tm, wd), x_idx)]
    mix_args = list(mix_x)
    if mix_c is not None:
        mix_specs += [pl.BlockSpec((tm, wr), c_idx), pl.BlockSpec((tm, wd), c_idx)]
        mix_args += list(mix_c)
    return pl.pallas_call(
        functools.partial(_outproj_kernel, d=d, wr=wr, n_x_tiles=n_x_tiles, with_ctx=mix_c is not None),
        name="out_proj_router",
        grid=(rows // tm,),
        in_specs=mix_specs + [
                  pl.BlockSpec(w_bf.shape, lambda i: (0, 0)),
                  pl.BlockSpec((tm, d), lambda i: (i, 0)),
                  pl.BlockSpec((1, 1, N_MOD * d), mod_idx),
                  pl.BlockSpec((1, d), lambda i: (0, 0)),
                  pl.BlockSpec((d, 2 * e), lambda i: (0, 0))],
        out_specs=[pl.BlockSpec((tm, d), lambda i: (i, 0)),
                   pl.BlockSpec((tm * nw, LANES), lambda i: (i, 0)),
                   pl.BlockSpec((tm, e), lambda i: (i, 0))],
        out_shape=[jax.ShapeDtypeStruct((rows, d), F32),
                   jax.ShapeDtypeStruct((rows * nw, LANES), I32),
                   jax.ShapeDtypeStruct((rows, e), F32)],
        compiler_params=_params(("arbitrary",)),
    )(*mix_args, w_bf, xa, mod, gain, wr_split)


def _ones(mask, dtype):
    return jnp.where(mask, 1.0, 0.0).astype(dtype)


def _prefix_lanes(m):
    upper = _ones(lax.broadcasted_iota(I32, (LANES, LANES), 0) < lax.broadcasted_iota(I32, (LANES, LANES), 1), BF16)
    run = jnp.zeros((m.shape[0], 1), F32)
    outs = []
    for s in range(m.shape[1] // LANES):
        seg = m[:, s * LANES:(s + 1) * LANES]
        outs.append(_dot(seg, upper) + run)
        run = run + jnp.sum(seg.astype(F32), axis=1, keepdims=True)
    return jnp.concatenate(outs, axis=1)


def _topk_kernel(a_ref, idx_ref, gate_ref, dest_ref, start_ref, cnt_ref, posm_scr, dest_scr,
                 *, lg, cap, row_base, ent_base):
    g = pl.program_id(0)
    a = a_ref[...]
    n_e = a.shape[0]
    bits = pltpu.bitcast(a, I32)

    def bisect(_, lohi):
        lo, hi = lohi
        mid = lo + ((hi - lo) >> 1)
        ge = jnp.sum(_ones(bits >= mid, F32), axis=1, keepdims=True) >= cap
        return jnp.where(ge, mid, lo), jnp.where(ge, hi, mid)

    thr, _ = lax.fori_loop(0, 31, bisect,
                           (jnp.zeros((n_e, 1), I32), jnp.full((n_e, 1), 0x7F800000, I32)))
    gt = bits > thr
    eq = bits == thr
    need = cap - jnp.sum(_ones(gt, F32), axis=1, keepdims=True)
    sel = gt | (eq & (_prefix_lanes(_ones(eq, BF16)) < need))
    selb = _ones(sel, BF16)
    pos = _prefix_lanes(selb)
    start_t = jnp.sum(pos, axis=0, keepdims=True)
    lower = _ones(lax.broadcasted_iota(I32, (n_e, n_e), 0) > lax.broadcasted_iota(I32, (n_e, n_e), 1), BF16)
    base = (ent_base + g * (n_e * cap)).astype(F32)
    start_ref[...] = (start_t + base).astype(I32)
    cnt_ref[...] = jnp.sum(_ones(sel, F32), axis=0, keepdims=True).astype(I32)
    posm_scr[...] = jnp.where(sel, pos, -1.0)
    dest_scr[...] = start_t + base + _dot(lower, selb)

    tvals = (row_base + g * lg + lax.broadcasted_iota(I32, (1, lg), 1)).astype(F32)
    slot = lax.broadcasted_iota(I32, (cap, 1), 0).astype(F32)
    col = lax.broadcasted_iota(I32, (cap, n_e), 1)

    def compact(e, carry):
        ia, ga, da = carry
        m = posm_scr[pl.ds(e, 1), :] == slot
        i_e = jnp.sum(jnp.where(m, tvals, 0.0), axis=1, keepdims=True)
        g_e = jnp.sum(jnp.where(m, a_ref[pl.ds(e, 1), :], 0.0), axis=1, keepdims=True)
        d_e = jnp.sum(jnp.where(m, dest_scr[pl.ds(e, 1), :], 0.0), axis=1, keepdims=True)
        here = col == e
        return jnp.where(here, i_e, ia), jnp.where(here, g_e, ga), jnp.where(here, d_e, da)

    z = jnp.zeros((cap, n_e), F32)
    ia, ga, da = lax.fori_loop(0, n_e, compact, (z, z, z))
    idx_ref[0] = ia.astype(I32)
    gate_ref[0] = ga
    dest_ref[0] = da.astype(I32)


def expert_choice(aff_t, groups, lg, first_block, cap, row_base, ent_base):
    e = aff_t.shape[0]
    return pl.pallas_call(
        functools.partial(_topk_kernel, lg=lg, cap=cap, row_base=row_base, ent_base=ent_base),
        name="expert_choice",
        grid=(groups,),
        in_specs=[pl.BlockSpec((e, lg), lambda g: (0, first_block + g))],
        out_specs=[pl.BlockSpec((1, cap, e), lambda g: (g, 0, 0)),
                   pl.BlockSpec((1, cap, e), lambda g: (g, 0, 0)),
                   pl.BlockSpec((1, cap, e), lambda g: (g, 0, 0)),
                   pl.BlockSpec((1, lg), lambda g: (0, g)),
                   pl.BlockSpec((1, lg), lambda g: (0, g))],
        out_shape=[jax.ShapeDtypeStruct((groups, cap, e), I32),
                   jax.ShapeDtypeStruct((groups, cap, e), F32),
                   jax.ShapeDtypeStruct((groups, cap, e), I32),
                   jax.ShapeDtypeStruct((1, groups * lg), I32),
                   jax.ShapeDtypeStruct((1, groups * lg), I32)],
        scratch_shapes=[pltpu.VMEM((e, lg), F32), pltpu.VMEM((e, lg), F32)],
        compiler_params=_params(("arbitrary",)),
    )(aff_t)


ROWS_PER_STEP = 1024


def _gather_kernel(si_ref, src_ref, o_ref, sem, *, nw, per):
    base = pl.program_id(0) * per

    def issue(i, carry):
        s = pl.multiple_of(si_ref[base + i] * nw, nw)
        pltpu.make_async_copy(src_ref.at[pl.ds(s, nw)], o_ref.at[pl.ds(pl.multiple_of(i * nw, nw), nw)], sem).start()
        return carry

    lax.fori_loop(0, per, issue, 0, unroll=8)
    pltpu.make_async_copy(src_ref.at[pl.ds(0, per * nw)], o_ref, sem).wait()


def gather_rows(src, si, nw):
    n = si.shape[0]
    per = ROWS_PER_STEP
    assert n % per == 0
    return pl.pallas_call(
        functools.partial(_gather_kernel, nw=nw, per=per),
        name="gather_rows",
        grid_spec=pltpu.PrefetchScalarGridSpec(
            num_scalar_prefetch=1, grid=(n // per,),
            in_specs=[pl.BlockSpec(memory_space=pl.ANY)],
            out_specs=pl.BlockSpec((per * nw, src.shape[1]), lambda g, si: (g, 0)),
            scratch_shapes=[pltpu.SemaphoreType.DMA(())]),
        out_shape=jax.ShapeDtypeStruct((n * nw, src.shape[1]), src.dtype),
        compiler_params=_params(("arbitrary",)),
    )(si, src)


def _scatter_kernel(di_ref, x_ref, dst_ref, sem, *, nw, per):
    base = pl.program_id(0) * per

    def issue(i, carry):
        t = pl.multiple_of(di_ref[base + i] * nw, nw)
        pltpu.make_async_copy(x_ref.at[pl.ds(pl.multiple_of(i * nw, nw), nw)], dst_ref.at[pl.ds(t, nw)], sem).start()
        return carry

    lax.fori_loop(0, per, issue, 0, unroll=8)
    pltpu.make_async_copy(x_ref, dst_ref.at[pl.ds(0, per * nw)], sem).wait()


def scatter_rows(x, di, nw):
    n = di.shape[0]
    per = ROWS_PER_STEP
    assert n % per == 0
    return pl.pallas_call(
        functools.partial(_scatter_kernel, nw=nw, per=per),
        name="scatter_rows",
        grid_spec=pltpu.PrefetchScalarGridSpec(
            num_scalar_prefetch=1, grid=(n // per,),
            in_specs=[pl.BlockSpec((per * nw, x.shape[1]), lambda g, di: (g, 0))],
            out_specs=pl.BlockSpec(memory_space=pl.ANY),
            scratch_shapes=[pltpu.SemaphoreType.DMA(())]),
        out_shape=jax.ShapeDtypeStruct((n * nw, x.shape[1]), x.dtype),
        compiler_params=_params(("arbitrary",)),
    )(di, x)


def _ffn_kernel(xs_ref, wg_ref, wu_ref, wd_ref, gate_ref, ys_ref, xb_scr, hid_scr, wd_scr, *, ts, nw, tf):
    f = pl.program_id(1)
    s = xb_scr.shape[0]

    @pl.when(f == 0)
    def _():
        _load_packed_rows(xs_ref.at[0], 0, s, nw, xb_scr, 0)

    wg = wg_ref[0, 0].astype(BF16)
    wu = wu_ref[0, 0].astype(BF16)
    for r in range(s // ts):
        rows = slice(r * ts, (r + 1) * ts)
        xb = xb_scr[rows, :]
        hid_scr[f, rows, :] = (_silu(_dot(xb, wg)) * _dot(xb, wu)).astype(BF16)
    wd_scr[pl.ds(pl.multiple_of(f * tf, tf), tf), :] = wd_ref[0, 0].astype(BF16)

    @pl.when(f == pl.num_programs(1) - 1)
    def _():
        for r in range(s // ts):
            rows = slice(r * ts, (r + 1) * ts)
            hid = jnp.concatenate([hid_scr[c, rows, :] for c in range(hid_scr.shape[0])], axis=1)
            y = _dot(hid, wd_scr[...])
            _store_packed_rows(ys_ref.at[0], r * ts, ts, y * gate_ref[0, rows, :])


def expert_ffn(xs, w_gate, w_up, w_down, layer, gates, d):
    nw = d // (2 * LANES)
    e = xs.shape[0]
    s = xs.shape[1] // nw
    ff = w_gate.shape[3]
    tf = 256 if ff % 256 == 0 else ff
    once = pl.Buffered(1)
    ts = max(t for t in range(8, 577, 8) if s % t == 0)
    return pl.pallas_call(
        functools.partial(_ffn_kernel, ts=ts, nw=nw, tf=tf),
        name="expert_ffn",
        grid=(e, ff // tf),
        in_specs=[pl.BlockSpec((1, s * nw, LANES), lambda i, f: (i, 0, 0)),
                  pl.BlockSpec((1, 1, d, tf), lambda i, f: (layer, i, 0, f)),
                  pl.BlockSpec((1, 1, d, tf), lambda i, f: (layer, i, 0, f)),
                  pl.BlockSpec((1, 1, tf, d), lambda i, f: (layer, i, f, 0)),
                  pl.BlockSpec((1, s, 1), lambda i, f: (i, 0, 0))],
        out_specs=pl.BlockSpec((1, s * nw, LANES), lambda i, f: (i, 0, 0), pipeline_mode=once),
        out_shape=jax.ShapeDtypeStruct((e, s * nw, LANES), I32),
        scratch_shapes=[pltpu.VMEM((s, d), BF16), pltpu.VMEM((ff // tf, s, tf), BF16),
                        pltpu.VMEM((ff, d), BF16)],
        compiler_params=_params(("arbitrary", "arbitrary")),
    )(xs, w_gate, w_up, w_down, gates)


COMB_ROWS = 256
COMB_ENT = 256


def _combine_kernel(kb_ref, ch_ref, flag_ref, ys_ref, xm_ref, start_ref, cnt_ref, mod_ref, *rest, d, final):
    if final:
        gf_ref, o_ref, chunk_scr = rest
    else:
        o_ref, chunk_scr = rest
    p = pl.program_id(0)
    flags = flag_ref[p]

    @pl.when((flags & 2) != 0)
    def _():
        o_ref[...] = jnp.zeros_like(o_ref)

    @pl.when((flags & 8) != 0)
    def _():
        _load_packed_rows(ys_ref, 0, COMB_ENT, d // (2 * LANES), chunk_scr, 0)

    @pl.when((flags & 1) != 0)
    def _():
        gpos = ch_ref[p] * COMB_ENT + lax.broadcasted_iota(I32, (COMB_ROWS, COMB_ENT), 1)
        st = start_ref[...]
        onehot = _ones((gpos >= st) & (gpos < st + cnt_ref[...]), BF16)
        o_ref[...] += _dot(onehot, chunk_scr[...])

    @pl.when((flags & 4) != 0)
    def _():
        m = mod_ref[0]
        out = xm_ref[...] + m[:, 5 * d:6 * d] * o_ref[...]
        if final:
            out = out * lax.rsqrt(jnp.mean(out * out, axis=-1, keepdims=True) + EPS) * gf_ref[...]
        o_ref[...] = out


def combine(ys_sorted, xm, start_col, cnt_col, mod, kb, ch, flags, rows, tiles_per_sample, n_x_tiles,
            n_samples, final_gain):
    d = xm.shape[1]
    nw = d // (2 * LANES)
    n_pairs = kb.shape[0]
    mod_idx = lambda p, kb, ch, fl: (jnp.where(kb[p] < n_x_tiles, kb[p] // tiles_per_sample, n_samples), 0, 0)
    in_specs = [pl.BlockSpec((COMB_ENT * nw, LANES), lambda p, kb, ch, fl: (ch[p], 0)),
                pl.BlockSpec((COMB_ROWS, d), lambda p, kb, ch, fl: (kb[p], 0)),
                pl.BlockSpec((COMB_ROWS, 1), lambda p, kb, ch, fl: (kb[p], 0)),
                pl.BlockSpec((COMB_ROWS, 1), lambda p, kb, ch, fl: (kb[p], 0)),
                pl.BlockSpec((1, 1, N_MOD * d), mod_idx)]
    args = [ys_sorted, xm, start_col, cnt_col, mod]
    if final_gain is not None:
        in_specs.append(pl.BlockSpec((1, d), lambda p, kb, ch, fl: (0, 0)))
        args.append(final_gain)
    return pl.pallas_call(
        functools.partial(_combine_kernel, d=d, final=final_gain is not None),
        name="combine",
        grid_spec=pltpu.PrefetchScalarGridSpec(
            num_scalar_prefetch=3, grid=(n_pairs,),
            in_specs=in_specs,
            out_specs=pl.BlockSpec((COMB_ROWS, d), lambda p, kb, ch, fl: (kb[p], 0)),
            scratch_shapes=[pltpu.VMEM((COMB_ENT, d), BF16)]),
        out_shape=jax.ShapeDtypeStruct((rows, d), F32),
        compiler_params=_params(("arbitrary",)),
    )(kb, ch, flags, *args)


def _combine_schedule(start_row, n_blocks, n_chunks):
    blk_start = start_row[::COMB_ROWS]
    blk_end = jnp.concatenate([blk_start[1:], jnp.full((1,), n_chunks * COMB_ENT, I32)])
    lo = jnp.minimum(blk_start // COMB_ENT, n_chunks - 1)
    hi = jnp.maximum(lo, (blk_end - 1) // COMB_ENT)
    cnt = hi - lo + 1
    off = jnp.cumsum(cnt) - cnt
    total = off[-1] + cnt[-1]
    n_pairs = n_blocks + n_chunks
    pidx = jnp.arange(n_pairs, dtype=I32)
    kb = jnp.sum((off[None, :] <= pidx[:, None]).astype(I32), axis=1) - 1
    valid = pidx < total
    ch = jnp.where(valid, lo[kb] + pidx - off[kb], hi[n_blocks - 1]).astype(I32)
    first = valid & (pidx == off[kb])
    last = valid & (pidx == off[kb] + cnt[kb] - 1)
    fresh = jnp.concatenate([jnp.ones((1,), bool), ch[1:] != ch[:-1]])
    flags = valid.astype(I32) + 2 * first.astype(I32) + 4 * last.astype(I32) + 8 * fresh.astype(I32)
    return kb, ch, flags


def _rope_tables(lx, n_ctx_rows, n_samples):
    n_freq = QK_DIM // 4
    freqs = ROPE_BASE ** (-jnp.arange(n_freq, dtype=F32) / n_freq)
    rows = lx // GRID_W
    row = jnp.repeat(jnp.arange(rows, dtype=F32), GRID_W)
    col = jnp.tile(jnp.arange(GRID_W, dtype=F32), rows)
    ang = jnp.concatenate([row[:, None] * freqs, col[:, None] * freqs], axis=-1)
    cos, sin = jnp.cos(ang), jnp.sin(ang)
    ct = jnp.tile(jnp.concatenate([cos, cos], axis=-1), (n_samples, LANES // QK_DIM))
    st = jnp.tile(jnp.concatenate([-sin, sin], axis=-1), (n_samples, LANES // QK_DIM))
    ct = jnp.concatenate([ct, jnp.ones((n_ctx_rows, LANES), F32)], axis=0)
    st = jnp.concatenate([st, jnp.zeros((n_ctx_rows, LANES), F32)], axis=0)
    return ct, st


def kernel(x, c, ctx, c_ctx, w_ada, b_ada, norm_mix, norm_ffn, w_in, w_out, ret_log_decay, ret_norm,
           diff_lambda, diff_norm, w_router, w_gate, w_up, w_down, norm_final):
    n_b, lx, d = x.shape
    lc = ctx.shape[1]
    depth = w_ada.shape[0]
    heads = ret_log_decay.shape[-1]
    dheads = diff_norm.shape[-1] // V_DIM
    n_e = w_router.shape[-1]
    rx, rc = n_b * lx, n_b * lc
    cap_x = CAPACITY_FACTOR * lx // n_e
    cap_c = CAPACITY_FACTOR * lc // n_e
    tm = 512 if (lx % 512 == 0 and rx % 512 == 0 and rc % 512 == 0) else 256
    tmo = 256
    assert lx % lc == 0 and lc % tmo == 0 and heads % 2 == 0 and n_b < MOD_ROWS

    qk_scale = QK_DIM ** -0.5
    n_rq = heads * QK_DIM // LANES
    rope = tuple([True] * (2 * n_rq) + [False] * (2 * heads) + [True] * (2 * dheads) + [False] * dheads)
    slab_scale = tuple([qk_scale] * n_rq + [1.0] * (n_rq + 2 * heads) + [qk_scale * math.log2(math.e)] * dheads
                       + [1.0] * (2 * dheads))
    w_in_bf = _prep_w_in(w_in, slab_scale)

    xa = jnp.concatenate([x.reshape(rx, d), ctx.reshape(rc, d)], axis=0)
    cc = jnp.concatenate([c, c_ctx[None], jnp.zeros((MOD_ROWS - n_b - 1, d), F32)], axis=0)
    mods = ada_modulation(cc, w_ada, b_ada)
    ct, st = _rope_tables(lx, rc, n_b)
    log_g = jnp.log1p(-jnp.exp(ret_log_decay.astype(F32)))
    out = None

    for layer in range(depth):
        last = layer == depth - 1
        lam_init = 0.8 - 0.6 * math.exp(-0.3 * layer)
        mod = mods[layer].reshape(MOD_ROWS, 1, N_MOD * d)
        p = in_projection(xa, mod, norm_mix[layer][None], w_in_bf[layer], ct, st,
                          rx // tm, lx // tm, n_b, rope, tm)

        ret = retention(p, log_g[layer], ret_norm[layer][None], n_b, lx, lc, heads, not last)
        dif_x = diff_attention(p, diff_lambda[layer], diff_norm[layer][None], n_b, lx, lc, heads, dheads,
                               lam_init, True, 1024 if lx % 1024 == 0 else 256)
        if last:
            rows = rx
            mix_c = None
        else:
            rows = rx + rc
            dif_c = diff_attention(p, diff_lambda[layer], diff_norm[layer][None], n_b, lx, lc, heads, dheads,
                                   lam_init, False, lc)
            mix_c = (ret[1], dif_c)

        xm, h2, aff = out_projection((ret[0], dif_x), mix_c, w_out[layer].astype(BF16), xa, mod,
                                     norm_ffn[layer][None], w_router[layer], rows, rx // tmo, lx // tmo, n_b, tmo)
        aff_t = aff.T

        sel = [expert_choice(aff_t, n_b, lx, 0, cap_x, 0, 0)]
        if not last:
            sel.append(expert_choice(aff_t, n_b, lc, rx // lc, cap_c, rx, n_b * n_e * cap_x))
        to_es = lambda t: jnp.transpose(t, (2, 0, 1)).reshape(n_e, -1)
        idx = jnp.concatenate([to_es(s[0]) for s in sel], axis=1)
        gates = jnp.concatenate([to_es(s[1]) for s in sel], axis=1)
        dest = jnp.concatenate([to_es(s[2]) for s in sel], axis=1)
        start_row = jnp.concatenate([s[3][0] for s in sel])
        cnt_row = jnp.concatenate([s[4][0] for s in sel])
        slots = idx.shape[1]
        n_ent = n_e * slots

        nw = d // (2 * LANES)
        xs = gather_rows(h2, idx.reshape(-1), nw)
        ys = expert_ffn(xs.reshape(n_e, slots * nw, LANES), w_gate, w_up, w_down, layer,
                        gates.reshape(n_e, slots, 1), d)
        ys_sorted = scatter_rows(ys.reshape(n_ent * nw, LANES), dest.reshape(-1), nw)

        kb, ch, flags = _combine_schedule(start_row, rows // COMB_ROWS, n_ent // COMB_ENT)
        xa_new = combine(ys_sorted, xm, start_row[:, None], cnt_row[:, None], mod, kb, ch, flags, rows,
                         lx // COMB_ROWS, rx // COMB_ROWS, n_b, norm_final[None] if last else None)
        if last:
            out = xa_new
        else:
            xa = xa_new
    return out.reshape(n_b, lx, d)
```

```python
import functools
import math

import jax
import jax.numpy as jnp
from jax import lax
from jax.experimental import pallas as pl
from jax.experimental.pallas import tpu as pltpu

F32 = jnp.float32
BF16 = jnp.bfloat16
I32 = jnp.int32

EPS = 1e-6
N_MOD = 6
GRID_W = 64
ROPE_BASE = 10000.0
QK_DIM = 64
V_DIM = 128
LANES = 128
CHUNK = 128
CAPACITY_FACTOR = 2
MOD_ROWS = 8
VMEM_LIMIT = 56 * 1024 * 1024


def _params(sem, vmem=VMEM_LIMIT):
    return pltpu.CompilerParams(dimension_semantics=sem, vmem_limit_bytes=vmem)


def _silu(x):
    return x / (1.0 + jnp.exp(-x))


def _dot(a, b):
    return jnp.dot(a, b, preferred_element_type=F32)


def _dot_nt(a, b):
    return lax.dot_general(a, b, (((1,), (1,)), ((), ())), preferred_element_type=F32)


def _ada_kernel(c_ref, w_ref, b_ref, o_ref):
    s = _silu(c_ref[...]).astype(BF16)
    o_ref[0] = _dot(s, w_ref[0].astype(BF16)) + b_ref[0]


def ada_modulation(cc, w_ada, b_ada):
    depth, d, n = w_ada.shape
    tn = 1024 if n % 1024 == 0 else n
    return pl.pallas_call(
        _ada_kernel,
        name="ada_mod",
        grid=(depth, n // tn),
        in_specs=[pl.BlockSpec((MOD_ROWS, d), lambda l, j: (0, 0)),
                  pl.BlockSpec((1, d, tn), lambda l, j: (l, 0, j)),
                  pl.BlockSpec((1, 1, tn), lambda l, j: (l, 0, j))],
        out_specs=pl.BlockSpec((1, MOD_ROWS, tn), lambda l, j: (l, 0, j)),
        out_shape=jax.ShapeDtypeStruct((depth, MOD_ROWS, n), F32),
        compiler_params=_params(("arbitrary", "arbitrary")),
    )(cc, w_ada, b_ada.reshape(depth, 1, n))


def _rmsnorm_mod(x, gain, shift, scale):
    y = x * lax.rsqrt(jnp.mean(x * x, axis=-1, keepdims=True) + EPS) * gain
    return y * (1.0 + scale) + shift


MXU_N = 256


def _group_of(lane):
    return (lane >> 6) & 1


def _swap_halves(a):
    lane = lax.broadcasted_iota(I32, a.shape, 1)
    return jnp.where((lane & 32) == 0, pltpu.roll(a, 96, 1), pltpu.roll(a, 32, 1))


def _inproj_kernel(x_ref, mod_ref, g_ref, w_ref, ct_ref, st_ref, o_ref, *, d, rope):
    m = mod_ref[0]
    h = _rmsnorm_mod(x_ref[...], g_ref[...], m[:, 0:d], m[:, d:2 * d]).astype(BF16)
    per = MXU_N // LANES
    for s in range(w_ref.shape[2] // MXU_N):
        acc = _dot(h, w_ref[0, :, s * MXU_N:(s + 1) * MXU_N])
        for t in range(per):
            a = acc[:, t * LANES:(t + 1) * LANES]
            if rope[s * per + t]:
                a = a * ct_ref[...] + _swap_halves(a) * st_ref[...]
            c0 = s * MXU_N + t * LANES
            o_ref[:, c0:c0 + LANES] = a.astype(BF16)


def in_projection(xa, mod, gain, w_bf, layer, ct, st, n_x_tiles, tiles_per_sample, n_samples, rope, tm):
    r, d = xa.shape
    n = w_bf.shape[2]
    mod_idx = lambda i: (jnp.where(i < n_x_tiles, i // tiles_per_sample, n_samples), 0, 0)
    return pl.pallas_call(
        functools.partial(_inproj_kernel, d=d, rope=rope),
        name="in_proj",
        grid=(r // tm,),
        in_specs=[pl.BlockSpec((tm, d), lambda i: (i, 0)),
                  pl.BlockSpec((1, 1, N_MOD * d), mod_idx),
                  pl.BlockSpec((1, d), lambda i: (0, 0)),
                  pl.BlockSpec((1, d, n), lambda i: (layer, 0, 0), pipeline_mode=pl.Buffered(1)),
                  pl.BlockSpec((tm, LANES), lambda i: (i, 0)),
                  pl.BlockSpec((tm, LANES), lambda i: (i, 0))],
        out_specs=pl.BlockSpec((tm, n), lambda i: (i, 0)),
        out_shape=jax.ShapeDtypeStruct((r, n), BF16),
        compiler_params=_params(("arbitrary",)),
    )(xa, mod, gain, w_bf, ct, st)


def _prep_w_in(w_in, slab_scale):
    col_scale = jnp.repeat(jnp.asarray(slab_scale, F32), LANES)
    return (w_in * col_scale).astype(BF16)


def _ret_kernel(lg_ref, qx_ref, kx_ref, vx_ref, gx_ref, qc_ref, kc_ref, vc_ref, gc_ref, gain_ref,
                *rest, lx, lc, ctx_out):
    if ctx_out:
        ox_ref, oc_ref, accx, accc = rest
    else:
        ox_ref, accx = rest
    pair = pl.program_id(1)
    w2 = 2 * V_DIM
    lane = lax.broadcasted_iota(I32, (CHUNK, LANES), 1)
    dif = (lax.broadcasted_iota(I32, (CHUNK, CHUNK), 0) - lax.broadcasted_iota(I32, (CHUNK, CHUNK), 1)).astype(F32)
    pos = lax.broadcasted_iota(I32, (CHUNK, 1), 0).astype(F32)
    cpos = lax.broadcasted_iota(I32, (lc, 1), 0).astype(F32)
    c_len = float(CHUNK)
    lgf = [lg_ref[0, 2 * pair + hh] for hh in range(2)]
    lgb = [lg_ref[1, 2 * pair + hh] for hh in range(2)]

    def per_head(fn):
        vals = [fn(hh) for hh in range(2)]
        return jnp.concatenate([jnp.broadcast_to(v, (v.shape[0], V_DIM)) for v in vals], axis=1)

    decay = per_head(lambda hh: jnp.where(dif >= 0, jnp.exp(lgf[hh] * jnp.maximum(dif, 0.0)),
                                          jnp.exp(lgb[hh] * jnp.maximum(-dif, 0.0))))
    qd_f = per_head(lambda hh: jnp.exp(lgf[hh] * (pos + 1.0)))
    kd_f = per_head(lambda hh: jnp.exp(lgf[hh] * (c_len - 1.0 - pos)))
    qd_b = per_head(lambda hh: jnp.exp(lgb[hh] * (c_len - pos)))
    kd_b = per_head(lambda hh: jnp.exp(lgb[hh] * pos))
    one = jnp.ones((1, 1), F32)
    cd_f = per_head(lambda hh: jnp.exp(lgf[hh] * c_len) * one)
    cd_b = per_head(lambda hh: jnp.exp(lgb[hh] * c_len) * one)
    srow = lax.broadcasted_iota(I32, (LANES, w2), 0)
    scol = lax.broadcasted_iota(I32, (LANES, w2), 1)
    diag = (_group_of(srow) == 0) == (scol < V_DIM)
    zeros_v = jnp.zeros((CHUNK, V_DIM), BF16)

    def kv_state(k, v_scaled):
        return jnp.where(diag, _dot(k.astype(F32).T.astype(BF16), v_scaled.astype(BF16)), 0.0)

    kc = kc_ref[...]
    vc = vc_ref[...].astype(F32)
    s_f = kv_state(kc, vc * per_head(lambda hh: jnp.exp(lgf[hh] * (lc - 1.0 - cpos))))
    s_b = kv_state(kc, vc * per_head(lambda hh: jnp.exp(lgb[hh] * cpos)))

    def sweeps(q_ref, k_ref, v_ref, g_ref, o_ref, acc_ref, n, s_f0, s_b0):
        def rows_of(c):
            return pl.ds(pl.multiple_of(c * CHUNK, CHUNK), CHUNK)

        def forward(c, s):
            rows = rows_of(c)
            q, k, v = q_ref[rows, :], k_ref[rows, :], v_ref[rows, :]
            k_bd = jnp.concatenate([jnp.where(_group_of(lane) == 0, k, 0), jnp.where(_group_of(lane) == 1, k, 0)],
                                   axis=0)
            v_bd = jnp.concatenate([jnp.concatenate([v[:, :V_DIM], zeros_v], axis=1),
                                    jnp.concatenate([zeros_v, v[:, V_DIM:]], axis=1)], axis=0)
            sc = _dot_nt(q, k_bd) * decay
            acc_ref[rows, :] = _dot(sc.astype(BF16), v_bd) + _dot(q, s.astype(BF16)) * qd_f
            return s * cd_f + kv_state(k, v.astype(F32) * kd_f)

        def backward(t, s):
            rows = rows_of(n - 1 - t)
            q, k, v = q_ref[rows, :], k_ref[rows, :], v_ref[rows, :]
            y = acc_ref[rows, :] + _dot(q, s.astype(BF16)) * qd_b
            for hh in range(2):
                vsl = slice(hh * V_DIM, (hh + 1) * V_DIM)
                yh = y[:, vsl]
                yc = yh - jnp.mean(yh, axis=-1, keepdims=True)
                out = yc * lax.rsqrt(jnp.mean(yc * yc, axis=-1, keepdims=True) + EPS) * gain_ref[:, vsl]
                o_ref[rows, vsl] = (_silu(g_ref[rows, vsl].astype(F32)) * out).astype(BF16)
            return s * cd_b + kv_state(k, v.astype(F32) * kd_b)

        unroll = 4 if n % 4 == 0 else n
        lax.fori_loop(0, n, forward, s_f0, unroll=unroll)
        lax.fori_loop(0, n, backward, s_b0, unroll=unroll)

    sweeps(qx_ref, kx_ref, vx_ref, gx_ref, ox_ref, accx, lx // CHUNK, s_f, s_b)
    if ctx_out:
        zero = jnp.zeros((LANES, w2), F32)
        sweeps(qc_ref, kc_ref, vc_ref, gc_ref, oc_ref, accc, lc // CHUNK, zero, zero)


def retention(p, lg, gain, n_samples, lx, lc, heads, ctx_out):
    pairs = heads // 2
    half = heads // 2
    cb = n_samples * lx // lc
    w2 = 2 * V_DIM
    in_specs = [
        pl.BlockSpec(memory_space=pltpu.SMEM),
        pl.BlockSpec((lx, LANES), lambda b, h: (b, h)),
        pl.BlockSpec((lx, LANES), lambda b, h: (b, half + h)),
        pl.BlockSpec((lx, w2), lambda b, h: (b, half + h)),
        pl.BlockSpec((lx, w2), lambda b, h: (b, heads + h)),
        pl.BlockSpec((lc, LANES), lambda b, h: (cb + b, h)),
        pl.BlockSpec((lc, LANES), lambda b, h: (cb + b, half + h)),
        pl.BlockSpec((lc, w2), lambda b, h: (cb + b, half + h)),
        pl.BlockSpec((lc, w2), lambda b, h: (cb + b, heads + h)),
        pl.BlockSpec((1, w2), lambda b, h: (0, h)),
    ]
    out_specs = [pl.BlockSpec((lx, w2), lambda b, h: (b, h))]
    out_shape = [jax.ShapeDtypeStruct((n_samples * lx, heads * V_DIM), BF16)]
    scratch = [pltpu.VMEM((lx, w2), F32)]
    if ctx_out:
        out_specs.append(pl.BlockSpec((lc, w2), lambda b, h: (b, h)))
        out_shape.append(jax.ShapeDtypeStruct((n_samples * lc, heads * V_DIM), BF16))
        scratch.append(pltpu.VMEM((lc, w2), F32))
    return pl.pallas_call(
        functools.partial(_ret_kernel, lx=lx, lc=lc, ctx_out=ctx_out),
        name="retention",
        grid=(n_samples, pairs),
        in_specs=in_specs, out_specs=out_specs, out_shape=out_shape, scratch_shapes=scratch,
        compiler_params=_params(("arbitrary", "arbitrary")),
    )(lg, p, p, p, p, p, p, p, p, gain)


def _diff_kernel(lam_ref, q_ref, kc_ref, vc_ref, *rest, lam_init, has_x, kb):
    if has_x:
        kx_ref, vx_ref, gain_ref, o_ref = rest
    else:
        gain_ref, o_ref = rest
    lv = lam_ref[...]
    lam = (jnp.exp(jnp.sum(lv[0:1] * lv[1:2], axis=-1, keepdims=True))
           - jnp.exp(jnp.sum(lv[2:3] * lv[3:4], axis=-1, keepdims=True)) + lam_init)
    q = q_ref[...]
    tq = q.shape[0]
    lane = lax.broadcasted_iota(I32, q.shape, 1)
    blocks = [(kc_ref, vc_ref, 0, kc_ref.shape[0])]
    if has_x:
        blocks += [(kx_ref, vx_ref, r0, kb) for r0 in range(0, kx_ref.shape[0], kb)]
    v_ext = [jnp.concatenate([v_ref[r0:r0 + n, :], jnp.ones((n, V_DIM), BF16)], axis=1) for _, v_ref, r0, n in blocks]
    heads_out = []
    for t in range(2):
        qm = jnp.where(_group_of(lane) == t, q, 0)
        m = jnp.full((tq, 1), -jnp.inf, F32)
        acc = jnp.zeros((tq, 2 * V_DIM), F32)
        for (k_ref, _, r0, n), ve in zip(blocks, v_ext):
            s = _dot_nt(qm, k_ref[r0:r0 + n, :])
            m_new = jnp.maximum(m, jnp.max(s, axis=-1, keepdims=True))
            acc = acc * jnp.exp2(m - m_new) + _dot(jnp.exp2(s - m_new).astype(BF16), ve)
            m = m_new
        heads_out.append(acc[:, :V_DIM] * (1.0 / acc[:, V_DIM:V_DIM + 1]))
    out = heads_out[0] - lam * heads_out[1]
    y = out * lax.rsqrt(jnp.mean(out * out, axis=-1, keepdims=True) + EPS) * gain_ref[...]
    o_ref[...] = (y * (1.0 - lam_init)).astype(BF16)


def diff_attention(p, lam_vec, gain, n_samples, lx, lc, ret_heads, heads, lam_init, latent, tq):
    qo = 3 * ret_heads
    ko = qo + heads
    vo = ko + heads
    cb = n_samples * lx // lc
    lq = lx if latent else lc
    nq = lq // tq
    q_row = (lambda b, t: b * nq + t) if latent else (lambda b, t: n_samples * lx // tq + b * nq + t)
    in_specs = [
        pl.BlockSpec((4, QK_DIM), lambda b, h, t: (0, 0)),
        pl.BlockSpec((tq, LANES), lambda b, h, t: (q_row(b, t), qo + h)),
        pl.BlockSpec((lc, LANES), lambda b, h, t: (cb + b, ko + h)),
        pl.BlockSpec((lc, LANES), lambda b, h, t: (cb + b, vo + h)),
    ]
    args = [lam_vec, p, p, p]
    if latent:
        in_specs += [pl.BlockSpec((lx, LANES), lambda b, h, t: (b, ko + h)),
                     pl.BlockSpec((lx, LANES), lambda b, h, t: (b, vo + h))]
        args += [p, p]
    in_specs.append(pl.BlockSpec((1, LANES), lambda b, h, t: (0, h)))
    args.append(gain)
    return pl.pallas_call(
        functools.partial(_diff_kernel, lam_init=lam_init, has_x=latent, kb=min(1024, lx)),
        name="diff_attn_x" if latent else "diff_attn_ctx",
        grid=(n_samples, heads, nq),
        in_specs=in_specs,
        out_specs=pl.BlockSpec((tq, LANES), lambda b, h, t: (b * nq + t, h)),
        out_shape=jax.ShapeDtypeStruct((n_samples * lq, heads * V_DIM), BF16),
        compiler_params=_params(("arbitrary", "arbitrary", "arbitrary")),
    )(*args)


def _pack_pair(a, b):
    ua = pltpu.bitcast(a.astype(BF16).astype(F32), I32)
    ub = pltpu.bitcast(b.astype(BF16).astype(F32), I32)
    return ua | lax.shift_right_logical(ub, 16)


def _unpack_pair(w):
    return (pltpu.bitcast(w & -65536, F32).astype(BF16), pltpu.bitcast(w << 16, F32).astype(BF16))


def _store_packed_rows(ref, row0, n_rows, val):
    nw = val.shape[1] // (2 * LANES)
    for j in range(nw):
        w = _pack_pair(val[:, j * LANES:(j + 1) * LANES], val[:, (nw + j) * LANES:(nw + j + 1) * LANES])
        ref[pl.ds(row0 * nw + j, n_rows, stride=nw), :] = w


def _load_packed_rows(ref, row0, n_rows, nw, dst, dst_row0):
    for j in range(nw):
        hi, lo = _unpack_pair(ref[pl.ds(row0 * nw + j, n_rows, stride=nw), :])
        dst[dst_row0:dst_row0 + n_rows, j * LANES:(j + 1) * LANES] = hi
        dst[dst_row0:dst_row0 + n_rows, (nw + j) * LANES:(nw + j + 1) * LANES] = lo


def _outproj_kernel(*refs, d, wr, n_x_tiles, with_ctx):
    if with_ctx:
        retx_ref, difx_ref, retc_ref, difc_ref = refs[:4]
        is_ctx = pl.program_id(0) >= n_x_tiles
        ret = jnp.where(is_ctx, retc_ref[...], retx_ref[...])
        dif = jnp.where(is_ctx, difc_ref[...], difx_ref[...])
        refs = refs[4:]
    else:
        ret, dif = refs[0][...], refs[1][...]
        refs = refs[2:]
    w_ref, x_ref, mod_ref, g_ref, wr_ref, xm_ref, h_ref, aff_ref = refs
    y = _dot(ret, w_ref[0:wr, :]) + _dot(dif, w_ref[wr:, :])
    m = mod_ref[0]
    xm = x_ref[...] + m[:, 2 * d:3 * d] * y
    xm_ref[...] = xm
    h = _rmsnorm_mod(xm, g_ref[...], m[:, 3 * d:4 * d], m[:, 4 * d:5 * d])
    _store_packed_rows(h_ref, 0, h.shape[0], h)
    n_e = wr_ref.shape[1] // 2
    h_hi = h.astype(BF16)
    h_lo = (h - h_hi.astype(F32)).astype(BF16)
    both = _dot(h_hi, wr_ref[...])
    logits = both[:, :n_e] + both[:, n_e:] + _dot(h_lo, wr_ref[...])[:, :n_e]
    e = jnp.exp(logits - jnp.max(logits, axis=-1, keepdims=True))
    aff_ref[...] = e / jnp.sum(e, axis=-1, keepdims=True)


def out_projection(mix_x, mix_c, w_bf, xa, mod, gain, w_router, rows, n_x_tiles, tiles_per_sample, n_samples, tm):
    d = xa.shape[1]
    wr = mix_x[0].shape[1]
    wd = mix_x[1].shape[1]
    e = w_router.shape[1]
    nw = d // (2 * LANES)
    wr_hi = w_router.astype(BF16)
    wr_split = jnp.concatenate([wr_hi, (w_router - wr_hi.astype(F32)).astype(BF16)], axis=1)
    mod_idx = lambda i: (jnp.where(i < n_x_tiles, i // tiles_per_sample, n_samples), 0, 0)
    x_idx = lambda i: (jnp.minimum(i, n_x_tiles - 1), 0)
    c_idx = lambda i: (jnp.maximum(i - n_x_tiles, 0), 0)
    mix_specs = [pl.BlockSpec((tm, wr), x_idx), pl.BlockSpec((tm, wd), x_idx)]
    mix_args = list(mix_x)
    if mix_c is not None:
        mix_specs += [pl.BlockSpec((tm, wr), c_idx), pl.BlockSpec((tm, wd), c_idx)]
        mix_args += list(mix_c)
    return pl.pallas_call(
        functools.partial(_outproj_kernel, d=d, wr=wr, n_x_tiles=n_x_tiles, with_ctx=mix_c is not None),
        name="out_proj_router",
        grid=(rows // tm,),
        in_specs=mix_specs + [
                  pl.BlockSpec(w_bf.shape, lambda i: (0, 0), pipeline_mode=pl.Buffered(1)),
                  pl.BlockSpec((tm, d), lambda i: (i, 0)),
                  pl.BlockSpec((1, 1, N_MOD * d), mod_idx),
                  pl.BlockSpec((1, d), lambda i: (0, 0)),
                  pl.BlockSpec((d, 2 * e), lambda i: (0, 0))],
        out_specs=[pl.BlockSpec((tm, d), lambda i: (i, 0)),
                   pl.BlockSpec((tm * nw, LANES), lambda i: (i, 0)),
                   pl.BlockSpec((tm, e), lambda i: (i, 0))],
        out_shape=[jax.ShapeDtypeStruct((rows, d), F32),
                   jax.ShapeDtypeStruct((rows * nw, LANES), I32),
                   jax.ShapeDtypeStruct((rows, e), F32)],
        compiler_params=_params(("arbitrary",)),
    )(*mix_args, w_bf, xa, mod, gain, wr_split)


def _ones(mask, dtype):
    return jnp.where(mask, 1.0, 0.0).astype(dtype)


def _prefix_lanes(m):
    upper = _ones(lax.broadcasted_iota(I32, (LANES, LANES), 0) < lax.broadcasted_iota(I32, (LANES, LANES), 1), BF16)
    run = jnp.zeros((m.shape[0], 1), F32)
    outs = []
    for s in range(m.shape[1] // LANES):
        seg = m[:, s * LANES:(s + 1) * LANES]
        outs.append(_dot(seg, upper) + run)
        run = run + jnp.sum(seg.astype(F32), axis=1, keepdims=True)
    return jnp.concatenate(outs, axis=1)


def _topk_kernel(a_ref, idx_ref, gate_ref, dest_ref, start_ref, cnt_ref, posm_scr, dest_scr,
                 *, lg, cap, row_base, ent_base):
    g = pl.program_id(0)
    a = a_ref[...]
    n_e = a.shape[0]
    bits = pltpu.bitcast(a, I32)

    def bisect(_, lohi):
        lo, hi = lohi
        mid = lo + ((hi - lo) >> 1)
        ge = jnp.sum(_ones(bits >= mid, F32), axis=1, keepdims=True) >= cap
        return jnp.where(ge, mid, lo), jnp.where(ge, hi, mid)

    thr, _ = lax.fori_loop(0, 31, bisect,
                           (jnp.zeros((n_e, 1), I32), jnp.full((n_e, 1), 0x7F800000, I32)))
    gt = bits > thr
    eq = bits == thr
    need = cap - jnp.sum(_ones(gt, F32), axis=1, keepdims=True)
    sel = gt | (eq & (_prefix_lanes(_ones(eq, BF16)) < need))
    selb = _ones(sel, BF16)
    pos = _prefix_lanes(selb)
    start_t = jnp.sum(pos, axis=0, keepdims=True)
    lower = _ones(lax.broadcasted_iota(I32, (n_e, n_e), 0) > lax.broadcasted_iota(I32, (n_e, n_e), 1), BF16)
    base = (ent_base + g * (n_e * cap)).astype(F32)
    start_ref[...] = (start_t + base).astype(I32)
    cnt_ref[...] = jnp.sum(_ones(sel, F32), axis=0, keepdims=True).astype(I32)
    posm_scr[...] = jnp.where(sel, pos, -1.0)
    dest_scr[...] = start_t + base + _dot(lower, selb)

    tvals = (row_base + g * lg + lax.broadcasted_iota(I32, (1, lg), 1)).astype(F32)
    slot = lax.broadcasted_iota(I32, (cap, 1), 0).astype(F32)
    col = lax.broadcasted_iota(I32, (cap, n_e), 1)

    def compact(e, carry):
        ia, ga, da = carry
        m = posm_scr[pl.ds(e, 1), :] == slot
        i_e = jnp.sum(jnp.where(m, tvals, 0.0), axis=1, keepdims=True)
        g_e = jnp.sum(jnp.where(m, a_ref[pl.ds(e, 1), :], 0.0), axis=1, keepdims=True)
        d_e = jnp.sum(jnp.where(m, dest_scr[pl.ds(e, 1), :], 0.0), axis=1, keepdims=True)
        here = col == e
        return jnp.where(here, i_e, ia), jnp.where(here, g_e, ga), jnp.where(here, d_e, da)

    z = jnp.zeros((cap, n_e), F32)
    ia, ga, da = lax.fori_loop(0, n_e, compact, (z, z, z))
    idx_ref[0] = ia.astype(I32)
    gate_ref[0] = ga
    dest_ref[0] = da.astype(I32)


def expert_choice(aff_t, groups, lg, first_block, cap, row_base, ent_base):
    e = aff_t.shape[0]
    return pl.pallas_call(
        functools.partial(_topk_kernel, lg=lg, cap=cap, row_base=row_base, ent_base=ent_base),
        name="expert_choice",
        grid=(groups,),
        in_specs=[pl.BlockSpec((e, lg), lambda g: (0, first_block + g))],
        out_specs=[pl.BlockSpec((1, cap, e), lambda g: (g, 0, 0)),
                   pl.BlockSpec((1, cap, e), lambda g: (g, 0, 0)),
                   pl.BlockSpec((1, cap, e), lambda g: (g, 0, 0)),
                   pl.BlockSpec((1, lg), lambda g: (0, g)),
                   pl.BlockSpec((1, lg), lambda g: (0, g))],
        out_shape=[jax.ShapeDtypeStruct((groups, cap, e), I32),
                   jax.ShapeDtypeStruct((groups, cap, e), F32),
                   jax.ShapeDtypeStruct((groups, cap, e), I32),
                   jax.ShapeDtypeStruct((1, groups * lg), I32),
                   jax.ShapeDtypeStruct((1, groups * lg), I32)],
        scratch_shapes=[pltpu.VMEM((e, lg), F32), pltpu.VMEM((e, lg), F32)],
        compiler_params=_params(("arbitrary",)),
    )(aff_t)


def _ffn_kernel(idx_ref, dest_ref, h_ref, wg_ref, wu_ref, wd_ref, gate_ref, ys_ref,
                xs_buf, ys_buf, xb_scr, hid_scr, wd_scr, gsem, ssem, *, ts, nw, tf):
    e = pl.program_id(0)
    f = pl.program_id(1)
    n_e = pl.num_programs(0)
    last_f = pl.num_programs(1) - 1
    s = xb_scr.shape[0]

    def issue_gather(expert, slot):
        def body(i, carry):
            src = pl.multiple_of(idx_ref[expert * s + i] * nw, nw)
            pltpu.make_async_copy(h_ref.at[pl.ds(src, nw)],
                                  xs_buf.at[slot, pl.ds(pl.multiple_of(i * nw, nw), nw)], gsem.at[slot]).start()
            return carry
        lax.fori_loop(0, s, body, 0, unroll=8)

    def wait_gather(slot):
        pltpu.make_async_copy(h_ref.at[pl.ds(0, s * nw)], xs_buf.at[slot], gsem.at[slot]).wait()

    def issue_scatter(expert):
        def body(i, carry):
            dst = pl.multiple_of(dest_ref[expert * s + i] * nw, nw)
            pltpu.make_async_copy(ys_buf.at[pl.ds(pl.multiple_of(i * nw, nw), nw)],
                                  ys_ref.at[pl.ds(dst, nw)], ssem).start()
            return carry
        lax.fori_loop(0, s, body, 0, unroll=8)

    def wait_scatter():
        pltpu.make_async_copy(ys_buf, ys_ref.at[pl.ds(0, s * nw)], ssem).wait()

    @pl.when(f == 0)
    def _():
        slot = e % 2

        @pl.when(e == 0)
        def _():
            issue_gather(0, 0)

        wait_gather(slot)
        _load_packed_rows(xs_buf.at[slot], 0, s, nw, xb_scr, 0)

        @pl.when(e + 1 < n_e)
        def _():
            issue_gather(e + 1, 1 - slot)

    wg = wg_ref[0, 0].astype(BF16)
    wu = wu_ref[0, 0].astype(BF16)
    for r in range(s // ts):
        rows = slice(r * ts, (r + 1) * ts)
        xb = xb_scr[rows, :]
        hid_scr[f, rows, :] = (_silu(_dot(xb, wg)) * _dot(xb, wu)).astype(BF16)
    wd_scr[pl.ds(pl.multiple_of(f * tf, tf), tf), :] = wd_ref[0, 0].astype(BF16)

    @pl.when(f == last_f)
    def _():
        @pl.when(e > 0)
        def _():
            wait_scatter()

        for r in range(s // ts):
            rows = slice(r * ts, (r + 1) * ts)
            hid = jnp.concatenate([hid_scr[c, rows, :] for c in range(hid_scr.shape[0])], axis=1)
            y = _dot(hid, wd_scr[...])
            _store_packed_rows(ys_buf, r * ts, ts, y * gate_ref[0, rows, :])
        issue_scatter(e)

        @pl.when(e == n_e - 1)
        def _():
            wait_scatter()


def expert_ffn(h_packed, idx, dest, w_gate, w_up, w_down, layer, gates, d):
    nw = d // (2 * LANES)
    e, s = gates.shape[0], gates.shape[1]
    ff = w_gate.shape[3]
    tf = 256 if ff % 256 == 0 else ff
    ts = max(t for t in range(8, 577, 8) if s % t == 0)
    assert h_packed.shape[0] >= s * nw
    return pl.pallas_call(
        functools.partial(_ffn_kernel, ts=ts, nw=nw, tf=tf),
        name="expert_ffn",
        grid_spec=pltpu.PrefetchScalarGridSpec(
            num_scalar_prefetch=2, grid=(e, ff // tf),
            in_specs=[pl.BlockSpec(memory_space=pl.ANY),
                      pl.BlockSpec((1, 1, d, tf), lambda i, f, si, di: (layer, i, 0, f)),
                      pl.BlockSpec((1, 1, d, tf), lambda i, f, si, di: (layer, i, 0, f)),
                      pl.BlockSpec((1, 1, tf, d), lambda i, f, si, di: (layer, i, f, 0)),
                      pl.BlockSpec((1, s, 1), lambda i, f, si, di: (i, 0, 0))],
            out_specs=pl.BlockSpec(memory_space=pl.ANY),
            scratch_shapes=[pltpu.VMEM((2, s * nw, LANES), I32), pltpu.VMEM((s * nw, LANES), I32),
                            pltpu.VMEM((s, d), BF16), pltpu.VMEM((ff // tf, s, tf), BF16),
                            pltpu.VMEM((ff, d), BF16),
                            pltpu.SemaphoreType.DMA((2,)), pltpu.SemaphoreType.DMA(())]),
        out_shape=jax.ShapeDtypeStruct((e * s * nw, LANES), I32),
        compiler_params=_params(("arbitrary", "arbitrary")),
    )(idx, dest, h_packed, w_gate, w_up, w_down, gates)


COMB_ROWS = 256
COMB_ENT = 256


def _combine_kernel(kb_ref, ch_ref, flag_ref, ys_ref, xm_ref, start_ref, cnt_ref, mod_ref, *rest, d, final):
    if final:
        gf_ref, o_ref, chunk_scr = rest
    else:
        o_ref, chunk_scr = rest
    p = pl.program_id(0)
    flags = flag_ref[p]

    @pl.when((flags & 2) != 0)
    def _():
        o_ref[...] = jnp.zeros_like(o_ref)

    @pl.when((flags & 8) != 0)
    def _():
        _load_packed_rows(ys_ref, 0, COMB_ENT, d // (2 * LANES), chunk_scr, 0)

    @pl.when((flags & 1) != 0)
    def _():
        gpos = ch_ref[p] * COMB_ENT + lax.broadcasted_iota(I32, (COMB_ROWS, COMB_ENT), 1)
        st = start_ref[...]
        onehot = _ones((gpos >= st) & (gpos < st + cnt_ref[...]), BF16)
        o_ref[...] += _dot(onehot, chunk_scr[...])

    @pl.when((flags & 4) != 0)
    def _():
        m = mod_ref[0]
        out = xm_ref[...] + m[:, 5 * d:6 * d] * o_ref[...]
        if final:
            out = out * lax.rsqrt(jnp.mean(out * out, axis=-1, keepdims=True) + EPS) * gf_ref[...]
        o_ref[...] = out


def combine(ys_sorted, xm, start_col, cnt_col, mod, kb, ch, flags, rows, tiles_per_sample, n_x_tiles,
            n_samples, final_gain):
    d = xm.shape[1]
    nw = d // (2 * LANES)
    n_pairs = kb.shape[0]
    mod_idx = lambda p, kb, ch, fl: (jnp.where(kb[p] < n_x_tiles, kb[p] // tiles_per_sample, n_samples), 0, 0)
    in_specs = [pl.BlockSpec((COMB_ENT * nw, LANES), lambda p, kb, ch, fl: (ch[p], 0)),
                pl.BlockSpec((COMB_ROWS, d), lambda p, kb, ch, fl: (kb[p], 0)),
                pl.BlockSpec((COMB_ROWS, 1), lambda p, kb, ch, fl: (kb[p], 0)),
                pl.BlockSpec((COMB_ROWS, 1), lambda p, kb, ch, fl: (kb[p], 0)),
                pl.BlockSpec((1, 1, N_MOD * d), mod_idx)]
    args = [ys_sorted, xm, start_col, cnt_col, mod]
    if final_gain is not None:
        in_specs.append(pl.BlockSpec((1, d), lambda p, kb, ch, fl: (0, 0)))
        args.append(final_gain)
    return pl.pallas_call(
        functools.partial(_combine_kernel, d=d, final=final_gain is not None),
        name="combine",
        grid_spec=pltpu.PrefetchScalarGridSpec(
            num_scalar_prefetch=3, grid=(n_pairs,),
            in_specs=in_specs,
            out_specs=pl.BlockSpec((COMB_ROWS, d), lambda p, kb, ch, fl: (kb[p], 0)),
            scratch_shapes=[pltpu.VMEM((COMB_ENT, d), BF16)]),
        out_shape=jax.ShapeDtypeStruct((rows, d), F32),
        compiler_params=_params(("arbitrary",)),
    )(kb, ch, flags, *args)


def _combine_schedule(start_row, n_blocks, n_chunks):
    blk_start = start_row[::COMB_ROWS]
    blk_end = jnp.concatenate([blk_start[1:], jnp.full((1,), n_chunks * COMB_ENT, I32)])
    lo = jnp.minimum(blk_start // COMB_ENT, n_chunks - 1)
    hi = jnp.maximum(lo, (blk_end - 1) // COMB_ENT)
    cnt = hi - lo + 1
    off = jnp.cumsum(cnt) - cnt
    total = off[-1] + cnt[-1]
    n_pairs = n_blocks + n_chunks
    pidx = jnp.arange(n_pairs, dtype=I32)
    kb = jnp.sum((off[None, :] <= pidx[:, None]).astype(I32), axis=1) - 1
    valid = pidx < total
    ch = jnp.where(valid, lo[kb] + pidx - off[kb], hi[n_blocks - 1]).astype(I32)
    first = valid & (pidx == off[kb])
    last = valid & (pidx == off[kb] + cnt[kb] - 1)
    fresh = jnp.concatenate([jnp.ones((1,), bool), ch[1:] != ch[:-1]])
    flags = valid.astype(I32) + 2 * first.astype(I32) + 4 * last.astype(I32) + 8 * fresh.astype(I32)
    return kb, ch, flags


def _rope_tables(lx, n_ctx_rows, n_samples):
    n_freq = QK_DIM // 4
    freqs = ROPE_BASE ** (-jnp.arange(n_freq, dtype=F32) / n_freq)
    rows = lx // GRID_W
    row = jnp.repeat(jnp.arange(rows, dtype=F32), GRID_W)
    col = jnp.tile(jnp.arange(GRID_W, dtype=F32), rows)
    ang = jnp.concatenate([row[:, None] * freqs, col[:, None] * freqs], axis=-1)
    cos, sin = jnp.cos(ang), jnp.sin(ang)
    ct = jnp.tile(jnp.concatenate([cos, cos], axis=-1), (n_samples, LANES // QK_DIM))
    st = jnp.tile(jnp.concatenate([-sin, sin], axis=-1), (n_samples, LANES // QK_DIM))
    ct = jnp.concatenate([ct, jnp.ones((n_ctx_rows, LANES), F32)], axis=0)
    st = jnp.concatenate([st, jnp.zeros((n_ctx_rows, LANES), F32)], axis=0)
    return ct, st


def kernel(x, c, ctx, c_ctx, w_ada, b_ada, norm_mix, norm_ffn, w_in, w_out, ret_log_decay, ret_norm,
           diff_lambda, diff_norm, w_router, w_gate, w_up, w_down, norm_final):
    n_b, lx, d = x.shape
    lc = ctx.shape[1]
    depth = w_ada.shape[0]
    heads = ret_log_decay.shape[-1]
    dheads = diff_norm.shape[-1] // V_DIM
    n_e = w_router.shape[-1]
    rx, rc = n_b * lx, n_b * lc
    cap_x = CAPACITY_FACTOR * lx // n_e
    cap_c = CAPACITY_FACTOR * lc // n_e
    tm = 512 if (lx % 512 == 0 and rx % 512 == 0 and rc % 512 == 0) else 256
    tmo = tm
    assert lx % lc == 0 and rc % tmo == 0 and heads % 2 == 0 and n_b < MOD_ROWS

    qk_scale = QK_DIM ** -0.5
    n_rq = heads * QK_DIM // LANES
    rope = tuple([True] * (2 * n_rq) + [False] * (2 * heads) + [True] * (2 * dheads) + [False] * dheads)
    slab_scale = tuple([qk_scale] * n_rq + [1.0] * (n_rq + 2 * heads) + [qk_scale * math.log2(math.e)] * dheads
                       + [1.0] * (2 * dheads))
    w_in_bf = _prep_w_in(w_in, slab_scale)

    xa = jnp.concatenate([x.reshape(rx, d), ctx.reshape(rc, d)], axis=0)
    cc = jnp.concatenate([c, c_ctx[None], jnp.zeros((MOD_ROWS - n_b - 1, d), F32)], axis=0)
    mods = ada_modulation(cc, w_ada, b_ada)
    ct, st = _rope_tables(lx, rc, n_b)
    log_g = jnp.log1p(-jnp.exp(ret_log_decay.astype(F32)))
    out = None

    for layer in range(depth):
        last = layer == depth - 1
        lam_init = 0.8 - 0.6 * math.exp(-0.3 * layer)
        mod = mods[layer].reshape(MOD_ROWS, 1, N_MOD * d)
        p = in_projection(xa, mod, norm_mix[layer][None], w_in_bf, layer, ct, st,
                          rx // tm, lx // tm, n_b, rope, tm)

        ret = retention(p, log_g[layer], ret_norm[layer][None], n_b, lx, lc, heads, not last)
        dif_x = diff_attention(p, diff_lambda[layer], diff_norm[layer][None], n_b, lx, lc, heads, dheads,
                               lam_init, True, 1024 if lx % 1024 == 0 else 256)
        if last:
            rows = rx
            mix_c = None
        else:
            rows = rx + rc
            dif_c = diff_attention(p, diff_lambda[layer], diff_norm[layer][None], n_b, lx, lc, heads, dheads,
                                   lam_init, False, lc)
            mix_c = (ret[1], dif_c)

        xm, h2, aff = out_projection((ret[0], dif_x), mix_c, w_out[layer].astype(BF16), xa, mod,
                                     norm_ffn[layer][None], w_router[layer], rows, rx // tmo, lx // tmo, n_b, tmo)
        aff_t = aff.T

        sel = [expert_choice(aff_t, n_b, lx, 0, cap_x, 0, 0)]
        if not last:
            sel.append(expert_choice(aff_t, n_b, lc, rx // lc, cap_c, rx, n_b * n_e * cap_x))
        to_es = lambda t: jnp.transpose(t, (2, 0, 1)).reshape(n_e, -1)
        idx = jnp.concatenate([to_es(s[0]) for s in sel], axis=1)
        gates = jnp.concatenate([to_es(s[1]) for s in sel], axis=1)
        dest = jnp.concatenate([to_es(s[2]) for s in sel], axis=1)
        start_row = jnp.concatenate([s[3][0] for s in sel])
        cnt_row = jnp.concatenate([s[4][0] for s in sel])
        slots = idx.shape[1]
        n_ent = n_e * slots

        ys_sorted = expert_ffn(h2, idx.reshape(-1), dest.reshape(-1), w_gate, w_up, w_down, layer,
                               gates.reshape(n_e, slots, 1), d)

        kb, ch, flags = _combine_schedule(start_row, rows // COMB_ROWS, n_ent // COMB_ENT)
        xa_new = combine(ys_sorted, xm, start_row[:, None], cnt_row[:, None], mod, kb, ch, flags, rows,
                         lx // COMB_ROWS, rx // COMB_ROWS, n_b, norm_final[None] if last else None)
        if last:
            out = xa_new
        else:
            xa = xa_new
    return out.reshape(n_b, lx, d)
```

```python
import functools
import math

import jax
import jax.numpy as jnp
from jax import lax
from jax.experimental import pallas as pl
from jax.experimental.pallas import tpu as pltpu

F32 = jnp.float32
BF16 = jnp.bfloat16
I32 = jnp.int32

EPS = 1e-6
N_MOD = 6
GRID_W = 64
ROPE_BASE = 10000.0
QK_DIM = 64
V_DIM = 128
LANES = 128
CHUNK = 128
CAPACITY_FACTOR = 2
MOD_ROWS = 8
VMEM_LIMIT = 56 * 1024 * 1024


def _params(sem, vmem=VMEM_LIMIT):
    return pltpu.CompilerParams(dimension_semantics=sem, vmem_limit_bytes=vmem)


def _silu(x):
    return x / (1.0 + jnp.exp(-x))


def _dot(a, b):
    return jnp.dot(a, b, preferred_element_type=F32)


def _dot_nt(a, b):
    return lax.dot_general(a, b, (((1,), (1,)), ((), ())), preferred_element_type=F32)


def _ada_kernel(c_ref, w_ref, b_ref, o_ref):
    s = _silu(c_ref[...]).astype(BF16)
    o_ref[0] = _dot(s, w_ref[0].astype(BF16)) + b_ref[0]


def ada_modulation(cc, w_ada, b_ada):
    depth, d, n = w_ada.shape
    tn = 1024 if n % 1024 == 0 else n
    return pl.pallas_call(
        _ada_kernel,
        name="ada_mod",
        grid=(depth, n // tn),
        in_specs=[pl.BlockSpec((MOD_ROWS, d), lambda l, j: (0, 0)),
                  pl.BlockSpec((1, d, tn), lambda l, j: (l, 0, j)),
                  pl.BlockSpec((1, 1, tn), lambda l, j: (l, 0, j))],
        out_specs=pl.BlockSpec((1, MOD_ROWS, tn), lambda l, j: (l, 0, j)),
        out_shape=jax.ShapeDtypeStruct((depth, MOD_ROWS, n), F32),
        compiler_params=_params(("arbitrary", "arbitrary")),
    )(cc, w_ada, b_ada.reshape(depth, 1, n))


def _rmsnorm_mod(x, gain, shift, scale):
    y = x * lax.rsqrt(jnp.mean(x * x, axis=-1, keepdims=True) + EPS) * gain
    return y * (1.0 + scale) + shift


MXU_N = 256


def _group_of(lane):
    return (lane >> 6) & 1


def _swap_halves(a):
    lane = lax.broadcasted_iota(I32, a.shape, 1)
    return jnp.where((lane & 32) == 0, pltpu.roll(a, 96, 1), pltpu.roll(a, 32, 1))


def _inproj_kernel(x_ref, mod_ref, g_ref, w_ref, ct_ref, st_ref, o_ref, *, d, rope):
    m = mod_ref[0]
    h = _rmsnorm_mod(x_ref[...], g_ref[...], m[:, 0:d], m[:, d:2 * d]).astype(BF16)
    per = MXU_N // LANES
    for s in range(w_ref.shape[2] // MXU_N):
        acc = _dot(h, w_ref[0, :, s * MXU_N:(s + 1) * MXU_N])
        for t in range(per):
            a = acc[:, t * LANES:(t + 1) * LANES]
            if rope[s * per + t]:
                a = a * ct_ref[...] + _swap_halves(a) * st_ref[...]
            c0 = s * MXU_N + t * LANES
            o_ref[:, c0:c0 + LANES] = a.astype(BF16)


def in_projection(xa, mod, gain, w_bf, layer, ct, st, n_x_tiles, tiles_per_sample, n_samples, rope, tm):
    r, d = xa.shape
    n = w_bf.shape[2]
    mod_idx = lambda i: (jnp.where(i < n_x_tiles, i // tiles_per_sample, n_samples), 0, 0)
    return pl.pallas_call(
        functools.partial(_inproj_kernel, d=d, rope=rope),
        name="in_proj",
        grid=(r // tm,),
        in_specs=[pl.BlockSpec((tm, d), lambda i: (i, 0)),
                  pl.BlockSpec((1, 1, N_MOD * d), mod_idx),
                  pl.BlockSpec((1, d), lambda i: (0, 0)),
                  pl.BlockSpec((1, d, n), lambda i: (layer, 0, 0), pipeline_mode=pl.Buffered(1)),
                  pl.BlockSpec((tm, LANES), lambda i: (i, 0)),
                  pl.BlockSpec((tm, LANES), lambda i: (i, 0))],
        out_specs=pl.BlockSpec((tm, n), lambda i: (i, 0)),
        out_shape=jax.ShapeDtypeStruct((r, n), BF16),
        compiler_params=_params(("arbitrary",)),
    )(xa, mod, gain, w_bf, ct, st)


def _prep_w_in(w_in, slab_scale):
    col_scale = jnp.repeat(jnp.asarray(slab_scale, F32), LANES)
    return (w_in * col_scale).astype(BF16)


def _ret_kernel(lg_ref, qx_ref, kx_ref, vx_ref, gx_ref, qc_ref, kc_ref, vc_ref, gc_ref, gain_ref,
                *rest, lx, lc, ctx_out):
    if ctx_out:
        ox_ref, oc_ref, accx, accc = rest
    else:
        ox_ref, accx = rest
    pair = pl.program_id(1)
    w2 = 2 * V_DIM
    lane = lax.broadcasted_iota(I32, (CHUNK, LANES), 1)
    dif = (lax.broadcasted_iota(I32, (CHUNK, CHUNK), 0) - lax.broadcasted_iota(I32, (CHUNK, CHUNK), 1)).astype(F32)
    pos = lax.broadcasted_iota(I32, (CHUNK, 1), 0).astype(F32)
    cpos = lax.broadcasted_iota(I32, (lc, 1), 0).astype(F32)
    c_len = float(CHUNK)
    lgf = [lg_ref[0, 2 * pair + hh] for hh in range(2)]
    lgb = [lg_ref[1, 2 * pair + hh] for hh in range(2)]

    def per_head(fn):
        vals = [fn(hh) for hh in range(2)]
        return jnp.concatenate([jnp.broadcast_to(v, (v.shape[0], V_DIM)) for v in vals], axis=1)

    decay = per_head(lambda hh: jnp.where(dif >= 0, jnp.exp(lgf[hh] * jnp.maximum(dif, 0.0)),
                                          jnp.exp(lgb[hh] * jnp.maximum(-dif, 0.0))))
    qd_f = per_head(lambda hh: jnp.exp(lgf[hh] * (pos + 1.0)))
    kd_f = per_head(lambda hh: jnp.exp(lgf[hh] * (c_len - 1.0 - pos)))
    qd_b = per_head(lambda hh: jnp.exp(lgb[hh] * (c_len - pos)))
    kd_b = per_head(lambda hh: jnp.exp(lgb[hh] * pos))
    one = jnp.ones((1, 1), F32)
    cd_f = per_head(lambda hh: jnp.exp(lgf[hh] * c_len) * one)
    cd_b = per_head(lambda hh: jnp.exp(lgb[hh] * c_len) * one)
    srow = lax.broadcasted_iota(I32, (LANES, w2), 0)
    scol = lax.broadcasted_iota(I32, (LANES, w2), 1)
    diag = (_group_of(srow) == 0) == (scol < V_DIM)
    zeros_v = jnp.zeros((CHUNK, V_DIM), BF16)

    def kv_state(k, v_scaled):
        return jnp.where(diag, _dot(k.astype(F32).T.astype(BF16), v_scaled.astype(BF16)), 0.0)

    kc = kc_ref[...]
    vc = vc_ref[...].astype(F32)
    s_f = kv_state(kc, vc * per_head(lambda hh: jnp.exp(lgf[hh] * (lc - 1.0 - cpos))))
    s_b = kv_state(kc, vc * per_head(lambda hh: jnp.exp(lgb[hh] * cpos)))

    def sweeps(q_ref, k_ref, v_ref, g_ref, o_ref, acc_ref, n, s_f0, s_b0):
        def rows_of(c):
            return pl.ds(pl.multiple_of(c * CHUNK, CHUNK), CHUNK)

        def forward(c, s):
            rows = rows_of(c)
            q, k, v = q_ref[rows, :], k_ref[rows, :], v_ref[rows, :]
            k_bd = jnp.concatenate([jnp.where(_group_of(lane) == 0, k, 0), jnp.where(_group_of(lane) == 1, k, 0)],
                                   axis=0)
            v_bd = jnp.concatenate([jnp.concatenate([v[:, :V_DIM], zeros_v], axis=1),
                                    jnp.concatenate([zeros_v, v[:, V_DIM:]], axis=1)], axis=0)
            sc = _dot_nt(q, k_bd) * decay
            acc_ref[rows, :] = _dot(sc.astype(BF16), v_bd) + _dot(q, s.astype(BF16)) * qd_f
            return s * cd_f + kv_state(k, v.astype(F32) * kd_f)

        def backward(t, s):
            rows = rows_of(n - 1 - t)
            q, k, v = q_ref[rows, :], k_ref[rows, :], v_ref[rows, :]
            y = acc_ref[rows, :] + _dot(q, s.astype(BF16)) * qd_b
            for hh in range(2):
                vsl = slice(hh * V_DIM, (hh + 1) * V_DIM)
                yh = y[:, vsl]
                yc = yh - jnp.mean(yh, axis=-1, keepdims=True)
                out = yc * lax.rsqrt(jnp.mean(yc * yc, axis=-1, keepdims=True) + EPS) * gain_ref[:, vsl]
                o_ref[rows, vsl] = (_silu(g_ref[rows, vsl].astype(F32)) * out).astype(BF16)
            return s * cd_b + kv_state(k, v.astype(F32) * kd_b)

        unroll = 4 if n % 4 == 0 else n
        lax.fori_loop(0, n, forward, s_f0, unroll=unroll)
        lax.fori_loop(0, n, backward, s_b0, unroll=unroll)

    sweeps(qx_ref, kx_ref, vx_ref, gx_ref, ox_ref, accx, lx // CHUNK, s_f, s_b)
    if ctx_out:
        zero = jnp.zeros((LANES, w2), F32)
        sweeps(qc_ref, kc_ref, vc_ref, gc_ref, oc_ref, accc, lc // CHUNK, zero, zero)


def retention(p, lg, gain, n_samples, lx, lc, heads, ctx_out):
    pairs = heads // 2
    half = heads // 2
    cb = n_samples * lx // lc
    w2 = 2 * V_DIM
    in_specs = [
        pl.BlockSpec(memory_space=pltpu.SMEM),
        pl.BlockSpec((lx, LANES), lambda b, h: (b, h)),
        pl.BlockSpec((lx, LANES), lambda b, h: (b, half + h)),
        pl.BlockSpec((lx, w2), lambda b, h: (b, half + h)),
        pl.BlockSpec((lx, w2), lambda b, h: (b, heads + h)),
        pl.BlockSpec((lc, LANES), lambda b, h: (cb + b, h)),
        pl.BlockSpec((lc, LANES), lambda b, h: (cb + b, half + h)),
        pl.BlockSpec((lc, w2), lambda b, h: (cb + b, half + h)),
        pl.BlockSpec((lc, w2), lambda b, h: (cb + b, heads + h)),
        pl.BlockSpec((1, w2), lambda b, h: (0, h)),
    ]
    out_specs = [pl.BlockSpec((lx, w2), lambda b, h: (b, h))]
    out_shape = [jax.ShapeDtypeStruct((n_samples * lx, heads * V_DIM), BF16)]
    scratch = [pltpu.VMEM((lx, w2), F32)]
    if ctx_out:
        out_specs.append(pl.BlockSpec((lc, w2), lambda b, h: (b, h)))
        out_shape.append(jax.ShapeDtypeStruct((n_samples * lc, heads * V_DIM), BF16))
        scratch.append(pltpu.VMEM((lc, w2), F32))
    return pl.pallas_call(
        functools.partial(_ret_kernel, lx=lx, lc=lc, ctx_out=ctx_out),
        name="retention",
        grid=(n_samples, pairs),
        in_specs=in_specs, out_specs=out_specs, out_shape=out_shape, scratch_shapes=scratch,
        compiler_params=_params(("arbitrary", "arbitrary")),
    )(lg, p, p, p, p, p, p, p, p, gain)


def _diff_kernel(lam_ref, q_ref, kc_ref, vc_ref, *rest, lam_init, has_x, kb):
    if has_x:
        kx_ref, vx_ref, gain_ref, o_ref = rest
    else:
        gain_ref, o_ref = rest
    lv = lam_ref[...]
    lam = (jnp.exp(jnp.sum(lv[0:1] * lv[1:2], axis=-1, keepdims=True))
           - jnp.exp(jnp.sum(lv[2:3] * lv[3:4], axis=-1, keepdims=True)) + lam_init)
    q = q_ref[...]
    tq = q.shape[0]
    lane = lax.broadcasted_iota(I32, q.shape, 1)
    blocks = [(kc_ref, vc_ref, 0, kc_ref.shape[0])]
    if has_x:
        blocks += [(kx_ref, vx_ref, r0, kb) for r0 in range(0, kx_ref.shape[0], kb)]
    v_ext = [jnp.concatenate([v_ref[r0:r0 + n, :], jnp.ones((n, V_DIM), BF16)], axis=1) for _, v_ref, r0, n in blocks]
    heads_out = []
    for t in range(2):
        qm = jnp.where(_group_of(lane) == t, q, 0)
        m = jnp.full((tq, 1), -jnp.inf, F32)
        acc = jnp.zeros((tq, 2 * V_DIM), F32)
        for (k_ref, _, r0, n), ve in zip(blocks, v_ext):
            s = _dot_nt(qm, k_ref[r0:r0 + n, :])
            m_new = jnp.maximum(m, jnp.max(s, axis=-1, keepdims=True))
            acc = acc * jnp.exp2(m - m_new) + _dot(jnp.exp2(s - m_new).astype(BF16), ve)
            m = m_new
        heads_out.append(acc[:, :V_DIM] * (1.0 / acc[:, V_DIM:V_DIM + 1]))
    out = heads_out[0] - lam * heads_out[1]
    y = out * lax.rsqrt(jnp.mean(out * out, axis=-1, keepdims=True) + EPS) * gain_ref[...]
    o_ref[...] = (y * (1.0 - lam_init)).astype(BF16)


def diff_attention(p, lam_vec, gain, n_samples, lx, lc, ret_heads, heads, lam_init, latent, tq):
    qo = 3 * ret_heads
    ko = qo + heads
    vo = ko + heads
    cb = n_samples * lx // lc
    lq = lx if latent else lc
    nq = lq // tq
    q_row = (lambda b, t: b * nq + t) if latent else (lambda b, t: n_samples * lx // tq + b * nq + t)
    in_specs = [
        pl.BlockSpec((4, QK_DIM), lambda b, h, t: (0, 0)),
        pl.BlockSpec((tq, LANES), lambda b, h, t: (q_row(b, t), qo + h)),
        pl.BlockSpec((lc, LANES), lambda b, h, t: (cb + b, ko + h)),
        pl.BlockSpec((lc, LANES), lambda b, h, t: (cb + b, vo + h)),
    ]
    args = [lam_vec, p, p, p]
    if latent:
        in_specs += [pl.BlockSpec((lx, LANES), lambda b, h, t: (b, ko + h)),
                     pl.BlockSpec((lx, LANES), lambda b, h, t: (b, vo + h))]
        args += [p, p]
    in_specs.append(pl.BlockSpec((1, LANES), lambda b, h, t: (0, h)))
    args.append(gain)
    return pl.pallas_call(
        functools.partial(_diff_kernel, lam_init=lam_init, has_x=latent, kb=min(1024, lx)),
        name="diff_attn_x" if latent else "diff_attn_ctx",
        grid=(n_samples, heads, nq),
        in_specs=in_specs,
        out_specs=pl.BlockSpec((tq, LANES), lambda b, h, t: (b * nq + t, h)),
        out_shape=jax.ShapeDtypeStruct((n_samples * lq, heads * V_DIM), BF16),
        compiler_params=_params(("arbitrary", "arbitrary", "arbitrary")),
    )(*args)


def _pack_pair(a, b):
    ua = pltpu.bitcast(a.astype(BF16).astype(F32), I32)
    ub = pltpu.bitcast(b.astype(BF16).astype(F32), I32)
    return ua | lax.shift_right_logical(ub, 16)


def _unpack_pair(w):
    return (pltpu.bitcast(w & -65536, F32).astype(BF16), pltpu.bitcast(w << 16, F32).astype(BF16))


def _store_packed_rows(ref, row0, n_rows, val):
    nw = val.shape[1] // (2 * LANES)
    for j in range(nw):
        w = _pack_pair(val[:, j * LANES:(j + 1) * LANES], val[:, (nw + j) * LANES:(nw + j + 1) * LANES])
        ref[pl.ds(row0 * nw + j, n_rows, stride=nw), :] = w


def _load_packed_rows(ref, row0, n_rows, nw, dst, dst_row0):
    for j in range(nw):
        hi, lo = _unpack_pair(ref[pl.ds(row0 * nw + j, n_rows, stride=nw), :])
        dst[dst_row0:dst_row0 + n_rows, j * LANES:(j + 1) * LANES] = hi
        dst[dst_row0:dst_row0 + n_rows, (nw + j) * LANES:(nw + j + 1) * LANES] = lo


def _outproj_kernel(*refs, d, wr, n_x_tiles, with_ctx):
    if with_ctx:
        retx_ref, difx_ref, retc_ref, difc_ref = refs[:4]
        is_ctx = pl.program_id(0) >= n_x_tiles
        ret = jnp.where(is_ctx, retc_ref[...], retx_ref[...])
        dif = jnp.where(is_ctx, difc_ref[...], difx_ref[...])
        refs = refs[4:]
    else:
        ret, dif = refs[0][...], refs[1][...]
        refs = refs[2:]
    w_ref, x_ref, mod_ref, g_ref, wr_ref, xm_ref, h_ref, aff_ref = refs
    y = _dot(ret, w_ref[0:wr, :]) + _dot(dif, w_ref[wr:, :])
    m = mod_ref[0]
    xm = x_ref[...] + m[:, 2 * d:3 * d] * y
    xm_ref[...] = xm
    h = _rmsnorm_mod(xm, g_ref[...], m[:, 3 * d:4 * d], m[:, 4 * d:5 * d])
    _store_packed_rows(h_ref, 0, h.shape[0], h)
    n_e = wr_ref.shape[1] // 2
    h_hi = h.astype(BF16)
    h_lo = (h - h_hi.astype(F32)).astype(BF16)
    both = _dot(h_hi, wr_ref[...])
    logits = both[:, :n_e] + both[:, n_e:] + _dot(h_lo, wr_ref[...])[:, :n_e]
    e = jnp.exp(logits - jnp.max(logits, axis=-1, keepdims=True))
    aff_ref[...] = e / jnp.sum(e, axis=-1, keepdims=True)


def out_projection(mix_x, mix_c, w_bf, xa, mod, gain, w_router, rows, n_x_tiles, tiles_per_sample, n_samples, tm):
    d = xa.shape[1]
    wr = mix_x[0].shape[1]
    wd = mix_x[1].shape[1]
    e = w_router.shape[1]
    nw = d // (2 * LANES)
    wr_hi = w_router.astype(BF16)
    wr_split = jnp.concatenate([wr_hi, (w_router - wr_hi.astype(F32)).astype(BF16)], axis=1)
    mod_idx = lambda i: (jnp.where(i < n_x_tiles, i // tiles_per_sample, n_samples), 0, 0)
    x_idx = lambda i: (jnp.minimum(i, n_x_tiles - 1), 0)
    c_idx = lambda i: (jnp.maximum(i - n_x_tiles, 0), 0)
    mix_specs = [pl.BlockSpec((tm, wr), x_idx), pl.BlockSpec((tm, wd), x_idx)]
    mix_args = list(mix_x)
    if mix_c is not None:
        mix_specs += [pl.BlockSpec((tm, wr), c_idx), pl.BlockSpec((tm, wd), c_idx)]
        mix_args += list(mix_c)
    return pl.pallas_call(
        functools.partial(_outproj_kernel, d=d, wr=wr, n_x_tiles=n_x_tiles, with_ctx=mix_c is not None),
        name="out_proj_router",
        grid=(rows // tm,),
        in_specs=mix_specs + [
                  pl.BlockSpec(w_bf.shape, lambda i: (0, 0), pipeline_mode=pl.Buffered(1)),
                  pl.BlockSpec((tm, d), lambda i: (i, 0)),
                  pl.BlockSpec((1, 1, N_MOD * d), mod_idx),
                  pl.BlockSpec((1, d), lambda i: (0, 0)),
                  pl.BlockSpec((d, 2 * e), lambda i: (0, 0))],
        out_specs=[pl.BlockSpec((tm, d), lambda i: (i, 0)),
                   pl.BlockSpec((tm * nw, LANES), lambda i: (i, 0)),
                   pl.BlockSpec((tm, e), lambda i: (i, 0))],
        out_shape=[jax.ShapeDtypeStruct((rows, d), F32),
                   jax.ShapeDtypeStruct((rows * nw, LANES), I32),
                   jax.ShapeDtypeStruct((rows, e), F32)],
        compiler_params=_params(("arbitrary",)),
    )(*mix_args, w_bf, xa, mod, gain, wr_split)


def _ones(mask, dtype):
    return jnp.where(mask, 1.0, 0.0).astype(dtype)


def _prefix_lanes(m):
    upper = _ones(lax.broadcasted_iota(I32, (LANES, LANES), 0) < lax.broadcasted_iota(I32, (LANES, LANES), 1), BF16)
    run = jnp.zeros((m.shape[0], 1), F32)
    outs = []
    for s in range(m.shape[1] // LANES):
        seg = m[:, s * LANES:(s + 1) * LANES]
        outs.append(_dot(seg, upper) + run)
        run = run + jnp.sum(seg.astype(F32), axis=1, keepdims=True)
    return jnp.concatenate(outs, axis=1)


def _topk_kernel(a_ref, idx_ref, gate_ref, dest_ref, start_ref, cnt_ref, posm_scr, dest_scr,
                 *, lg, cap, row_base, ent_base):
    g = pl.program_id(0)
    a = a_ref[...]
    n_e = a.shape[0]
    bits = pltpu.bitcast(a, I32)

    def bisect(_, lohi):
        lo, hi = lohi
        mid = lo + ((hi - lo) >> 1)
        ge = jnp.sum(_ones(bits >= mid, F32), axis=1, keepdims=True) >= cap
        return jnp.where(ge, mid, lo), jnp.where(ge, hi, mid)

    thr, _ = lax.fori_loop(0, 31, bisect,
                           (jnp.zeros((n_e, 1), I32), jnp.full((n_e, 1), 0x7F800000, I32)))
    gt = bits > thr
    eq = bits == thr
    need = cap - jnp.sum(_ones(gt, F32), axis=1, keepdims=True)
    sel = gt | (eq & (_prefix_lanes(_ones(eq, BF16)) < need))
    selb = _ones(sel, BF16)
    pos = _prefix_lanes(selb)
    start_t = jnp.sum(pos, axis=0, keepdims=True)
    lower = _ones(lax.broadcasted_iota(I32, (n_e, n_e), 0) > lax.broadcasted_iota(I32, (n_e, n_e), 1), BF16)
    base = (ent_base + g * (n_e * cap)).astype(F32)
    start_ref[...] = (start_t + base).astype(I32)
    cnt_ref[...] = jnp.sum(_ones(sel, F32), axis=0, keepdims=True).astype(I32)
    posm_scr[...] = jnp.where(sel, pos, -1.0)
    dest_scr[...] = start_t + base + _dot(lower, selb)

    tvals = (row_base + g * lg + lax.broadcasted_iota(I32, (1, lg), 1)).astype(F32)
    slot = lax.broadcasted_iota(I32, (cap, 1), 0).astype(F32)
    col = lax.broadcasted_iota(I32, (cap, n_e), 1)

    def compact(e, carry):
        ia, ga, da = carry
        m = posm_scr[pl.ds(e, 1), :] == slot
        i_e = jnp.sum(jnp.where(m, tvals, 0.0), axis=1, keepdims=True)
        g_e = jnp.sum(jnp.where(m, a_ref[pl.ds(e, 1), :], 0.0), axis=1, keepdims=True)
        d_e = jnp.sum(jnp.where(m, dest_scr[pl.ds(e, 1), :], 0.0), axis=1, keepdims=True)
        here = col == e
        return jnp.where(here, i_e, ia), jnp.where(here, g_e, ga), jnp.where(here, d_e, da)

    z = jnp.zeros((cap, n_e), F32)
    ia, ga, da = lax.fori_loop(0, n_e, compact, (z, z, z))
    idx_ref[0] = ia.astype(I32)
    gate_ref[0] = ga
    dest_ref[0] = da.astype(I32)


def expert_choice(aff_t, groups, lg, first_block, cap, row_base, ent_base):
    e = aff_t.shape[0]
    return pl.pallas_call(
        functools.partial(_topk_kernel, lg=lg, cap=cap, row_base=row_base, ent_base=ent_base),
        name="expert_choice",
        grid=(groups,),
        in_specs=[pl.BlockSpec((e, lg), lambda g: (0, first_block + g))],
        out_specs=[pl.BlockSpec((1, cap, e), lambda g: (g, 0, 0)),
                   pl.BlockSpec((1, cap, e), lambda g: (g, 0, 0)),
                   pl.BlockSpec((1, cap, e), lambda g: (g, 0, 0)),
                   pl.BlockSpec((1, lg), lambda g: (0, g)),
                   pl.BlockSpec((1, lg), lambda g: (0, g))],
        out_shape=[jax.ShapeDtypeStruct((groups, cap, e), I32),
                   jax.ShapeDtypeStruct((groups, cap, e), F32),
                   jax.ShapeDtypeStruct((groups, cap, e), I32),
                   jax.ShapeDtypeStruct((1, groups * lg), I32),
                   jax.ShapeDtypeStruct((1, groups * lg), I32)],
        scratch_shapes=[pltpu.VMEM((e, lg), F32), pltpu.VMEM((e, lg), F32)],
        compiler_params=_params(("arbitrary",)),
    )(aff_t)


def _ffn_kernel(idx_ref, dest_ref, h_ref, wg_ref, wu_ref, wd_ref, gate_ref, ys_ref,
                xs_buf, ys_buf, xb_scr, hid_scr, wd_scr, gsem, ssem, *, ts, nw, tf):
    e = pl.program_id(0)
    f = pl.program_id(1)
    n_e = pl.num_programs(0)
    last_f = pl.num_programs(1) - 1
    s = xb_scr.shape[0]

    n_f = hid_scr.shape[0]
    per_step = s // n_f
    slot = e % 2

    def gather_start(expert, into, row):
        src = pl.multiple_of(idx_ref[expert * s + row] * nw, nw)
        pltpu.make_async_copy(h_ref.at[pl.ds(src, nw)],
                              xs_buf.at[into, pl.ds(pl.multiple_of(row * nw, nw), nw)], gsem.at[into]).start()

    def wait_gather(into):
        pltpu.make_async_copy(h_ref.at[pl.ds(0, s * nw)], xs_buf.at[into], gsem.at[into]).wait()

    def scatter_start(row):
        dst = pl.multiple_of(dest_ref[e * s + row] * nw, nw)
        pltpu.make_async_copy(ys_buf.at[pl.ds(row * nw, nw)], ys_ref.at[pl.ds(dst, nw)], ssem).start()

    def wait_scatter():
        pltpu.make_async_copy(ys_buf, ys_ref.at[pl.ds(0, s * nw)], ssem).wait()

    @pl.when(f == 0)
    def _():
        @pl.when(e == 0)
        def _():
            def body(i, carry):
                gather_start(0, 0, i)
                return carry
            lax.fori_loop(0, s, body, 0, unroll=8)

        wait_gather(slot)
        _load_packed_rows(xs_buf.at[slot], 0, s, nw, xb_scr, 0)

    nxt = jnp.minimum(e + 1, n_e - 1)
    for i in range(per_step):
        gather_start(nxt, 1 - slot, f * per_step + i)

    wg = wg_ref[0, 0].astype(BF16)
    wu = wu_ref[0, 0].astype(BF16)
    for r in range(s // ts):
        rows = slice(r * ts, (r + 1) * ts)
        xb = xb_scr[rows, :]
        hid_scr[f, rows, :] = (_silu(_dot(xb, wg)) * _dot(xb, wu)).astype(BF16)
    wd_scr[pl.ds(pl.multiple_of(f * tf, tf), tf), :] = wd_ref[0, 0].astype(BF16)

    @pl.when(f == last_f)
    def _():
        @pl.when(e > 0)
        def _():
            wait_scatter()

        for r in range(s // ts):
            rows = slice(r * ts, (r + 1) * ts)
            hid = jnp.concatenate([hid_scr[c, rows, :] for c in range(n_f)], axis=1)
            y = _dot(hid, wd_scr[...])
            _store_packed_rows(ys_buf, r * ts, ts, y * gate_ref[0, rows, :])
            for i in range(ts):
                scatter_start(r * ts + i)

        @pl.when(e == n_e - 1)
        def _():
            wait_scatter()
            wait_gather(1 - slot)


def expert_ffn(h_packed, idx, dest, w_gate, w_up, w_down, layer, gates, d):
    nw = d // (2 * LANES)
    e, s = gates.shape[0], gates.shape[1]
    ff = w_gate.shape[3]
    tf = 256 if ff % 256 == 0 else ff
    ts = max(t for t in range(8, 577, 8) if s % t == 0)
    assert h_packed.shape[0] >= s * nw and s % (ff // tf) == 0
    return pl.pallas_call(
        functools.partial(_ffn_kernel, ts=ts, nw=nw, tf=tf),
        name="expert_ffn",
        grid_spec=pltpu.PrefetchScalarGridSpec(
            num_scalar_prefetch=2, grid=(e, ff // tf),
            in_specs=[pl.BlockSpec(memory_space=pl.ANY),
                      pl.BlockSpec((1, 1, d, tf), lambda i, f, si, di: (layer, i, 0, f)),
                      pl.BlockSpec((1, 1, d, tf), lambda i, f, si, di: (layer, i, 0, f)),
                      pl.BlockSpec((1, 1, tf, d), lambda i, f, si, di: (layer, i, f, 0)),
                      pl.BlockSpec((1, s, 1), lambda i, f, si, di: (i, 0, 0))],
            out_specs=pl.BlockSpec(memory_space=pl.ANY),
            scratch_shapes=[pltpu.VMEM((2, s * nw, LANES), I32), pltpu.VMEM((s * nw, LANES), I32),
                            pltpu.VMEM((s, d), BF16), pltpu.VMEM((ff // tf, s, tf), BF16),
                            pltpu.VMEM((ff, d), BF16),
                            pltpu.SemaphoreType.DMA((2,)), pltpu.SemaphoreType.DMA(())]),
        out_shape=jax.ShapeDtypeStruct((e * s * nw, LANES), I32),
        compiler_params=_params(("arbitrary", "arbitrary")),
    )(idx, dest, h_packed, w_gate, w_up, w_down, gates)


COMB_ROWS = 256
COMB_ENT = 256


def _combine_kernel(kb_ref, ch_ref, flag_ref, ys_ref, xm_ref, start_ref, cnt_ref, mod_ref, *rest, d, final):
    if final:
        gf_ref, o_ref, chunk_scr = rest
    else:
        o_ref, chunk_scr = rest
    p = pl.program_id(0)
    flags = flag_ref[p]

    @pl.when((flags & 2) != 0)
    def _():
        o_ref[...] = jnp.zeros_like(o_ref)

    @pl.when((flags & 8) != 0)
    def _():
        _load_packed_rows(ys_ref, 0, COMB_ENT, d // (2 * LANES), chunk_scr, 0)

    @pl.when((flags & 1) != 0)
    def _():
        gpos = ch_ref[p] * COMB_ENT + lax.broadcasted_iota(I32, (COMB_ROWS, COMB_ENT), 1)
        st = start_ref[...]
        onehot = _ones((gpos >= st) & (gpos < st + cnt_ref[...]), BF16)
        o_ref[...] += _dot(onehot, chunk_scr[...])

    @pl.when((flags & 4) != 0)
    def _():
        m = mod_ref[0]
        out = xm_ref[...] + m[:, 5 * d:6 * d] * o_ref[...]
        if final:
            out = out * lax.rsqrt(jnp.mean(out * out, axis=-1, keepdims=True) + EPS) * gf_ref[...]
        o_ref[...] = out


def combine(ys_sorted, xm, start_col, cnt_col, mod, kb, ch, flags, rows, tiles_per_sample, n_x_tiles,
            n_samples, final_gain):
    d = xm.shape[1]
    nw = d // (2 * LANES)
    n_pairs = kb.shape[0]
    mod_idx = lambda p, kb, ch, fl: (jnp.where(kb[p] < n_x_tiles, kb[p] // tiles_per_sample, n_samples), 0, 0)
    in_specs = [pl.BlockSpec((COMB_ENT * nw, LANES), lambda p, kb, ch, fl: (ch[p], 0)),
                pl.BlockSpec((COMB_ROWS, d), lambda p, kb, ch, fl: (kb[p], 0)),
                pl.BlockSpec((COMB_ROWS, 1), lambda p, kb, ch, fl: (kb[p], 0)),
                pl.BlockSpec((COMB_ROWS, 1), lambda p, kb, ch, fl: (kb[p], 0)),
                pl.BlockSpec((1, 1, N_MOD * d), mod_idx)]
    args = [ys_sorted, xm, start_col, cnt_col, mod]
    if final_gain is not None:
        in_specs.append(pl.BlockSpec((1, d), lambda p, kb, ch, fl: (0, 0)))
        args.append(final_gain)
    return pl.pallas_call(
        functools.partial(_combine_kernel, d=d, final=final_gain is not None),
        name="combine",
        grid_spec=pltpu.PrefetchScalarGridSpec(
            num_scalar_prefetch=3, grid=(n_pairs,),
            in_specs=in_specs,
            out_specs=pl.BlockSpec((COMB_ROWS, d), lambda p, kb, ch, fl: (kb[p], 0)),
            scratch_shapes=[pltpu.VMEM((COMB_ENT, d), BF16)]),
        out_shape=jax.ShapeDtypeStruct((rows, d), F32),
        compiler_params=_params(("arbitrary",)),
    )(kb, ch, flags, *args)


def _combine_schedule(start_row, n_blocks, n_chunks):
    blk_start = start_row[::COMB_ROWS]
    blk_end = jnp.concatenate([blk_start[1:], jnp.full((1,), n_chunks * COMB_ENT, I32)])
    lo = jnp.minimum(blk_start // COMB_ENT, n_chunks - 1)
    hi = jnp.maximum(lo, (blk_end - 1) // COMB_ENT)
    cnt = hi - lo + 1
    off = jnp.cumsum(cnt) - cnt
    total = off[-1] + cnt[-1]
    n_pairs = n_blocks + n_chunks
    pidx = jnp.arange(n_pairs, dtype=I32)
    kb = jnp.sum((off[None, :] <= pidx[:, None]).astype(I32), axis=1) - 1
    valid = pidx < total
    ch = jnp.where(valid, lo[kb] + pidx - off[kb], hi[n_blocks - 1]).astype(I32)
    first = valid & (pidx == off[kb])
    last = valid & (pidx == off[kb] + cnt[kb] - 1)
    fresh = jnp.concatenate([jnp.ones((1,), bool), ch[1:] != ch[:-1]])
    flags = valid.astype(I32) + 2 * first.astype(I32) + 4 * last.astype(I32) + 8 * fresh.astype(I32)
    return kb, ch, flags


def _rope_tables(lx, n_ctx_rows, n_samples):
    n_freq = QK_DIM // 4
    freqs = ROPE_BASE ** (-jnp.arange(n_freq, dtype=F32) / n_freq)
    rows = lx // GRID_W
    row = jnp.repeat(jnp.arange(rows, dtype=F32), GRID_W)
    col = jnp.tile(jnp.arange(GRID_W, dtype=F32), rows)
    ang = jnp.concatenate([row[:, None] * freqs, col[:, None] * freqs], axis=-1)
    cos, sin = jnp.cos(ang), jnp.sin(ang)
    ct = jnp.tile(jnp.concatenate([cos, cos], axis=-1), (n_samples, LANES // QK_DIM))
    st = jnp.tile(jnp.concatenate([-sin, sin], axis=-1), (n_samples, LANES // QK_DIM))
    ct = jnp.concatenate([ct, jnp.ones((n_ctx_rows, LANES), F32)], axis=0)
    st = jnp.concatenate([st, jnp.zeros((n_ctx_rows, LANES), F32)], axis=0)
    return ct, st


def kernel(x, c, ctx, c_ctx, w_ada, b_ada, norm_mix, norm_ffn, w_in, w_out, ret_log_decay, ret_norm,
           diff_lambda, diff_norm, w_router, w_gate, w_up, w_down, norm_final):
    n_b, lx, d = x.shape
    lc = ctx.shape[1]
    depth = w_ada.shape[0]
    heads = ret_log_decay.shape[-1]
    dheads = diff_norm.shape[-1] // V_DIM
    n_e = w_router.shape[-1]
    rx, rc = n_b * lx, n_b * lc
    cap_x = CAPACITY_FACTOR * lx // n_e
    cap_c = CAPACITY_FACTOR * lc // n_e
    tm = 512 if (lx % 512 == 0 and rx % 512 == 0 and rc % 512 == 0) else 256
    tmo = tm
    assert lx % lc == 0 and rc % tmo == 0 and heads % 2 == 0 and n_b < MOD_ROWS

    qk_scale = QK_DIM ** -0.5
    n_rq = heads * QK_DIM // LANES
    rope = tuple([True] * (2 * n_rq) + [False] * (2 * heads) + [True] * (2 * dheads) + [False] * dheads)
    slab_scale = tuple([qk_scale] * n_rq + [1.0] * (n_rq + 2 * heads) + [qk_scale * math.log2(math.e)] * dheads
                       + [1.0] * (2 * dheads))
    w_in_bf = _prep_w_in(w_in, slab_scale)

    xa = jnp.concatenate([x.reshape(rx, d), ctx.reshape(rc, d)], axis=0)
    cc = jnp.concatenate([c, c_ctx[None], jnp.zeros((MOD_ROWS - n_b - 1, d), F32)], axis=0)
    mods = ada_modulation(cc, w_ada, b_ada)
    ct, st = _rope_tables(lx, rc, n_b)
    log_g = jnp.log1p(-jnp.exp(ret_log_decay.astype(F32)))
    out = None

    for layer in range(depth):
        last = layer == depth - 1
        lam_init = 0.8 - 0.6 * math.exp(-0.3 * layer)
        mod = mods[layer].reshape(MOD_ROWS, 1, N_MOD * d)
        p = in_projection(xa, mod, norm_mix[layer][None], w_in_bf, layer, ct, st,
                          rx // tm, lx // tm, n_b, rope, tm)

        ret = retention(p, log_g[layer], ret_norm[layer][None], n_b, lx, lc, heads, not last)
        dif_x = diff_attention(p, diff_lambda[layer], diff_norm[layer][None], n_b, lx, lc, heads, dheads,
                               lam_init, True, 1024 if lx % 1024 == 0 else 256)
        if last:
            rows = rx
            mix_c = None
        else:
            rows = rx + rc
            dif_c = diff_attention(p, diff_lambda[layer], diff_norm[layer][None], n_b, lx, lc, heads, dheads,
                                   lam_init, False, lc)
            mix_c = (ret[1], dif_c)

        xm, h2, aff = out_projection((ret[0], dif_x), mix_c, w_out[layer].astype(BF16), xa, mod,
                                     norm_ffn[layer][None], w_router[layer], rows, rx // tmo, lx // tmo, n_b, tmo)
        aff_t = aff.T

        sel = [expert_choice(aff_t, n_b, lx, 0, cap_x, 0, 0)]
        if not last:
            sel.append(expert_choice(aff_t, n_b, lc, rx // lc, cap_c, rx, n_b * n_e * cap_x))
        to_es = lambda t: jnp.transpose(t, (2, 0, 1)).reshape(n_e, -1)
        idx = jnp.concatenate([to_es(s[0]) for s in sel], axis=1)
        gates = jnp.concatenate([to_es(s[1]) for s in sel], axis=1)
        dest = jnp.concatenate([to_es(s[2]) for s in sel], axis=1)
        start_row = jnp.concatenate([s[3][0] for s in sel])
        cnt_row = jnp.concatenate([s[4][0] for s in sel])
        slots = idx.shape[1]
        n_ent = n_e * slots

        ys_sorted = expert_ffn(h2, idx.reshape(-1), dest.reshape(-1), w_gate, w_up, w_down, layer,
                               gates.reshape(n_e, slots, 1), d)

        kb, ch, flags = _combine_schedule(start_row, rows // COMB_ROWS, n_ent // COMB_ENT)
        xa_new = combine(ys_sorted, xm, start_row[:, None], cnt_row[:, None], mod, kb, ch, flags, rows,
                         lx // COMB_ROWS, rx // COMB_ROWS, n_b, norm_final[None] if last else None)
        if last:
            out = xa_new
        else:
            xa = xa_new
    return out.reshape(n_b, lx, d)
```

```python
import functools
import math

import jax
import jax.numpy as jnp
from jax import lax
from jax.experimental import pallas as pl
from jax.experimental.pallas import tpu as pltpu

F32 = jnp.float32
BF16 = jnp.bfloat16
I32 = jnp.int32

EPS = 1e-6
N_MOD = 6
GRID_W = 64
ROPE_BASE = 10000.0
QK_DIM = 64
V_DIM = 128
LANES = 128
CHUNK = 128
CAPACITY_FACTOR = 2
MOD_ROWS = 8
VMEM_LIMIT = 56 * 1024 * 1024


def _params(sem, vmem=VMEM_LIMIT):
    return pltpu.CompilerParams(dimension_semantics=sem, vmem_limit_bytes=vmem)


def _silu(x):
    return x / (1.0 + jnp.exp(-x))


def _dot(a, b):
    return jnp.dot(a, b, preferred_element_type=F32)


def _dot_nt(a, b):
    return lax.dot_general(a, b, (((1,), (1,)), ((), ())), preferred_element_type=F32)


def _ada_kernel(c_ref, w_ref, b_ref, o_ref):
    s = _silu(c_ref[...]).astype(BF16)
    o_ref[0] = _dot(s, w_ref[0].astype(BF16)) + b_ref[0]


def ada_modulation(cc, w_ada, b_ada):
    depth, d, n = w_ada.shape
    tn = 1024 if n % 1024 == 0 else n
    return pl.pallas_call(
        _ada_kernel,
        name="ada_mod",
        grid=(depth, n // tn),
        in_specs=[pl.BlockSpec((MOD_ROWS, d), lambda l, j: (0, 0)),
                  pl.BlockSpec((1, d, tn), lambda l, j: (l, 0, j)),
                  pl.BlockSpec((1, 1, tn), lambda l, j: (l, 0, j))],
        out_specs=pl.BlockSpec((1, MOD_ROWS, tn), lambda l, j: (l, 0, j)),
        out_shape=jax.ShapeDtypeStruct((depth, MOD_ROWS, n), F32),
        compiler_params=_params(("arbitrary", "arbitrary")),
    )(cc, w_ada, b_ada.reshape(depth, 1, n))


def _rmsnorm_mod(x, gain, shift, scale):
    y = x * lax.rsqrt(jnp.mean(x * x, axis=-1, keepdims=True) + EPS) * gain
    return y * (1.0 + scale) + shift


MXU_N = 256


def _group_of(lane):
    return (lane >> 6) & 1


def _swap_halves(a):
    lane = lax.broadcasted_iota(I32, a.shape, 1)
    return jnp.where((lane & 32) == 0, pltpu.roll(a, 96, 1), pltpu.roll(a, 32, 1))


def _inproj_kernel(x_ref, mod_ref, g_ref, w_ref, ct_ref, st_ref, o_ref, *, d, rope):
    m = mod_ref[0]
    h = _rmsnorm_mod(x_ref[...], g_ref[...], m[:, 0:d], m[:, d:2 * d]).astype(BF16)
    per = MXU_N // LANES
    for s in range(w_ref.shape[2] // MXU_N):
        acc = _dot(h, w_ref[0, :, s * MXU_N:(s + 1) * MXU_N])
        for t in range(per):
            a = acc[:, t * LANES:(t + 1) * LANES]
            if rope[s * per + t]:
                a = a * ct_ref[...] + _swap_halves(a) * st_ref[...]
            c0 = s * MXU_N + t * LANES
            o_ref[:, c0:c0 + LANES] = a.astype(BF16)


def in_projection(xa, mod, gain, w_bf, layer, ct, st, n_x_tiles, tiles_per_sample, n_samples, rope, tm):
    r, d = xa.shape
    n = w_bf.shape[2]
    mod_idx = lambda i: (jnp.where(i < n_x_tiles, i // tiles_per_sample, n_samples), 0, 0)
    return pl.pallas_call(
        functools.partial(_inproj_kernel, d=d, rope=rope),
        name="in_proj",
        grid=(r // tm,),
        in_specs=[pl.BlockSpec((tm, d), lambda i: (i, 0)),
                  pl.BlockSpec((1, 1, N_MOD * d), mod_idx),
                  pl.BlockSpec((1, d), lambda i: (0, 0)),
                  pl.BlockSpec((1, d, n), lambda i: (layer, 0, 0), pipeline_mode=pl.Buffered(1)),
                  pl.BlockSpec((tm, LANES), lambda i: (i, 0)),
                  pl.BlockSpec((tm, LANES), lambda i: (i, 0))],
        out_specs=pl.BlockSpec((tm, n), lambda i: (i, 0)),
        out_shape=jax.ShapeDtypeStruct((r, n), BF16),
        compiler_params=_params(("arbitrary",)),
    )(xa, mod, gain, w_bf, ct, st)


def _prep_w_in(w_in, slab_scale):
    col_scale = jnp.repeat(jnp.asarray(slab_scale, F32), LANES)
    return (w_in * col_scale).astype(BF16)


def _ret_kernel(lg_ref, qx_ref, kx_ref, vx_ref, gx_ref, qc_ref, kc_ref, vc_ref, gc_ref, gain_ref,
                *rest, lx, lc, ctx_out):
    if ctx_out:
        ox_ref, oc_ref, accx, accc = rest
    else:
        ox_ref, accx = rest
    pair = pl.program_id(1)
    w2 = 2 * V_DIM
    lane = lax.broadcasted_iota(I32, (CHUNK, LANES), 1)
    dif = (lax.broadcasted_iota(I32, (CHUNK, CHUNK), 0) - lax.broadcasted_iota(I32, (CHUNK, CHUNK), 1)).astype(F32)
    pos = lax.broadcasted_iota(I32, (CHUNK, 1), 0).astype(F32)
    cpos = lax.broadcasted_iota(I32, (lc, 1), 0).astype(F32)
    c_len = float(CHUNK)
    lgf = [lg_ref[0, 2 * pair + hh] for hh in range(2)]
    lgb = [lg_ref[1, 2 * pair + hh] for hh in range(2)]

    def per_head(fn):
        vals = [fn(hh) for hh in range(2)]
        return jnp.concatenate([jnp.broadcast_to(v, (v.shape[0], V_DIM)) for v in vals], axis=1)

    decay = per_head(lambda hh: jnp.where(dif >= 0, jnp.exp(lgf[hh] * jnp.maximum(dif, 0.0)),
                                          jnp.exp(lgb[hh] * jnp.maximum(-dif, 0.0))))
    qd_f = per_head(lambda hh: jnp.exp(lgf[hh] * (pos + 1.0)))
    kd_f = per_head(lambda hh: jnp.exp(lgf[hh] * (c_len - 1.0 - pos)))
    qd_b = per_head(lambda hh: jnp.exp(lgb[hh] * (c_len - pos)))
    kd_b = per_head(lambda hh: jnp.exp(lgb[hh] * pos))
    one = jnp.ones((1, 1), F32)
    cd_f = per_head(lambda hh: jnp.exp(lgf[hh] * c_len) * one)
    cd_b = per_head(lambda hh: jnp.exp(lgb[hh] * c_len) * one)
    srow = lax.broadcasted_iota(I32, (LANES, w2), 0)
    scol = lax.broadcasted_iota(I32, (LANES, w2), 1)
    diag = (_group_of(srow) == 0) == (scol < V_DIM)
    zeros_v = jnp.zeros((CHUNK, V_DIM), BF16)

    def kv_state(k, v_scaled):
        return jnp.where(diag, _dot(k.astype(F32).T.astype(BF16), v_scaled.astype(BF16)), 0.0)

    kc = kc_ref[...]
    vc = vc_ref[...].astype(F32)
    s_f = kv_state(kc, vc * per_head(lambda hh: jnp.exp(lgf[hh] * (lc - 1.0 - cpos))))
    s_b = kv_state(kc, vc * per_head(lambda hh: jnp.exp(lgb[hh] * cpos)))

    def sweeps(q_ref, k_ref, v_ref, g_ref, o_ref, acc_ref, n, s_f0, s_b0):
        def rows_of(c):
            return pl.ds(pl.multiple_of(c * CHUNK, CHUNK), CHUNK)

        def forward(c, s):
            rows = rows_of(c)
            q, k, v = q_ref[rows, :], k_ref[rows, :], v_ref[rows, :]
            k_bd = jnp.concatenate([jnp.where(_group_of(lane) == 0, k, 0), jnp.where(_group_of(lane) == 1, k, 0)],
                                   axis=0)
            v_bd = jnp.concatenate([jnp.concatenate([v[:, :V_DIM], zeros_v], axis=1),
                                    jnp.concatenate([zeros_v, v[:, V_DIM:]], axis=1)], axis=0)
            sc = _dot_nt(q, k_bd) * decay
            acc_ref[rows, :] = _dot(sc.astype(BF16), v_bd) + _dot(q, s.astype(BF16)) * qd_f
            return s * cd_f + kv_state(k, v.astype(F32) * kd_f)

        def backward(t, s):
            rows = rows_of(n - 1 - t)
            q, k, v = q_ref[rows, :], k_ref[rows, :], v_ref[rows, :]
            y = acc_ref[rows, :] + _dot(q, s.astype(BF16)) * qd_b
            for hh in range(2):
                vsl = slice(hh * V_DIM, (hh + 1) * V_DIM)
                yh = y[:, vsl]
                yc = yh - jnp.mean(yh, axis=-1, keepdims=True)
                out = yc * lax.rsqrt(jnp.mean(yc * yc, axis=-1, keepdims=True) + EPS) * gain_ref[:, vsl]
                o_ref[rows, vsl] = (_silu(g_ref[rows, vsl].astype(F32)) * out).astype(BF16)
            return s * cd_b + kv_state(k, v.astype(F32) * kd_b)

        unroll = 4 if n % 4 == 0 else n
        lax.fori_loop(0, n, forward, s_f0, unroll=unroll)
        lax.fori_loop(0, n, backward, s_b0, unroll=unroll)

    sweeps(qx_ref, kx_ref, vx_ref, gx_ref, ox_ref, accx, lx // CHUNK, s_f, s_b)
    if ctx_out:
        zero = jnp.zeros((LANES, w2), F32)
        sweeps(qc_ref, kc_ref, vc_ref, gc_ref, oc_ref, accc, lc // CHUNK, zero, zero)


def retention(p, lg, gain, n_samples, lx, lc, heads, ctx_out):
    pairs = heads // 2
    half = heads // 2
    cb = n_samples * lx // lc
    w2 = 2 * V_DIM
    in_specs = [
        pl.BlockSpec(memory_space=pltpu.SMEM),
        pl.BlockSpec((lx, LANES), lambda b, h: (b, h)),
        pl.BlockSpec((lx, LANES), lambda b, h: (b, half + h)),
        pl.BlockSpec((lx, w2), lambda b, h: (b, half + h)),
        pl.BlockSpec((lx, w2), lambda b, h: (b, heads + h)),
        pl.BlockSpec((lc, LANES), lambda b, h: (cb + b, h)),
        pl.BlockSpec((lc, LANES), lambda b, h: (cb + b, half + h)),
        pl.BlockSpec((lc, w2), lambda b, h: (cb + b, half + h)),
        pl.BlockSpec((lc, w2), lambda b, h: (cb + b, heads + h)),
        pl.BlockSpec((1, w2), lambda b, h: (0, h)),
    ]
    out_specs = [pl.BlockSpec((lx, w2), lambda b, h: (b, h))]
    out_shape = [jax.ShapeDtypeStruct((n_samples * lx, heads * V_DIM), BF16)]
    scratch = [pltpu.VMEM((lx, w2), F32)]
    if ctx_out:
        out_specs.append(pl.BlockSpec((lc, w2), lambda b, h: (b, h)))
        out_shape.append(jax.ShapeDtypeStruct((n_samples * lc, heads * V_DIM), BF16))
        scratch.append(pltpu.VMEM((lc, w2), F32))
    return pl.pallas_call(
        functools.partial(_ret_kernel, lx=lx, lc=lc, ctx_out=ctx_out),
        name="retention",
        grid=(n_samples, pairs),
        in_specs=in_specs, out_specs=out_specs, out_shape=out_shape, scratch_shapes=scratch,
        compiler_params=_params(("arbitrary", "arbitrary")),
    )(lg, p, p, p, p, p, p, p, p, gain)


def _diff_kernel(lam_ref, q_ref, kc_ref, vc_ref, *rest, lam_init, has_x, kb):
    if has_x:
        kx_ref, vx_ref, gain_ref, o_ref = rest
    else:
        gain_ref, o_ref = rest
    lv = lam_ref[...]
    lam = (jnp.exp(jnp.sum(lv[0:1] * lv[1:2], axis=-1, keepdims=True))
           - jnp.exp(jnp.sum(lv[2:3] * lv[3:4], axis=-1, keepdims=True)) + lam_init)
    q = q_ref[...]
    tq = q.shape[0]
    lane = lax.broadcasted_iota(I32, q.shape, 1)
    blocks = [(kc_ref, vc_ref, 0, kc_ref.shape[0])]
    if has_x:
        blocks += [(kx_ref, vx_ref, r0, kb) for r0 in range(0, kx_ref.shape[0], kb)]
    v_ext = [jnp.concatenate([v_ref[r0:r0 + n, :], jnp.ones((n, V_DIM), BF16)], axis=1) for _, v_ref, r0, n in blocks]
    heads_out = []
    for t in range(2):
        qm = jnp.where(_group_of(lane) == t, q, 0)
        m = jnp.full((tq, 1), -jnp.inf, F32)
        acc = jnp.zeros((tq, 2 * V_DIM), F32)
        for (k_ref, _, r0, n), ve in zip(blocks, v_ext):
            s = _dot_nt(qm, k_ref[r0:r0 + n, :])
            m_new = jnp.maximum(m, jnp.max(s, axis=-1, keepdims=True))
            acc = acc * jnp.exp2(m - m_new) + _dot(jnp.exp2(s - m_new).astype(BF16), ve)
            m = m_new
        heads_out.append(acc[:, :V_DIM] * (1.0 / acc[:, V_DIM:V_DIM + 1]))
    out = heads_out[0] - lam * heads_out[1]
    y = out * lax.rsqrt(jnp.mean(out * out, axis=-1, keepdims=True) + EPS) * gain_ref[...]
    o_ref[...] = (y * (1.0 - lam_init)).astype(BF16)


def diff_attention(p, lam_vec, gain, n_samples, lx, lc, ret_heads, heads, lam_init, latent, tq):
    qo = 3 * ret_heads
    ko = qo + heads
    vo = ko + heads
    cb = n_samples * lx // lc
    lq = lx if latent else lc
    nq = lq // tq
    q_row = (lambda b, t: b * nq + t) if latent else (lambda b, t: n_samples * lx // tq + b * nq + t)
    in_specs = [
        pl.BlockSpec((4, QK_DIM), lambda b, h, t: (0, 0)),
        pl.BlockSpec((tq, LANES), lambda b, h, t: (q_row(b, t), qo + h)),
        pl.BlockSpec((lc, LANES), lambda b, h, t: (cb + b, ko + h)),
        pl.BlockSpec((lc, LANES), lambda b, h, t: (cb + b, vo + h)),
    ]
    args = [lam_vec, p, p, p]
    if latent:
        in_specs += [pl.BlockSpec((lx, LANES), lambda b, h, t: (b, ko + h)),
                     pl.BlockSpec((lx, LANES), lambda b, h, t: (b, vo + h))]
        args += [p, p]
    in_specs.append(pl.BlockSpec((1, LANES), lambda b, h, t: (0, h)))
    args.append(gain)
    return pl.pallas_call(
        functools.partial(_diff_kernel, lam_init=lam_init, has_x=latent, kb=min(1024, lx)),
        name="diff_attn_x" if latent else "diff_attn_ctx",
        grid=(n_samples, heads, nq),
        in_specs=in_specs,
        out_specs=pl.BlockSpec((tq, LANES), lambda b, h, t: (b * nq + t, h)),
        out_shape=jax.ShapeDtypeStruct((n_samples * lq, heads * V_DIM), BF16),
        compiler_params=_params(("arbitrary", "arbitrary", "arbitrary")),
    )(*args)


def _pack_pair(a, b):
    ua = pltpu.bitcast(a.astype(BF16).astype(F32), I32)
    ub = pltpu.bitcast(b.astype(BF16).astype(F32), I32)
    return ua | lax.shift_right_logical(ub, 16)


def _unpack_pair(w):
    return (pltpu.bitcast(w & -65536, F32).astype(BF16), pltpu.bitcast(w << 16, F32).astype(BF16))


def _store_packed_rows(ref, row0, n_rows, val):
    nw = val.shape[1] // (2 * LANES)
    for j in range(nw):
        w = _pack_pair(val[:, j * LANES:(j + 1) * LANES], val[:, (nw + j) * LANES:(nw + j + 1) * LANES])
        ref[pl.ds(row0 * nw + j, n_rows, stride=nw), :] = w


def _load_packed_rows(ref, row0, n_rows, nw, dst, dst_row0):
    for j in range(nw):
        hi, lo = _unpack_pair(ref[pl.ds(row0 * nw + j, n_rows, stride=nw), :])
        dst[dst_row0:dst_row0 + n_rows, j * LANES:(j + 1) * LANES] = hi
        dst[dst_row0:dst_row0 + n_rows, (nw + j) * LANES:(nw + j + 1) * LANES] = lo


def _outproj_kernel(*refs, d, wr, n_x_tiles, with_ctx):
    if with_ctx:
        retx_ref, difx_ref, retc_ref, difc_ref = refs[:4]
        is_ctx = pl.program_id(0) >= n_x_tiles
        ret = jnp.where(is_ctx, retc_ref[...], retx_ref[...])
        dif = jnp.where(is_ctx, difc_ref[...], difx_ref[...])
        refs = refs[4:]
    else:
        ret, dif = refs[0][...], refs[1][...]
        refs = refs[2:]
    w_ref, x_ref, mod_ref, g_ref, wr_ref, xm_ref, h_ref, aff_ref = refs
    y = _dot(ret, w_ref[0:wr, :]) + _dot(dif, w_ref[wr:, :])
    m = mod_ref[0]
    xm = x_ref[...] + m[:, 2 * d:3 * d] * y
    xm_ref[...] = xm
    h = _rmsnorm_mod(xm, g_ref[...], m[:, 3 * d:4 * d], m[:, 4 * d:5 * d])
    _store_packed_rows(h_ref, 0, h.shape[0], h)
    n_e = wr_ref.shape[1] // 2
    h_hi = h.astype(BF16)
    h_lo = (h - h_hi.astype(F32)).astype(BF16)
    both = _dot(h_hi, wr_ref[...])
    logits = both[:, :n_e] + both[:, n_e:] + _dot(h_lo, wr_ref[...])[:, :n_e]
    e = jnp.exp(logits - jnp.max(logits, axis=-1, keepdims=True))
    aff_ref[...] = e / jnp.sum(e, axis=-1, keepdims=True)


def out_projection(mix_x, mix_c, w_bf, xa, mod, gain, w_router, rows, n_x_tiles, tiles_per_sample, n_samples, tm):
    d = xa.shape[1]
    wr = mix_x[0].shape[1]
    wd = mix_x[1].shape[1]
    e = w_router.shape[1]
    nw = d // (2 * LANES)
    wr_hi = w_router.astype(BF16)
    wr_split = jnp.concatenate([wr_hi, (w_router - wr_hi.astype(F32)).astype(BF16)], axis=1)
    mod_idx = lambda i: (jnp.where(i < n_x_tiles, i // tiles_per_sample, n_samples), 0, 0)
    x_idx = lambda i: (jnp.minimum(i, n_x_tiles - 1), 0)
    c_idx = lambda i: (jnp.maximum(i - n_x_tiles, 0), 0)
    mix_specs = [pl.BlockSpec((tm, wr), x_idx), pl.BlockSpec((tm, wd), x_idx)]
    mix_args = list(mix_x)
    if mix_c is not None:
        mix_specs += [pl.BlockSpec((tm, wr), c_idx), pl.BlockSpec((tm, wd), c_idx)]
        mix_args += list(mix_c)
    return pl.pallas_call(
        functools.partial(_outproj_kernel, d=d, wr=wr, n_x_tiles=n_x_tiles, with_ctx=mix_c is not None),
        name="out_proj_router",
        grid=(rows // tm,),
        in_specs=mix_specs + [
                  pl.BlockSpec(w_bf.shape, lambda i: (0, 0), pipeline_mode=pl.Buffered(1)),
                  pl.BlockSpec((tm, d), lambda i: (i, 0)),
                  pl.BlockSpec((1, 1, N_MOD * d), mod_idx),
                  pl.BlockSpec((1, d), lambda i: (0, 0)),
                  pl.BlockSpec((d, 2 * e), lambda i: (0, 0))],
        out_specs=[pl.BlockSpec((tm, d), lambda i: (i, 0)),
                   pl.BlockSpec((tm * nw, LANES), lambda i: (i, 0)),
                   pl.BlockSpec((tm, e), lambda i: (i, 0))],
        out_shape=[jax.ShapeDtypeStruct((rows, d), F32),
                   jax.ShapeDtypeStruct((rows * nw, LANES), I32),
                   jax.ShapeDtypeStruct((rows, e), F32)],
        compiler_params=_params(("arbitrary",)),
    )(*mix_args, w_bf, xa, mod, gain, wr_split)


def _ones(mask, dtype):
    return jnp.where(mask, 1.0, 0.0).astype(dtype)


def _prefix_lanes(m):
    upper = _ones(lax.broadcasted_iota(I32, (LANES, LANES), 0) < lax.broadcasted_iota(I32, (LANES, LANES), 1), BF16)
    run = jnp.zeros((m.shape[0], 1), F32)
    outs = []
    for s in range(m.shape[1] // LANES):
        seg = m[:, s * LANES:(s + 1) * LANES]
        outs.append(_dot(seg, upper) + run)
        run = run + jnp.sum(seg.astype(F32), axis=1, keepdims=True)
    return jnp.concatenate(outs, axis=1)


def _topk_kernel(a_ref, idx_ref, gate_ref, dest_ref, start_ref, cnt_ref, blk_scr, first_scr,
                 *, lg, cap, row_base, ent_base):
    g = pl.program_id(0)
    a = a_ref[...]
    n_e = a.shape[0]
    bits = pltpu.bitcast(a, I32)

    def bisect(_, lohi):
        lo, hi = lohi
        mid = lo + ((hi - lo) >> 1)
        ge = jnp.sum(_ones(bits >= mid, F32), axis=1, keepdims=True) >= cap
        return jnp.where(ge, mid, lo), jnp.where(ge, hi, mid)

    thr, _ = lax.fori_loop(0, 31, bisect,
                           (jnp.zeros((n_e, 1), I32), jnp.full((n_e, 1), 0x7F800000, I32)))
    gt = bits > thr
    eq = bits == thr
    need = cap - jnp.sum(_ones(gt, F32), axis=1, keepdims=True)
    sel = gt | (eq & (_prefix_lanes(_ones(eq, BF16)) < need))
    selb = _ones(sel, BF16)
    pos = _prefix_lanes(selb)
    start_t = jnp.sum(pos, axis=0, keepdims=True)
    lower = _ones(lax.broadcasted_iota(I32, (n_e, n_e), 0) > lax.broadcasted_iota(I32, (n_e, n_e), 1), BF16)
    base = (ent_base + g * (n_e * cap)).astype(F32)
    start_ref[...] = (start_t + base).astype(I32)
    cnt_ref[...] = jnp.sum(_ones(sel, F32), axis=0, keepdims=True).astype(I32)
    posm = jnp.where(sel, pos, -1.0)
    dest = start_t + base + _dot(lower, selb)

    nb = lg // LANES
    nbp = blk_scr.shape[1] // n_e
    if nbp != nb:
        blk_scr[...] = jnp.zeros_like(blk_scr)
    for k in range(nb):
        blk = slice(k * LANES, (k + 1) * LANES)
        pieces = []
        for x in (posm[:, blk], a[:, blk], dest[:, blk]):
            x1 = x.astype(BF16).astype(F32)
            x2 = (x - x1).astype(BF16).astype(F32)
            pieces += [x1, x2, (x - x1 - x2).astype(BF16).astype(F32)]
        for j in range(9):
            blk_scr[j, pl.ds(k, n_e, stride=nbp), :] = pieces[j]
    before = _ones(lax.broadcasted_iota(I32, (lg, nb), 0) < LANES * lax.broadcasted_iota(I32, (lg, nb), 1), BF16)
    first_scr[...] = _dot(selb, before)

    tok_base = (row_base + g * lg).astype(F32)
    slot = lax.broadcasted_iota(I32, (cap, 1), 0).astype(F32)
    blane = lax.broadcasted_iota(I32, (cap, nbp), 1).astype(F32)
    lane_f = lax.broadcasted_iota(I32, (cap, LANES), 1).astype(F32)
    col = lax.broadcasted_iota(I32, (cap, n_e), 1)
    w3 = 3 * LANES

    def compact(e, carry):
        ia, ga, da = carry
        kb = jnp.sum(_ones(first_scr[pl.ds(e, 1), :] <= slot, F32), axis=1, keepdims=True) - 1.0
        onehot = _ones(blane == kb, BF16)
        own = pl.ds(pl.multiple_of(e * nbp, nbp), nbp)
        rows = jnp.concatenate([blk_scr[j, own, :] for j in range(9)], axis=1).astype(BF16)
        parts = _dot(onehot, rows)
        pos_g, aff_g, dest_g = [parts[:, q * w3:q * w3 + LANES] + parts[:, q * w3 + LANES:q * w3 + 2 * LANES]
                                + parts[:, q * w3 + 2 * LANES:(q + 1) * w3] for q in range(3)]
        m = pos_g == slot
        i_e = jnp.sum(jnp.where(m, lane_f, 0.0), axis=1, keepdims=True) + kb * LANES + tok_base
        g_e = jnp.sum(jnp.where(m, aff_g, 0.0), axis=1, keepdims=True)
        d_e = jnp.sum(jnp.where(m, dest_g, 0.0), axis=1, keepdims=True)
        here = col == e
        return jnp.where(here, i_e, ia), jnp.where(here, g_e, ga), jnp.where(here, d_e, da)

    z = jnp.zeros((cap, n_e), F32)
    ia, ga, da = lax.fori_loop(0, n_e, compact, (z, z, z))
    idx_ref[0] = ia.astype(I32)
    gate_ref[0] = ga
    dest_ref[0] = da.astype(I32)


def expert_choice(aff_t, groups, lg, first_block, cap, row_base, ent_base):
    e = aff_t.shape[0]
    return pl.pallas_call(
        functools.partial(_topk_kernel, lg=lg, cap=cap, row_base=row_base, ent_base=ent_base),
        name="expert_choice",
        grid=(groups,),
        in_specs=[pl.BlockSpec((e, lg), lambda g: (0, first_block + g))],
        out_specs=[pl.BlockSpec((1, cap, e), lambda g: (g, 0, 0)),
                   pl.BlockSpec((1, cap, e), lambda g: (g, 0, 0)),
                   pl.BlockSpec((1, cap, e), lambda g: (g, 0, 0)),
                   pl.BlockSpec((1, lg), lambda g: (0, g)),
                   pl.BlockSpec((1, lg), lambda g: (0, g))],
        out_shape=[jax.ShapeDtypeStruct((groups, cap, e), I32),
                   jax.ShapeDtypeStruct((groups, cap, e), F32),
                   jax.ShapeDtypeStruct((groups, cap, e), I32),
                   jax.ShapeDtypeStruct((1, groups * lg), I32),
                   jax.ShapeDtypeStruct((1, groups * lg), I32)],
        scratch_shapes=[pltpu.VMEM((9, e * (-(-(lg // LANES) // 8) * 8), LANES), F32),
                        pltpu.VMEM((e, lg // LANES), F32)],
        compiler_params=_params(("arbitrary",)),
    )(aff_t)


def _ffn_kernel(idx_ref, dest_ref, h_ref, wg_ref, wu_ref, wd_ref, gate_ref, ys_ref,
                xs_buf, ys_buf, xb_scr, hid_scr, wd_scr, gsem, ssem, *, ts, nw, tf):
    e = pl.program_id(0)
    f = pl.program_id(1)
    n_e = pl.num_programs(0)
    last_f = pl.num_programs(1) - 1
    s = xb_scr.shape[0]

    n_f = hid_scr.shape[0]
    per_step = s // n_f
    slot = e % 2

    def gather_start(expert, into, row):
        src = pl.multiple_of(idx_ref[expert * s + row] * nw, nw)
        pltpu.make_async_copy(h_ref.at[pl.ds(src, nw)],
                              xs_buf.at[into, pl.ds(pl.multiple_of(row * nw, nw), nw)], gsem.at[into]).start()

    def wait_gather(into):
        pltpu.make_async_copy(h_ref.at[pl.ds(0, s * nw)], xs_buf.at[into], gsem.at[into]).wait()

    def scatter_start(row):
        dst = pl.multiple_of(dest_ref[e * s + row] * nw, nw)
        pltpu.make_async_copy(ys_buf.at[pl.ds(row * nw, nw)], ys_ref.at[pl.ds(dst, nw)], ssem).start()

    def wait_scatter():
        pltpu.make_async_copy(ys_buf, ys_ref.at[pl.ds(0, s * nw)], ssem).wait()

    @pl.when(f == 0)
    def _():
        @pl.when(e == 0)
        def _():
            def body(i, carry):
                gather_start(0, 0, i)
                return carry
            lax.fori_loop(0, s, body, 0, unroll=8)

        wait_gather(slot)
        _load_packed_rows(xs_buf.at[slot], 0, s, nw, xb_scr, 0)

    nxt = jnp.minimum(e + 1, n_e - 1)
    for i in range(per_step):
        gather_start(nxt, 1 - slot, f * per_step + i)

    wg = wg_ref[0, 0].astype(BF16)
    wu = wu_ref[0, 0].astype(BF16)
    for r in range(s // ts):
        rows = slice(r * ts, (r + 1) * ts)
        xb = xb_scr[rows, :]
        hid_scr[f, rows, :] = (_silu(_dot(xb, wg)) * _dot(xb, wu)).astype(BF16)
    wd_scr[pl.ds(pl.multiple_of(f * tf, tf), tf), :] = wd_ref[0, 0].astype(BF16)

    @pl.when(f == last_f)
    def _():
        @pl.when(e > 0)
        def _():
            wait_scatter()

        for r in range(s // ts):
            rows = slice(r * ts, (r + 1) * ts)
            hid = jnp.concatenate([hid_scr[c, rows, :] for c in range(n_f)], axis=1)
            y = _dot(hid, wd_scr[...])
            _store_packed_rows(ys_buf, r * ts, ts, y * gate_ref[0, rows, :])
            for i in range(ts):
                scatter_start(r * ts + i)

        @pl.when(e == n_e - 1)
        def _():
            wait_scatter()
            wait_gather(1 - slot)


def expert_ffn(h_packed, idx, dest, w_gate, w_up, w_down, layer, gates, d):
    nw = d // (2 * LANES)
    e, s = gates.shape[0], gates.shape[1]
    ff = w_gate.shape[3]
    tf = 256 if ff % 256 == 0 else ff
    ts = max(t for t in range(8, 577, 8) if s % t == 0)
    assert h_packed.shape[0] >= s * nw and s % (ff // tf) == 0
    return pl.pallas_call(
        functools.partial(_ffn_kernel, ts=ts, nw=nw, tf=tf),
        name="expert_ffn",
        grid_spec=pltpu.PrefetchScalarGridSpec(
            num_scalar_prefetch=2, grid=(e, ff // tf),
            in_specs=[pl.BlockSpec(memory_space=pl.ANY),
                      pl.BlockSpec((1, 1, d, tf), lambda i, f, si, di: (layer, i, 0, f)),
                      pl.BlockSpec((1, 1, d, tf), lambda i, f, si, di: (layer, i, 0, f)),
                      pl.BlockSpec((1, 1, tf, d), lambda i, f, si, di: (layer, i, f, 0)),
                      pl.BlockSpec((1, s, 1), lambda i, f, si, di: (i, 0, 0))],
            out_specs=pl.BlockSpec(memory_space=pl.ANY),
            scratch_shapes=[pltpu.VMEM((2, s * nw, LANES), I32), pltpu.VMEM((s * nw, LANES), I32),
                            pltpu.VMEM((s, d), BF16), pltpu.VMEM((ff // tf, s, tf), BF16),
                            pltpu.VMEM((ff, d), BF16),
                            pltpu.SemaphoreType.DMA((2,)), pltpu.SemaphoreType.DMA(())]),
        out_shape=jax.ShapeDtypeStruct((e * s * nw, LANES), I32),
        compiler_params=_params(("arbitrary", "arbitrary")),
    )(idx, dest, h_packed, w_gate, w_up, w_down, gates)


COMB_ROWS = 256
COMB_ENT = 256


def _combine_kernel(kb_ref, ch_ref, flag_ref, ys_ref, xm_ref, start_ref, cnt_ref, mod_ref, *rest, d, final):
    if final:
        gf_ref, o_ref, chunk_scr = rest
    else:
        o_ref, chunk_scr = rest
    p = pl.program_id(0)
    flags = flag_ref[p]

    @pl.when((flags & 2) != 0)
    def _():
        o_ref[...] = jnp.zeros_like(o_ref)

    @pl.when((flags & 8) != 0)
    def _():
        _load_packed_rows(ys_ref, 0, COMB_ENT, d // (2 * LANES), chunk_scr, 0)

    @pl.when((flags & 1) != 0)
    def _():
        gpos = ch_ref[p] * COMB_ENT + lax.broadcasted_iota(I32, (COMB_ROWS, COMB_ENT), 1)
        st = start_ref[...]
        onehot = _ones((gpos >= st) & (gpos < st + cnt_ref[...]), BF16)
        o_ref[...] += _dot(onehot, chunk_scr[...])

    @pl.when((flags & 4) != 0)
    def _():
        m = mod_ref[0]
        out = xm_ref[...] + m[:, 5 * d:6 * d] * o_ref[...]
        if final:
            out = out * lax.rsqrt(jnp.mean(out * out, axis=-1, keepdims=True) + EPS) * gf_ref[...]
        o_ref[...] = out


def combine(ys_sorted, xm, start_col, cnt_col, mod, kb, ch, flags, rows, tiles_per_sample, n_x_tiles,
            n_samples, final_gain):
    d = xm.shape[1]
    nw = d // (2 * LANES)
    n_pairs = kb.shape[0]
    mod_idx = lambda p, kb, ch, fl: (jnp.where(kb[p] < n_x_tiles, kb[p] // tiles_per_sample, n_samples), 0, 0)
    in_specs = [pl.BlockSpec((COMB_ENT * nw, LANES), lambda p, kb, ch, fl: (ch[p], 0)),
                pl.BlockSpec((COMB_ROWS, d), lambda p, kb, ch, fl: (kb[p], 0)),
                pl.BlockSpec((COMB_ROWS, 1), lambda p, kb, ch, fl: (kb[p], 0)),
                pl.BlockSpec((COMB_ROWS, 1), lambda p, kb, ch, fl: (kb[p], 0)),
                pl.BlockSpec((1, 1, N_MOD * d), mod_idx)]
    args = [ys_sorted, xm, start_col, cnt_col, mod]
    if final_gain is not None:
        in_specs.append(pl.BlockSpec((1, d), lambda p, kb, ch, fl: (0, 0)))
        args.append(final_gain)
    return pl.pallas_call(
        functools.partial(_combine_kernel, d=d, final=final_gain is not None),
        name="combine",
        grid_spec=pltpu.PrefetchScalarGridSpec(
            num_scalar_prefetch=3, grid=(n_pairs,),
            in_specs=in_specs,
            out_specs=pl.BlockSpec((COMB_ROWS, d), lambda p, kb, ch, fl: (kb[p], 0)),
            scratch_shapes=[pltpu.VMEM((COMB_ENT, d), BF16)]),
        out_shape=jax.ShapeDtypeStruct((rows, d), F32),
        compiler_params=_params(("arbitrary",)),
    )(kb, ch, flags, *args)


def _combine_schedule(start_row, n_blocks, n_chunks):
    blk_start = start_row[::COMB_ROWS]
    blk_end = jnp.concatenate([blk_start[1:], jnp.full((1,), n_chunks * COMB_ENT, I32)])
    lo = jnp.minimum(blk_start // COMB_ENT, n_chunks - 1)
    hi = jnp.maximum(lo, (blk_end - 1) // COMB_ENT)
    cnt = hi - lo + 1
    off = jnp.cumsum(cnt) - cnt
    total = off[-1] + cnt[-1]
    n_pairs = n_blocks + n_chunks
    pidx = jnp.arange(n_pairs, dtype=I32)
    kb = jnp.sum((off[None, :] <= pidx[:, None]).astype(I32), axis=1) - 1
    valid = pidx < total
    ch = jnp.where(valid, lo[kb] + pidx - off[kb], hi[n_blocks - 1]).astype(I32)
    first = valid & (pidx == off[kb])
    last = valid & (pidx == off[kb] + cnt[kb] - 1)
    fresh = jnp.concatenate([jnp.ones((1,), bool), ch[1:] != ch[:-1]])
    flags = valid.astype(I32) + 2 * first.astype(I32) + 4 * last.astype(I32) + 8 * fresh.astype(I32)
    return kb, ch, flags


def _rope_tables(lx, n_ctx_rows, n_samples):
    n_freq = QK_DIM // 4
    freqs = ROPE_BASE ** (-jnp.arange(n_freq, dtype=F32) / n_freq)
    rows = lx // GRID_W
    row = jnp.repeat(jnp.arange(rows, dtype=F32), GRID_W)
    col = jnp.tile(jnp.arange(GRID_W, dtype=F32), rows)
    ang = jnp.concatenate([row[:, None] * freqs, col[:, None] * freqs], axis=-1)
    cos, sin = jnp.cos(ang), jnp.sin(ang)
    ct = jnp.tile(jnp.concatenate([cos, cos], axis=-1), (n_samples, LANES // QK_DIM))
    st = jnp.tile(jnp.concatenate([-sin, sin], axis=-1), (n_samples, LANES // QK_DIM))
    ct = jnp.concatenate([ct, jnp.ones((n_ctx_rows, LANES), F32)], axis=0)
    st = jnp.concatenate([st, jnp.zeros((n_ctx_rows, LANES), F32)], axis=0)
    return ct, st


def kernel(x, c, ctx, c_ctx, w_ada, b_ada, norm_mix, norm_ffn, w_in, w_out, ret_log_decay, ret_norm,
           diff_lambda, diff_norm, w_router, w_gate, w_up, w_down, norm_final):
    n_b, lx, d = x.shape
    lc = ctx.shape[1]
    depth = w_ada.shape[0]
    heads = ret_log_decay.shape[-1]
    dheads = diff_norm.shape[-1] // V_DIM
    n_e = w_router.shape[-1]
    rx, rc = n_b * lx, n_b * lc
    cap_x = CAPACITY_FACTOR * lx // n_e
    cap_c = CAPACITY_FACTOR * lc // n_e
    tm = 512 if (lx % 512 == 0 and rx % 512 == 0 and rc % 512 == 0) else 256
    tmo = tm
    assert lx % lc == 0 and rc % tmo == 0 and heads % 2 == 0 and n_b < MOD_ROWS

    qk_scale = QK_DIM ** -0.5
    n_rq = heads * QK_DIM // LANES
    rope = tuple([True] * (2 * n_rq) + [False] * (2 * heads) + [True] * (2 * dheads) + [False] * dheads)
    slab_scale = tuple([qk_scale] * n_rq + [1.0] * (n_rq + 2 * heads) + [qk_scale * math.log2(math.e)] * dheads
                       + [1.0] * (2 * dheads))
    w_in_bf = _prep_w_in(w_in, slab_scale)

    xa = jnp.concatenate([x.reshape(rx, d), ctx.reshape(rc, d)], axis=0)
    cc = jnp.concatenate([c, c_ctx[None], jnp.zeros((MOD_ROWS - n_b - 1, d), F32)], axis=0)
    mods = ada_modulation(cc, w_ada, b_ada)
    ct, st = _rope_tables(lx, rc, n_b)
    log_g = jnp.log1p(-jnp.exp(ret_log_decay.astype(F32)))
    out = None

    for layer in range(depth):
        last = layer == depth - 1
        lam_init = 0.8 - 0.6 * math.exp(-0.3 * layer)
        mod = mods[layer].reshape(MOD_ROWS, 1, N_MOD * d)
        p = in_projection(xa, mod, norm_mix[layer][None], w_in_bf, layer, ct, st,
                          rx // tm, lx // tm, n_b, rope, tm)

        ret = retention(p, log_g[layer], ret_norm[layer][None], n_b, lx, lc, heads, not last)
        dif_x = diff_attention(p, diff_lambda[layer], diff_norm[layer][None], n_b, lx, lc, heads, dheads,
                               lam_init, True, 1024 if lx % 1024 == 0 else 256)
        if last:
            rows = rx
            mix_c = None
        else:
            rows = rx + rc
            dif_c = diff_attention(p, diff_lambda[layer], diff_norm[layer][None], n_b, lx, lc, heads, dheads,
                                   lam_init, False, lc)
            mix_c = (ret[1], dif_c)

        xm, h2, aff = out_projection((ret[0], dif_x), mix_c, w_out[layer].astype(BF16), xa, mod,
                                     norm_ffn[layer][None], w_router[layer], rows, rx // tmo, lx // tmo, n_b, tmo)
        aff_t = aff.T

        sel = [expert_choice(aff_t, n_b, lx, 0, cap_x, 0, 0)]
        if not last:
            sel.append(expert_choice(aff_t, n_b, lc, rx // lc, cap_c, rx, n_b * n_e * cap_x))
        to_es = lambda t: jnp.transpose(t, (2, 0, 1)).reshape(n_e, -1)
        idx = jnp.concatenate([to_es(s[0]) for s in sel], axis=1)
        gates = jnp.concatenate([to_es(s[1]) for s in sel], axis=1)
        dest = jnp.concatenate([to_es(s[2]) for s in sel], axis=1)
        start_row = jnp.concatenate([s[3][0] for s in sel])
        cnt_row = jnp.concatenate([s[4][0] for s in sel])
        slots = idx.shape[1]
        n_ent = n_e * slots

        ys_sorted = expert_ffn(h2, idx.reshape(-1), dest.reshape(-1), w_gate, w_up, w_down, layer,
                               gates.reshape(n_e, slots, 1), d)

        kb, ch, flags = _combine_schedule(start_row, rows // COMB_ROWS, n_ent // COMB_ENT)
        xa_new = combine(ys_sorted, xm, start_row[:, None], cnt_row[:, None], mod, kb, ch, flags, rows,
                         lx // COMB_ROWS, rx // COMB_ROWS, n_b, norm_final[None] if last else None)
        if last:
            out = xa_new
        else:
            xa = xa_new
    return out.reshape(n_b, lx, d)
```

```python
import functools
import math

import jax
import jax.numpy as jnp
from jax import lax
from jax.experimental import pallas as pl
from jax.experimental.pallas import tpu as pltpu

F32 = jnp.float32
BF16 = jnp.bfloat16
I32 = jnp.int32

EPS = 1e-6
N_MOD = 6
GRID_W = 64
ROPE_BASE = 10000.0
QK_DIM = 64
V_DIM = 128
LANES = 128
CHUNK = 128
CAPACITY_FACTOR = 2
MOD_ROWS = 8
VMEM_LIMIT = 56 * 1024 * 1024


def _params(sem, vmem=VMEM_LIMIT):
    return pltpu.CompilerParams(dimension_semantics=sem, vmem_limit_bytes=vmem)


def _silu(x):
    return x / (1.0 + jnp.exp(-x))


def _dot(a, b):
    return jnp.dot(a, b, preferred_element_type=F32)


def _dot_nt(a, b):
    return lax.dot_general(a, b, (((1,), (1,)), ((), ())), preferred_element_type=F32)


def _ada_kernel(c_ref, w_ref, b_ref, o_ref):
    s = _silu(c_ref[...]).astype(BF16)
    o_ref[0] = _dot(s, w_ref[0].astype(BF16)) + b_ref[0]


def ada_modulation(cc, w_ada, b_ada):
    depth, d, n = w_ada.shape
    tn = 1024 if n % 1024 == 0 else n
    return pl.pallas_call(
        _ada_kernel,
        name="ada_mod",
        grid=(depth, n // tn),
        in_specs=[pl.BlockSpec((MOD_ROWS, d), lambda l, j: (0, 0)),
                  pl.BlockSpec((1, d, tn), lambda l, j: (l, 0, j)),
                  pl.BlockSpec((1, 1, tn), lambda l, j: (l, 0, j))],
        out_specs=pl.BlockSpec((1, MOD_ROWS, tn), lambda l, j: (l, 0, j)),
        out_shape=jax.ShapeDtypeStruct((depth, MOD_ROWS, n), F32),
        compiler_params=_params(("arbitrary", "arbitrary")),
    )(cc, w_ada, b_ada.reshape(depth, 1, n))


def _rmsnorm_mod(x, gain, shift, scale):
    y = x * lax.rsqrt(jnp.mean(x * x, axis=-1, keepdims=True) + EPS) * gain
    return y * (1.0 + scale) + shift


MXU_N = 256


def _group_of(lane):
    return (lane >> 6) & 1


def _swap_halves(a):
    lane = lax.broadcasted_iota(I32, a.shape, 1)
    return jnp.where((lane & 32) == 0, pltpu.roll(a, 96, 1), pltpu.roll(a, 32, 1))


def _inproj_kernel(x_ref, mod_ref, g_ref, w_ref, ct_ref, st_ref, o_ref, *, d, rope):
    m = mod_ref[0]
    h = _rmsnorm_mod(x_ref[...], g_ref[...], m[:, 0:d], m[:, d:2 * d]).astype(BF16)
    per = MXU_N // LANES
    for s in range(w_ref.shape[2] // MXU_N):
        acc = _dot(h, w_ref[0, :, s * MXU_N:(s + 1) * MXU_N])
        for t in range(per):
            a = acc[:, t * LANES:(t + 1) * LANES]
            if rope[s * per + t]:
                a = a * ct_ref[...] + _swap_halves(a) * st_ref[...]
            c0 = s * MXU_N + t * LANES
            o_ref[:, c0:c0 + LANES] = a.astype(BF16)


def in_projection(xa, mod, gain, w_bf, layer, ct, st, n_x_tiles, tiles_per_sample, n_samples, rope, tm):
    r, d = xa.shape
    n = w_bf.shape[2]
    mod_idx = lambda i: (jnp.where(i < n_x_tiles, i // tiles_per_sample, n_samples), 0, 0)
    return pl.pallas_call(
        functools.partial(_inproj_kernel, d=d, rope=rope),
        name="in_proj",
        grid=(r // tm,),
        in_specs=[pl.BlockSpec((tm, d), lambda i: (i, 0)),
                  pl.BlockSpec((1, 1, N_MOD * d), mod_idx),
                  pl.BlockSpec((1, d), lambda i: (0, 0)),
                  pl.BlockSpec((1, d, n), lambda i: (layer, 0, 0), pipeline_mode=pl.Buffered(1)),
                  pl.BlockSpec((tm, LANES), lambda i: (i, 0)),
                  pl.BlockSpec((tm, LANES), lambda i: (i, 0))],
        out_specs=pl.BlockSpec((tm, n), lambda i: (i, 0)),
        out_shape=jax.ShapeDtypeStruct((r, n), BF16),
        compiler_params=_params(("arbitrary",)),
    )(xa, mod, gain, w_bf, ct, st)


def _prep_w_in(w_in, slab_scale):
    col_scale = jnp.repeat(jnp.asarray(slab_scale, F32), LANES)
    return (w_in * col_scale).astype(BF16)


def _ret_kernel(lg_ref, qx_ref, kx_ref, vx_ref, gx_ref, qc_ref, kc_ref, vc_ref, gc_ref, gain_ref,
                *rest, lx, lc, ctx_out):
    if ctx_out:
        ox_ref, oc_ref, accx, accc = rest
    else:
        ox_ref, accx = rest
    pair = pl.program_id(1)
    w2 = 2 * V_DIM
    lane = lax.broadcasted_iota(I32, (CHUNK, LANES), 1)
    dif = (lax.broadcasted_iota(I32, (CHUNK, CHUNK), 0) - lax.broadcasted_iota(I32, (CHUNK, CHUNK), 1)).astype(F32)
    pos = lax.broadcasted_iota(I32, (CHUNK, 1), 0).astype(F32)
    cpos = lax.broadcasted_iota(I32, (lc, 1), 0).astype(F32)
    c_len = float(CHUNK)
    lgf = [lg_ref[0, 2 * pair + hh] for hh in range(2)]
    lgb = [lg_ref[1, 2 * pair + hh] for hh in range(2)]

    def per_head(fn):
        vals = [fn(hh) for hh in range(2)]
        return jnp.concatenate([jnp.broadcast_to(v, (v.shape[0], V_DIM)) for v in vals], axis=1)

    decay = per_head(lambda hh: jnp.where(dif >= 0, jnp.exp(lgf[hh] * jnp.maximum(dif, 0.0)),
                                          jnp.exp(lgb[hh] * jnp.maximum(-dif, 0.0))))
    qd_f = per_head(lambda hh: jnp.exp(lgf[hh] * (pos + 1.0)))
    kd_f = per_head(lambda hh: jnp.exp(lgf[hh] * (c_len - 1.0 - pos)))
    qd_b = per_head(lambda hh: jnp.exp(lgb[hh] * (c_len - pos)))
    kd_b = per_head(lambda hh: jnp.exp(lgb[hh] * pos))
    one = jnp.ones((1, 1), F32)
    cd_f = per_head(lambda hh: jnp.exp(lgf[hh] * c_len) * one)
    cd_b = per_head(lambda hh: jnp.exp(lgb[hh] * c_len) * one)
    srow = lax.broadcasted_iota(I32, (LANES, w2), 0)
    scol = lax.broadcasted_iota(I32, (LANES, w2), 1)
    diag = (_group_of(srow) == 0) == (scol < V_DIM)
    zeros_v = jnp.zeros((CHUNK, V_DIM), BF16)

    def kv_state(k, v_scaled):
        return jnp.where(diag, _dot(k.astype(F32).T.astype(BF16), v_scaled.astype(BF16)), 0.0)

    kc = kc_ref[...]
    vc = vc_ref[...].astype(F32)
    s_f = kv_state(kc, vc * per_head(lambda hh: jnp.exp(lgf[hh] * (lc - 1.0 - cpos))))
    s_b = kv_state(kc, vc * per_head(lambda hh: jnp.exp(lgb[hh] * cpos)))

    def sweeps(q_ref, k_ref, v_ref, g_ref, o_ref, acc_ref, n, s_f0, s_b0):
        def rows_of(c):
            return pl.ds(pl.multiple_of(c * CHUNK, CHUNK), CHUNK)

        def forward(c, s):
            rows = rows_of(c)
            q, k, v = q_ref[rows, :], k_ref[rows, :], v_ref[rows, :]
            k_bd = jnp.concatenate([jnp.where(_group_of(lane) == 0, k, 0), jnp.where(_group_of(lane) == 1, k, 0)],
                                   axis=0)
            v_bd = jnp.concatenate([jnp.concatenate([v[:, :V_DIM], zeros_v], axis=1),
                                    jnp.concatenate([zeros_v, v[:, V_DIM:]], axis=1)], axis=0)
            sc = _dot_nt(q, k_bd) * decay
            acc_ref[rows, :] = _dot(sc.astype(BF16), v_bd) + _dot(q, s.astype(BF16)) * qd_f
            return s * cd_f + kv_state(k, v.astype(F32) * kd_f)

        def backward(t, s):
            rows = rows_of(n - 1 - t)
            q, k, v = q_ref[rows, :], k_ref[rows, :], v_ref[rows, :]
            y = acc_ref[rows, :] + _dot(q, s.astype(BF16)) * qd_b
            for hh in range(2):
                vsl = slice(hh * V_DIM, (hh + 1) * V_DIM)
                yh = y[:, vsl]
                yc = yh - jnp.mean(yh, axis=-1, keepdims=True)
                out = yc * lax.rsqrt(jnp.mean(yc * yc, axis=-1, keepdims=True) + EPS) * gain_ref[:, vsl]
                o_ref[rows, vsl] = (_silu(g_ref[rows, vsl].astype(F32)) * out).astype(BF16)
            return s * cd_b + kv_state(k, v.astype(F32) * kd_b)

        unroll = 4 if n % 4 == 0 else n
        lax.fori_loop(0, n, forward, s_f0, unroll=unroll)
        lax.fori_loop(0, n, backward, s_b0, unroll=unroll)

    sweeps(qx_ref, kx_ref, vx_ref, gx_ref, ox_ref, accx, lx // CHUNK, s_f, s_b)
    if ctx_out:
        zero = jnp.zeros((LANES, w2), F32)
        sweeps(qc_ref, kc_ref, vc_ref, gc_ref, oc_ref, accc, lc // CHUNK, zero, zero)


def retention(p, lg, gain, n_samples, lx, lc, heads, ctx_out):
    pairs = heads // 2
    half = heads // 2
    cb = n_samples * lx // lc
    w2 = 2 * V_DIM
    in_specs = [
        pl.BlockSpec(memory_space=pltpu.SMEM),
        pl.BlockSpec((lx, LANES), lambda b, h: (b, h)),
        pl.BlockSpec((lx, LANES), lambda b, h: (b, half + h)),
        pl.BlockSpec((lx, w2), lambda b, h: (b, half + h)),
        pl.BlockSpec((lx, w2), lambda b, h: (b, heads + h)),
        pl.BlockSpec((lc, LANES), lambda b, h: (cb + b, h)),
        pl.BlockSpec((lc, LANES), lambda b, h: (cb + b, half + h)),
        pl.BlockSpec((lc, w2), lambda b, h: (cb + b, half + h)),
        pl.BlockSpec((lc, w2), lambda b, h: (cb + b, heads + h)),
        pl.BlockSpec((1, w2), lambda b, h: (0, h)),
    ]
    out_specs = [pl.BlockSpec((lx, w2), lambda b, h: (b, h))]
    out_shape = [jax.ShapeDtypeStruct((n_samples * lx, heads * V_DIM), BF16)]
    scratch = [pltpu.VMEM((lx, w2), F32)]
    if ctx_out:
        out_specs.append(pl.BlockSpec((lc, w2), lambda b, h: (b, h)))
        out_shape.append(jax.ShapeDtypeStruct((n_samples * lc, heads * V_DIM), BF16))
        scratch.append(pltpu.VMEM((lc, w2), F32))
    return pl.pallas_call(
        functools.partial(_ret_kernel, lx=lx, lc=lc, ctx_out=ctx_out),
        name="retention",
        grid=(n_samples, pairs),
        in_specs=in_specs, out_specs=out_specs, out_shape=out_shape, scratch_shapes=scratch,
        compiler_params=_params(("arbitrary", "arbitrary")),
    )(lg, p, p, p, p, p, p, p, p, gain)


def _diff_kernel(lam_ref, q_ref, kc_ref, vc_ref, *rest, lam_init, has_x, kb):
    if has_x:
        kx_ref, vx_ref, gain_ref, o_ref = rest
    else:
        gain_ref, o_ref = rest
    lv = lam_ref[...]
    lam = (jnp.exp(jnp.sum(lv[0:1] * lv[1:2], axis=-1, keepdims=True))
           - jnp.exp(jnp.sum(lv[2:3] * lv[3:4], axis=-1, keepdims=True)) + lam_init)
    q = q_ref[...]
    tq = q.shape[0]
    lane = lax.broadcasted_iota(I32, q.shape, 1)
    blocks = [(kc_ref, vc_ref, 0, kc_ref.shape[0])]
    if has_x:
        blocks += [(kx_ref, vx_ref, r0, kb) for r0 in range(0, kx_ref.shape[0], kb)]
    v_ext = [jnp.concatenate([v_ref[r0:r0 + n, :], jnp.ones((n, V_DIM), BF16)], axis=1) for _, v_ref, r0, n in blocks]
    heads_out = []
    for t in range(2):
        qm = jnp.where(_group_of(lane) == t, q, 0)
        m = jnp.full((tq, 1), -jnp.inf, F32)
        acc = jnp.zeros((tq, 2 * V_DIM), F32)
        for (k_ref, _, r0, n), ve in zip(blocks, v_ext):
            s = _dot_nt(qm, k_ref[r0:r0 + n, :])
            m_new = jnp.maximum(m, jnp.max(s, axis=-1, keepdims=True))
            acc = acc * jnp.exp2(m - m_new) + _dot(jnp.exp2(s - m_new).astype(BF16), ve)
            m = m_new
        heads_out.append(acc[:, :V_DIM] * (1.0 / acc[:, V_DIM:V_DIM + 1]))
    out = heads_out[0] - lam * heads_out[1]
    y = out * lax.rsqrt(jnp.mean(out * out, axis=-1, keepdims=True) + EPS) * gain_ref[...]
    o_ref[...] = (y * (1.0 - lam_init)).astype(BF16)


def diff_attention(p, lam_vec, gain, n_samples, lx, lc, ret_heads, heads, lam_init, latent, tq):
    qo = 3 * ret_heads
    ko = qo + heads
    vo = ko + heads
    cb = n_samples * lx // lc
    lq = lx if latent else lc
    nq = lq // tq
    q_row = (lambda b, t: b * nq + t) if latent else (lambda b, t: n_samples * lx // tq + b * nq + t)
    in_specs = [
        pl.BlockSpec((4, QK_DIM), lambda b, h, t: (0, 0)),
        pl.BlockSpec((tq, LANES), lambda b, h, t: (q_row(b, t), qo + h)),
        pl.BlockSpec((lc, LANES), lambda b, h, t: (cb + b, ko + h)),
        pl.BlockSpec((lc, LANES), lambda b, h, t: (cb + b, vo + h)),
    ]
    args = [lam_vec, p, p, p]
    if latent:
        in_specs += [pl.BlockSpec((lx, LANES), lambda b, h, t: (b, ko + h)),
                     pl.BlockSpec((lx, LANES), lambda b, h, t: (b, vo + h))]
        args += [p, p]
    in_specs.append(pl.BlockSpec((1, LANES), lambda b, h, t: (0, h)))
    args.append(gain)
    return pl.pallas_call(
        functools.partial(_diff_kernel, lam_init=lam_init, has_x=latent, kb=min(1024, lx)),
        name="diff_attn_x" if latent else "diff_attn_ctx",
        grid=(n_samples, heads, nq),
        in_specs=in_specs,
        out_specs=pl.BlockSpec((tq, LANES), lambda b, h, t: (b * nq + t, h)),
        out_shape=jax.ShapeDtypeStruct((n_samples * lq, heads * V_DIM), BF16),
        compiler_params=_params(("arbitrary", "arbitrary", "arbitrary")),
    )(*args)


def _pack_pair(a, b):
    ua = pltpu.bitcast(a.astype(BF16).astype(F32), I32)
    ub = pltpu.bitcast(b.astype(BF16).astype(F32), I32)
    return ua | lax.shift_right_logical(ub, 16)


def _unpack_pair(w):
    return (pltpu.bitcast(w & -65536, F32).astype(BF16), pltpu.bitcast(w << 16, F32).astype(BF16))


def _store_packed_rows(ref, row0, n_rows, val):
    nw = val.shape[1] // (2 * LANES)
    for j in range(nw):
        w = _pack_pair(val[:, j * LANES:(j + 1) * LANES], val[:, (nw + j) * LANES:(nw + j + 1) * LANES])
        ref[pl.ds(row0 * nw + j, n_rows, stride=nw), :] = w


def _load_packed_rows(ref, row0, n_rows, nw, dst, dst_row0):
    for j in range(nw):
        hi, lo = _unpack_pair(ref[pl.ds(row0 * nw + j, n_rows, stride=nw), :])
        dst[dst_row0:dst_row0 + n_rows, j * LANES:(j + 1) * LANES] = hi
        dst[dst_row0:dst_row0 + n_rows, (nw + j) * LANES:(nw + j + 1) * LANES] = lo


def _outproj_kernel(*refs, d, wr, n_x_tiles, with_ctx):
    if with_ctx:
        retx_ref, difx_ref, retc_ref, difc_ref = refs[:4]
        is_ctx = pl.program_id(0) >= n_x_tiles
        ret = jnp.where(is_ctx, retc_ref[...], retx_ref[...])
        dif = jnp.where(is_ctx, difc_ref[...], difx_ref[...])
        refs = refs[4:]
    else:
        ret, dif = refs[0][...], refs[1][...]
        refs = refs[2:]
    w_ref, x_ref, mod_ref, g_ref, wr_ref, xm_ref, h_ref, aff_ref = refs
    y = _dot(ret, w_ref[0:wr, :]) + _dot(dif, w_ref[wr:, :])
    m = mod_ref[0]
    xm = x_ref[...] + m[:, 2 * d:3 * d] * y
    xm_ref[...] = xm
    h = _rmsnorm_mod(xm, g_ref[...], m[:, 3 * d:4 * d], m[:, 4 * d:5 * d])
    _store_packed_rows(h_ref, 0, h.shape[0], h)
    n_e = wr_ref.shape[1] // 2
    h_hi = h.astype(BF16)
    h_lo = (h - h_hi.astype(F32)).astype(BF16)
    both = _dot(h_hi, wr_ref[...])
    logits = both[:, :n_e] + both[:, n_e:] + _dot(h_lo, wr_ref[...])[:, :n_e]
    e = jnp.exp(logits - jnp.max(logits, axis=-1, keepdims=True))
    aff_ref[...] = e / jnp.sum(e, axis=-1, keepdims=True)


def out_projection(mix_x, mix_c, w_bf, xa, mod, gain, w_router, rows, n_x_tiles, tiles_per_sample, n_samples, tm):
    d = xa.shape[1]
    wr = mix_x[0].shape[1]
    wd = mix_x[1].shape[1]
    e = w_router.shape[1]
    nw = d // (2 * LANES)
    wr_hi = w_router.astype(BF16)
    wr_split = jnp.concatenate([wr_hi, (w_router - wr_hi.astype(F32)).astype(BF16)], axis=1)
    mod_idx = lambda i: (jnp.where(i < n_x_tiles, i // tiles_per_sample, n_samples), 0, 0)
    x_idx = lambda i: (jnp.minimum(i, n_x_tiles - 1), 0)
    c_idx = lambda i: (jnp.maximum(i - n_x_tiles, 0), 0)
    mix_specs = [pl.BlockSpec((tm, wr), x_idx), pl.BlockSpec((tm, wd), x_idx)]
    mix_args = list(mix_x)
    if mix_c is not None:
        mix_specs += [pl.BlockSpec((tm, wr), c_idx), pl.BlockSpec((tm, wd), c_idx)]
        mix_args += list(mix_c)
    return pl.pallas_call(
        functools.partial(_outproj_kernel, d=d, wr=wr, n_x_tiles=n_x_tiles, with_ctx=mix_c is not None),
        name="out_proj_router",
        grid=(rows // tm,),
        in_specs=mix_specs + [
                  pl.BlockSpec(w_bf.shape, lambda i: (0, 0), pipeline_mode=pl.Buffered(1)),
                  pl.BlockSpec((tm, d), lambda i: (i, 0)),
                  pl.BlockSpec((1, 1, N_MOD * d), mod_idx),
                  pl.BlockSpec((1, d), lambda i: (0, 0)),
                  pl.BlockSpec((d, 2 * e), lambda i: (0, 0))],
        out_specs=[pl.BlockSpec((tm, d), lambda i: (i, 0)),
                   pl.BlockSpec((tm * nw, LANES), lambda i: (i, 0)),
                   pl.BlockSpec((tm, e), lambda i: (i, 0))],
        out_shape=[jax.ShapeDtypeStruct((rows, d), F32),
                   jax.ShapeDtypeStruct((rows * nw, LANES), I32),
                   jax.ShapeDtypeStruct((rows, e), F32)],
        compiler_params=_params(("arbitrary",)),
    )(*mix_args, w_bf, xa, mod, gain, wr_split)


def _ones(mask, dtype):
    return jnp.where(mask, 1.0, 0.0).astype(dtype)


def _prefix_lanes(m):
    upper = _ones(lax.broadcasted_iota(I32, (LANES, LANES), 0) < lax.broadcasted_iota(I32, (LANES, LANES), 1), BF16)
    run = jnp.zeros((m.shape[0], 1), F32)
    outs = []
    for s in range(m.shape[1] // LANES):
        seg = m[:, s * LANES:(s + 1) * LANES]
        outs.append(_dot(seg, upper) + run)
        run = run + jnp.sum(seg.astype(F32), axis=1, keepdims=True)
    return jnp.concatenate(outs, axis=1)


DIRECT_BLOCKS = 8


def _topk_kernel(a_ref, idx_ref, gate_ref, dest_ref, start_ref, cnt_ref, *scr, lg, cap, row_base, ent_base):
    g = pl.program_id(0)
    a = a_ref[...]
    n_e = a.shape[0]
    bits = pltpu.bitcast(a, I32)

    def bisect(_, lohi):
        lo, hi = lohi
        mid = lo + ((hi - lo) >> 1)
        ge = jnp.sum(_ones(bits >= mid, F32), axis=1, keepdims=True) >= cap
        return jnp.where(ge, mid, lo), jnp.where(ge, hi, mid)

    thr, _ = lax.fori_loop(0, 31, bisect,
                           (jnp.zeros((n_e, 1), I32), jnp.full((n_e, 1), 0x7F800000, I32)))
    gt = bits > thr
    eq = bits == thr
    need = cap - jnp.sum(_ones(gt, F32), axis=1, keepdims=True)
    sel = gt | (eq & (_prefix_lanes(_ones(eq, BF16)) < need))
    selb = _ones(sel, BF16)
    pos = _prefix_lanes(selb)
    start_t = jnp.sum(pos, axis=0, keepdims=True)
    lower = _ones(lax.broadcasted_iota(I32, (n_e, n_e), 0) > lax.broadcasted_iota(I32, (n_e, n_e), 1), BF16)
    base = (ent_base + g * (n_e * cap)).astype(F32)
    start_ref[...] = (start_t + base).astype(I32)
    cnt_ref[...] = jnp.sum(_ones(sel, F32), axis=0, keepdims=True).astype(I32)
    posm = jnp.where(sel, pos, -1.0)
    dest = start_t + base + _dot(lower, selb)
    tok_base = (row_base + g * lg).astype(F32)
    slot = lax.broadcasted_iota(I32, (cap, 1), 0).astype(F32)
    col = lax.broadcasted_iota(I32, (cap, n_e), 1)
    z = jnp.zeros((cap, n_e), F32)

    def emit(ia, ga, da):
        idx_ref[0] = ia.astype(I32)
        gate_ref[0] = ga
        dest_ref[0] = da.astype(I32)

    nb = lg // LANES
    if nb <= DIRECT_BLOCKS:
        (row_scr,) = scr
        row_scr[0], row_scr[1], row_scr[2] = posm, a, dest
        tvals = tok_base + lax.broadcasted_iota(I32, (1, lg), 1).astype(F32)

        def compact_direct(e, carry):
            ia, ga, da = carry
            m = row_scr[0, pl.ds(e, 1), :] == slot
            i_e = jnp.sum(jnp.where(m, tvals, 0.0), axis=1, keepdims=True)
            g_e = jnp.sum(jnp.where(m, row_scr[1, pl.ds(e, 1), :], 0.0), axis=1, keepdims=True)
            d_e = jnp.sum(jnp.where(m, row_scr[2, pl.ds(e, 1), :], 0.0), axis=1, keepdims=True)
            here = col == e
            return jnp.where(here, i_e, ia), jnp.where(here, g_e, ga), jnp.where(here, d_e, da)

        emit(*lax.fori_loop(0, n_e, compact_direct, (z, z, z)))
        return

    blk_scr, first_scr = scr
    nbp = blk_scr.shape[1] // n_e
    if nbp != nb:
        blk_scr[...] = jnp.zeros_like(blk_scr)
    for k in range(nb):
        blk = slice(k * LANES, (k + 1) * LANES)
        pieces = []
        for x in (posm[:, blk], a[:, blk], dest[:, blk]):
            x1 = x.astype(BF16).astype(F32)
            x2 = (x - x1).astype(BF16).astype(F32)
            pieces += [x1, x2, (x - x1 - x2).astype(BF16).astype(F32)]
        for j in range(9):
            blk_scr[j, pl.ds(k, n_e, stride=nbp), :] = pieces[j]
    before = _ones(lax.broadcasted_iota(I32, (lg, nb), 0) < LANES * lax.broadcasted_iota(I32, (lg, nb), 1), BF16)
    first_scr[...] = _dot(selb, before)

    blane = lax.broadcasted_iota(I32, (cap, nbp), 1).astype(F32)
    lane_f = lax.broadcasted_iota(I32, (cap, LANES), 1).astype(F32)
    w3 = 3 * LANES

    def compact(e, carry):
        ia, ga, da = carry
        kb = jnp.sum(_ones(first_scr[pl.ds(e, 1), :] <= slot, F32), axis=1, keepdims=True) - 1.0
        onehot = _ones(blane == kb, BF16)
        own = pl.ds(pl.multiple_of(e * nbp, nbp), nbp)
        rows = jnp.concatenate([blk_scr[j, own, :] for j in range(9)], axis=1).astype(BF16)
        parts = _dot(onehot, rows)
        pos_g, aff_g, dest_g = [parts[:, q * w3:q * w3 + LANES] + parts[:, q * w3 + LANES:q * w3 + 2 * LANES]
                                + parts[:, q * w3 + 2 * LANES:(q + 1) * w3] for q in range(3)]
        m = pos_g == slot
        i_e = jnp.sum(jnp.where(m, lane_f, 0.0), axis=1, keepdims=True) + kb * LANES + tok_base
        g_e = jnp.sum(jnp.where(m, aff_g, 0.0), axis=1, keepdims=True)
        d_e = jnp.sum(jnp.where(m, dest_g, 0.0), axis=1, keepdims=True)
        here = col == e
        return jnp.where(here, i_e, ia), jnp.where(here, g_e, ga), jnp.where(here, d_e, da)

    emit(*lax.fori_loop(0, n_e, compact, (z, z, z)))


def expert_choice(aff_t, groups, lg, first_block, cap, row_base, ent_base):
    e = aff_t.shape[0]
    return pl.pallas_call(
        functools.partial(_topk_kernel, lg=lg, cap=cap, row_base=row_base, ent_base=ent_base),
        name="expert_choice",
        grid=(groups,),
        in_specs=[pl.BlockSpec((e, lg), lambda g: (0, first_block + g))],
        out_specs=[pl.BlockSpec((1, cap, e), lambda g: (g, 0, 0)),
                   pl.BlockSpec((1, cap, e), lambda g: (g, 0, 0)),
                   pl.BlockSpec((1, cap, e), lambda g: (g, 0, 0)),
                   pl.BlockSpec((1, lg), lambda g: (0, g)),
                   pl.BlockSpec((1, lg), lambda g: (0, g))],
        out_shape=[jax.ShapeDtypeStruct((groups, cap, e), I32),
                   jax.ShapeDtypeStruct((groups, cap, e), F32),
                   jax.ShapeDtypeStruct((groups, cap, e), I32),
                   jax.ShapeDtypeStruct((1, groups * lg), I32),
                   jax.ShapeDtypeStruct((1, groups * lg), I32)],
        scratch_shapes=([pltpu.VMEM((3, e, lg), F32)] if lg // LANES <= DIRECT_BLOCKS else
                        [pltpu.VMEM((9, e * (-(-(lg // LANES) // 8) * 8), LANES), F32),
                         pltpu.VMEM((e, lg // LANES), F32)]),
        compiler_params=_params(("arbitrary",)),
    )(aff_t)


def _ffn_kernel(idx_ref, dest_ref, h_ref, wg_ref, wu_ref, wd_ref, gate_ref, ys_ref,
                xs_buf, ys_buf, xb_scr, hid_scr, wd_scr, gsem, ssem, *, ts, nw, tf):
    e = pl.program_id(0)
    f = pl.program_id(1)
    n_e = pl.num_programs(0)
    last_f = pl.num_programs(1) - 1
    s = xb_scr.shape[0]

    n_f = hid_scr.shape[0]
    per_step = s // n_f
    slot = e % 2

    def gather_start(expert, into, row):
        src = pl.multiple_of(idx_ref[expert * s + row] * nw, nw)
        pltpu.make_async_copy(h_ref.at[pl.ds(src, nw)],
                              xs_buf.at[into, pl.ds(pl.multiple_of(row * nw, nw), nw)], gsem.at[into]).start()

    def wait_gather(into):
        pltpu.make_async_copy(h_ref.at[pl.ds(0, s * nw)], xs_buf.at[into], gsem.at[into]).wait()

    def scatter_start(row):
        dst = pl.multiple_of(dest_ref[e * s + row] * nw, nw)
        pltpu.make_async_copy(ys_buf.at[pl.ds(row * nw, nw)], ys_ref.at[pl.ds(dst, nw)], ssem).start()

    def wait_scatter():
        pltpu.make_async_copy(ys_buf, ys_ref.at[pl.ds(0, s * nw)], ssem).wait()

    @pl.when(f == 0)
    def _():
        @pl.when(e == 0)
        def _():
            def body(i, carry):
                gather_start(0, 0, i)
                return carry
            lax.fori_loop(0, s, body, 0, unroll=8)

        wait_gather(slot)
        _load_packed_rows(xs_buf.at[slot], 0, s, nw, xb_scr, 0)

    nxt = jnp.minimum(e + 1, n_e - 1)
    for i in range(per_step):
        gather_start(nxt, 1 - slot, f * per_step + i)

    wg = wg_ref[0, 0].astype(BF16)
    wu = wu_ref[0, 0].astype(BF16)
    for r in range(s // ts):
        rows = slice(r * ts, (r + 1) * ts)
        xb = xb_scr[rows, :]
        hid_scr[f, rows, :] = (_silu(_dot(xb, wg)) * _dot(xb, wu)).astype(BF16)
    wd_scr[pl.ds(pl.multiple_of(f * tf, tf), tf), :] = wd_ref[0, 0].astype(BF16)

    @pl.when(f == last_f)
    def _():
        @pl.when(e > 0)
        def _():
            wait_scatter()

        for r in range(s // ts):
            rows = slice(r * ts, (r + 1) * ts)
            hid = jnp.concatenate([hid_scr[c, rows, :] for c in range(n_f)], axis=1)
            y = _dot(hid, wd_scr[...])
            _store_packed_rows(ys_buf, r * ts, ts, y * gate_ref[0, rows, :])
            for i in range(ts):
                scatter_start(r * ts + i)

        @pl.when(e == n_e - 1)
        def _():
            wait_scatter()
            wait_gather(1 - slot)


def expert_ffn(h_packed, idx, dest, w_gate, w_up, w_down, layer, gates, d):
    nw = d // (2 * LANES)
    e, s = gates.shape[0], gates.shape[1]
    ff = w_gate.shape[3]
    tf = 256 if ff % 256 == 0 else ff
    ts = max(t for t in range(8, 577, 8) if s % t == 0)
    assert h_packed.shape[0] >= s * nw and s % (ff // tf) == 0
    return pl.pallas_call(
        functools.partial(_ffn_kernel, ts=ts, nw=nw, tf=tf),
        name="expert_ffn",
        grid_spec=pltpu.PrefetchScalarGridSpec(
            num_scalar_prefetch=2, grid=(e, ff // tf),
            in_specs=[pl.BlockSpec(memory_space=pl.ANY),
                      pl.BlockSpec((1, 1, d, tf), lambda i, f, si, di: (layer, i, 0, f)),
                      pl.BlockSpec((1, 1, d, tf), lambda i, f, si, di: (layer, i, 0, f)),
                      pl.BlockSpec((1, 1, tf, d), lambda i, f, si, di: (layer, i, f, 0)),
                      pl.BlockSpec((1, s, 1), lambda i, f, si, di: (i, 0, 0))],
            out_specs=pl.BlockSpec(memory_space=pl.ANY),
            scratch_shapes=[pltpu.VMEM((2, s * nw, LANES), I32), pltpu.VMEM((s * nw, LANES), I32),
                            pltpu.VMEM((s, d), BF16), pltpu.VMEM((ff // tf, s, tf), BF16),
                            pltpu.VMEM((ff, d), BF16),
                            pltpu.SemaphoreType.DMA((2,)), pltpu.SemaphoreType.DMA(())]),
        out_shape=jax.ShapeDtypeStruct((e * s * nw, LANES), I32),
        compiler_params=_params(("arbitrary", "arbitrary")),
    )(idx, dest, h_packed, w_gate, w_up, w_down, gates)


COMB_ROWS = 256
COMB_ENT = 256


def _combine_kernel(kb_ref, ch_ref, flag_ref, ys_ref, xm_ref, start_ref, cnt_ref, mod_ref, *rest, d, final):
    if final:
        gf_ref, o_ref, chunk_scr = rest
    else:
        o_ref, chunk_scr = rest
    p = pl.program_id(0)
    flags = flag_ref[p]

    @pl.when((flags & 2) != 0)
    def _():
        o_ref[...] = jnp.zeros_like(o_ref)

    @pl.when((flags & 8) != 0)
    def _():
        _load_packed_rows(ys_ref, 0, COMB_ENT, d // (2 * LANES), chunk_scr, 0)

    @pl.when((flags & 1) != 0)
    def _():
        gpos = ch_ref[p] * COMB_ENT + lax.broadcasted_iota(I32, (COMB_ROWS, COMB_ENT), 1)
        st = start_ref[...]
        onehot = _ones((gpos >= st) & (gpos < st + cnt_ref[...]), BF16)
        o_ref[...] += _dot(onehot, chunk_scr[...])

    @pl.when((flags & 4) != 0)
    def _():
        m = mod_ref[0]
        out = xm_ref[...] + m[:, 5 * d:6 * d] * o_ref[...]
        if final:
            out = out * lax.rsqrt(jnp.mean(out * out, axis=-1, keepdims=True) + EPS) * gf_ref[...]
        o_ref[...] = out


def combine(ys_sorted, xm, start_col, cnt_col, mod, kb, ch, flags, rows, tiles_per_sample, n_x_tiles,
            n_samples, final_gain):
    d = xm.shape[1]
    nw = d // (2 * LANES)
    n_pairs = kb.shape[0]
    mod_idx = lambda p, kb, ch, fl: (jnp.where(kb[p] < n_x_tiles, kb[p] // tiles_per_sample, n_samples), 0, 0)
    in_specs = [pl.BlockSpec((COMB_ENT * nw, LANES), lambda p, kb, ch, fl: (ch[p], 0)),
                pl.BlockSpec((COMB_ROWS, d), lambda p, kb, ch, fl: (kb[p], 0)),
                pl.BlockSpec((COMB_ROWS, 1), lambda p, kb, ch, fl: (kb[p], 0)),
                pl.BlockSpec((COMB_ROWS, 1), lambda p, kb, ch, fl: (kb[p], 0)),
                pl.BlockSpec((1, 1, N_MOD * d), mod_idx)]
    args = [ys_sorted, xm, start_col, cnt_col, mod]
    if final_gain is not None:
        in_specs.append(pl.BlockSpec((1, d), lambda p, kb, ch, fl: (0, 0)))
        args.append(final_gain)
    return pl.pallas_call(
        functools.partial(_combine_kernel, d=d, final=final_gain is not None),
        name="combine",
        grid_spec=pltpu.PrefetchScalarGridSpec(
            num_scalar_prefetch=3, grid=(n_pairs,),
            in_specs=in_specs,
            out_specs=pl.BlockSpec((COMB_ROWS, d), lambda p, kb, ch, fl: (kb[p], 0)),
            scratch_shapes=[pltpu.VMEM((COMB_ENT, d), BF16)]),
        out_shape=jax.ShapeDtypeStruct((rows, d), F32),
        compiler_params=_params(("arbitrary",)),
    )(kb, ch, flags, *args)


def _combine_schedule(start_row, n_blocks, n_chunks):
    blk_start = start_row[::COMB_ROWS]
    blk_end = jnp.concatenate([blk_start[1:], jnp.full((1,), n_chunks * COMB_ENT, I32)])
    lo = jnp.minimum(blk_start // COMB_ENT, n_chunks - 1)
    hi = jnp.maximum(lo, (blk_end - 1) // COMB_ENT)
    cnt = hi - lo + 1
    off = jnp.cumsum(cnt) - cnt
    total = off[-1] + cnt[-1]
    n_pairs = n_blocks + n_chunks
    pidx = jnp.arange(n_pairs, dtype=I32)
    kb = jnp.sum((off[None, :] <= pidx[:, None]).astype(I32), axis=1) - 1
    valid = pidx < total
    ch = jnp.where(valid, lo[kb] + pidx - off[kb], hi[n_blocks - 1]).astype(I32)
    first = valid & (pidx == off[kb])
    last = valid & (pidx == off[kb] + cnt[kb] - 1)
    fresh = jnp.concatenate([jnp.ones((1,), bool), ch[1:] != ch[:-1]])
    flags = valid.astype(I32) + 2 * first.astype(I32) + 4 * last.astype(I32) + 8 * fresh.astype(I32)
    return kb, ch, flags


def _rope_tables(lx, n_ctx_rows, n_samples):
    n_freq = QK_DIM // 4
    freqs = ROPE_BASE ** (-jnp.arange(n_freq, dtype=F32) / n_freq)
    rows = lx // GRID_W
    row = jnp.repeat(jnp.arange(rows, dtype=F32), GRID_W)
    col = jnp.tile(jnp.arange(GRID_W, dtype=F32), rows)
    ang = jnp.concatenate([row[:, None] * freqs, col[:, None] * freqs], axis=-1)
    cos, sin = jnp.cos(ang), jnp.sin(ang)
    ct = jnp.tile(jnp.concatenate([cos, cos], axis=-1), (n_samples, LANES // QK_DIM))
    st = jnp.tile(jnp.concatenate([-sin, sin], axis=-1), (n_samples, LANES // QK_DIM))
    ct = jnp.concatenate([ct, jnp.ones((n_ctx_rows, LANES), F32)], axis=0)
    st = jnp.concatenate([st, jnp.zeros((n_ctx_rows, LANES), F32)], axis=0)
    return ct, st


def kernel(x, c, ctx, c_ctx, w_ada, b_ada, norm_mix, norm_ffn, w_in, w_out, ret_log_decay, ret_norm,
           diff_lambda, diff_norm, w_router, w_gate, w_up, w_down, norm_final):
    n_b, lx, d = x.shape
    lc = ctx.shape[1]
    depth = w_ada.shape[0]
    heads = ret_log_decay.shape[-1]
    dheads = diff_norm.shape[-1] // V_DIM
    n_e = w_router.shape[-1]
    rx, rc = n_b * lx, n_b * lc
    cap_x = CAPACITY_FACTOR * lx // n_e
    cap_c = CAPACITY_FACTOR * lc // n_e
    tm = 512 if (lx % 512 == 0 and rx % 512 == 0 and rc % 512 == 0) else 256
    tmo = tm
    assert lx % lc == 0 and rc % tmo == 0 and heads % 2 == 0 and n_b < MOD_ROWS

    qk_scale = QK_DIM ** -0.5
    n_rq = heads * QK_DIM // LANES
    rope = tuple([True] * (2 * n_rq) + [False] * (2 * heads) + [True] * (2 * dheads) + [False] * dheads)
    slab_scale = tuple([qk_scale] * n_rq + [1.0] * (n_rq + 2 * heads) + [qk_scale * math.log2(math.e)] * dheads
                       + [1.0] * (2 * dheads))
    w_in_bf = _prep_w_in(w_in, slab_scale)

    xa = jnp.concatenate([x.reshape(rx, d), ctx.reshape(rc, d)], axis=0)
    cc = jnp.concatenate([c, c_ctx[None], jnp.zeros((MOD_ROWS - n_b - 1, d), F32)], axis=0)
    mods = ada_modulation(cc, w_ada, b_ada)
    ct, st = _rope_tables(lx, rc, n_b)
    log_g = jnp.log1p(-jnp.exp(ret_log_decay.astype(F32)))
    out = None

    for layer in range(depth):
        last = layer == depth - 1
        lam_init = 0.8 - 0.6 * math.exp(-0.3 * layer)
        mod = mods[layer].reshape(MOD_ROWS, 1, N_MOD * d)
        p = in_projection(xa, mod, norm_mix[layer][None], w_in_bf, layer, ct, st,
                          rx // tm, lx // tm, n_b, rope, tm)

        ret = retention(p, log_g[layer], ret_norm[layer][None], n_b, lx, lc, heads, not last)
        dif_x = diff_attention(p, diff_lambda[layer], diff_norm[layer][None], n_b, lx, lc, heads, dheads,
                               lam_init, True, 1024 if lx % 1024 == 0 else 256)
        if last:
            rows = rx
            mix_c = None
        else:
            rows = rx + rc
            dif_c = diff_attention(p, diff_lambda[layer], diff_norm[layer][None], n_b, lx, lc, heads, dheads,
                                   lam_init, False, lc)
            mix_c = (ret[1], dif_c)

        xm, h2, aff = out_projection((ret[0], dif_x), mix_c, w_out[layer].astype(BF16), xa, mod,
                                     norm_ffn[layer][None], w_router[layer], rows, rx // tmo, lx // tmo, n_b, tmo)
        aff_t = aff.T

        sel = [expert_choice(aff_t, n_b, lx, 0, cap_x, 0, 0)]
        if not last:
            sel.append(expert_choice(aff_t, n_b, lc, rx // lc, cap_c, rx, n_b * n_e * cap_x))
        to_es = lambda t: jnp.transpose(t, (2, 0, 1)).reshape(n_e, -1)
        idx = jnp.concatenate([to_es(s[0]) for s in sel], axis=1)
        gates = jnp.concatenate([to_es(s[1]) for s in sel], axis=1)
        dest = jnp.concatenate([to_es(s[2]) for s in sel], axis=1)
        start_row = jnp.concatenate([s[3][0] for s in sel])
        cnt_row = jnp.concatenate([s[4][0] for s in sel])
        slots = idx.shape[1]
        n_ent = n_e * slots

        ys_sorted = expert_ffn(h2, idx.reshape(-1), dest.reshape(-1), w_gate, w_up, w_down, layer,
                               gates.reshape(n_e, slots, 1), d)

        kb, ch, flags = _combine_schedule(start_row, rows // COMB_ROWS, n_ent // COMB_ENT)
        xa_new = combine(ys_sorted, xm, start_row[:, None], cnt_row[:, None], mod, kb, ch, flags, rows,
                         lx // COMB_ROWS, rx // COMB_ROWS, n_b, norm_final[None] if last else None)
        if last:
            out = xa_new
        else:
            xa = xa_new
    return out.reshape(n_b, lx, d)
```

```python
import functools
import math

import jax
import jax.numpy as jnp
from jax import lax
from jax.experimental import pallas as pl
from jax.experimental.pallas import tpu as pltpu

F32 = jnp.float32
BF16 = jnp.bfloat16
I32 = jnp.int32

EPS = 1e-6
N_MOD = 6
GRID_W = 64
ROPE_BASE = 10000.0
QK_DIM = 64
V_DIM = 128
LANES = 128
CHUNK = 128
CAPACITY_FACTOR = 2
MOD_ROWS = 8
VMEM_LIMIT = 56 * 1024 * 1024


def _params(sem, vmem=VMEM_LIMIT):
    return pltpu.CompilerParams(dimension_semantics=sem, vmem_limit_bytes=vmem)


def _silu(x):
    return x / (1.0 + jnp.exp(-x))


def _dot(a, b):
    return jnp.dot(a, b, preferred_element_type=F32)


def _dot_nt(a, b):
    return lax.dot_general(a, b, (((1,), (1,)), ((), ())), preferred_element_type=F32)


def _ada_kernel(c_ref, w_ref, b_ref, o_ref):
    s = _silu(c_ref[...]).astype(BF16)
    o_ref[0] = _dot(s, w_ref[0].astype(BF16)) + b_ref[0]


def ada_modulation(cc, w_ada, b_ada):
    depth, d, n = w_ada.shape
    tn = 1024 if n % 1024 == 0 else n
    return pl.pallas_call(
        _ada_kernel,
        name="ada_mod",
        grid=(depth, n // tn),
        in_specs=[pl.BlockSpec((MOD_ROWS, d), lambda l, j: (0, 0)),
                  pl.BlockSpec((1, d, tn), lambda l, j: (l, 0, j)),
                  pl.BlockSpec((1, 1, tn), lambda l, j: (l, 0, j))],
        out_specs=pl.BlockSpec((1, MOD_ROWS, tn), lambda l, j: (l, 0, j)),
        out_shape=jax.ShapeDtypeStruct((depth, MOD_ROWS, n), F32),
        compiler_params=_params(("arbitrary", "arbitrary")),
    )(cc, w_ada, b_ada.reshape(depth, 1, n))


def _rmsnorm_mod(x, gain, shift, scale):
    y = x * lax.rsqrt(jnp.mean(x * x, axis=-1, keepdims=True) + EPS) * gain
    return y * (1.0 + scale) + shift


MXU_N = 256


def _group_of(lane):
    return (lane >> 6) & 1


def _swap_halves(a):
    lane = lax.broadcasted_iota(I32, a.shape, 1)
    return jnp.where((lane & 32) == 0, pltpu.roll(a, 96, 1), pltpu.roll(a, 32, 1))


def _inproj_kernel(x_ref, mod_ref, g_ref, w_ref, ct_ref, st_ref, o_ref, *, d, rope):
    m = mod_ref[0]
    h = _rmsnorm_mod(x_ref[...], g_ref[...], m[:, 0:d], m[:, d:2 * d]).astype(BF16)
    per = MXU_N // LANES
    for s in range(w_ref.shape[2] // MXU_N):
        acc = _dot(h, w_ref[0, :, s * MXU_N:(s + 1) * MXU_N])
        for t in range(per):
            a = acc[:, t * LANES:(t + 1) * LANES]
            if rope[s * per + t]:
                a = a * ct_ref[...] + _swap_halves(a) * st_ref[...]
            c0 = s * MXU_N + t * LANES
            o_ref[:, c0:c0 + LANES] = a.astype(BF16)


def in_projection(xa, mod, gain, w_bf, layer, ct, st, n_x_tiles, tiles_per_sample, n_samples, rope, tm):
    r, d = xa.shape
    n = w_bf.shape[2]
    mod_idx = lambda i: (jnp.where(i < n_x_tiles, i // tiles_per_sample, n_samples), 0, 0)
    return pl.pallas_call(
        functools.partial(_inproj_kernel, d=d, rope=rope),
        name="in_proj",
        grid=(r // tm,),
        in_specs=[pl.BlockSpec((tm, d), lambda i: (i, 0)),
                  pl.BlockSpec((1, 1, N_MOD * d), mod_idx),
                  pl.BlockSpec((1, d), lambda i: (0, 0)),
                  pl.BlockSpec((1, d, n), lambda i: (layer, 0, 0), pipeline_mode=pl.Buffered(1)),
                  pl.BlockSpec((tm, LANES), lambda i: (i, 0)),
                  pl.BlockSpec((tm, LANES), lambda i: (i, 0))],
        out_specs=pl.BlockSpec((tm, n), lambda i: (i, 0)),
        out_shape=jax.ShapeDtypeStruct((r, n), BF16),
        compiler_params=_params(("arbitrary",)),
    )(xa, mod, gain, w_bf, ct, st)


def _prep_w_in(w_in, slab_scale):
    col_scale = jnp.repeat(jnp.asarray(slab_scale, F32), LANES)
    return (w_in * col_scale).astype(BF16)


def _ret_kernel(lg_ref, qx_ref, kx_ref, vx_ref, gx_ref, qc_ref, kc_ref, vc_ref, gc_ref, gain_ref,
                *rest, lx, lc, ctx_out):
    if ctx_out:
        ox_ref, oc_ref, accx, accc = rest
    else:
        ox_ref, accx = rest
    pair = pl.program_id(1)
    w2 = 2 * V_DIM
    lane = lax.broadcasted_iota(I32, (CHUNK, LANES), 1)
    dif = (lax.broadcasted_iota(I32, (CHUNK, CHUNK), 0) - lax.broadcasted_iota(I32, (CHUNK, CHUNK), 1)).astype(F32)
    pos = lax.broadcasted_iota(I32, (CHUNK, 1), 0).astype(F32)
    cpos = lax.broadcasted_iota(I32, (lc, 1), 0).astype(F32)
    c_len = float(CHUNK)
    lgf = [lg_ref[0, 2 * pair + hh] for hh in range(2)]
    lgb = [lg_ref[1, 2 * pair + hh] for hh in range(2)]

    def per_head(fn):
        vals = [fn(hh) for hh in range(2)]
        return jnp.concatenate([jnp.broadcast_to(v, (v.shape[0], V_DIM)) for v in vals], axis=1)

    decay = per_head(lambda hh: jnp.where(dif >= 0, jnp.exp(lgf[hh] * jnp.maximum(dif, 0.0)),
                                          jnp.exp(lgb[hh] * jnp.maximum(-dif, 0.0))))
    qd_f = per_head(lambda hh: jnp.exp(lgf[hh] * (pos + 1.0)))
    kd_f = per_head(lambda hh: jnp.exp(lgf[hh] * (c_len - 1.0 - pos)))
    qd_b = per_head(lambda hh: jnp.exp(lgb[hh] * (c_len - pos)))
    kd_b = per_head(lambda hh: jnp.exp(lgb[hh] * pos))
    one = jnp.ones((1, 1), F32)
    cd_f = per_head(lambda hh: jnp.exp(lgf[hh] * c_len) * one)
    cd_b = per_head(lambda hh: jnp.exp(lgb[hh] * c_len) * one)
    srow = lax.broadcasted_iota(I32, (LANES, w2), 0)
    scol = lax.broadcasted_iota(I32, (LANES, w2), 1)
    diag = (_group_of(srow) == 0) == (scol < V_DIM)
    zeros_v = jnp.zeros((CHUNK, V_DIM), BF16)

    def kv_state(k, v_scaled):
        return jnp.where(diag, _dot(k.astype(F32).T.astype(BF16), v_scaled.astype(BF16)), 0.0)

    kc = kc_ref[...]
    vc = vc_ref[...].astype(F32)
    s_f = kv_state(kc, vc * per_head(lambda hh: jnp.exp(lgf[hh] * (lc - 1.0 - cpos))))
    s_b = kv_state(kc, vc * per_head(lambda hh: jnp.exp(lgb[hh] * cpos)))

    def sweeps(q_ref, k_ref, v_ref, g_ref, o_ref, acc_ref, n, s_f0, s_b0):
        def rows_of(c):
            return pl.ds(pl.multiple_of(c * CHUNK, CHUNK), CHUNK)

        def forward(c, s):
            rows = rows_of(c)
            q, k, v = q_ref[rows, :], k_ref[rows, :], v_ref[rows, :]
            k_bd = jnp.concatenate([jnp.where(_group_of(lane) == 0, k, 0), jnp.where(_group_of(lane) == 1, k, 0)],
                                   axis=0)
            v_bd = jnp.concatenate([jnp.concatenate([v[:, :V_DIM], zeros_v], axis=1),
                                    jnp.concatenate([zeros_v, v[:, V_DIM:]], axis=1)], axis=0)
            sc = _dot_nt(q, k_bd) * decay
            acc_ref[rows, :] = _dot(sc.astype(BF16), v_bd) + _dot(q, s.astype(BF16)) * qd_f
            return s * cd_f + kv_state(k, v.astype(F32) * kd_f)

        def backward(t, s):
            rows = rows_of(n - 1 - t)
            q, k, v = q_ref[rows, :], k_ref[rows, :], v_ref[rows, :]
            y = acc_ref[rows, :] + _dot(q, s.astype(BF16)) * qd_b
            for hh in range(2):
                vsl = slice(hh * V_DIM, (hh + 1) * V_DIM)
                yh = y[:, vsl]
                yc = yh - jnp.mean(yh, axis=-1, keepdims=True)
                out = yc * lax.rsqrt(jnp.mean(yc * yc, axis=-1, keepdims=True) + EPS) * gain_ref[:, vsl]
                o_ref[rows, vsl] = (_silu(g_ref[rows, vsl].astype(F32)) * out).astype(BF16)
            return s * cd_b + kv_state(k, v.astype(F32) * kd_b)

        unroll = 4 if n % 4 == 0 else n
        lax.fori_loop(0, n, forward, s_f0, unroll=unroll)
        lax.fori_loop(0, n, backward, s_b0, unroll=unroll)

    sweeps(qx_ref, kx_ref, vx_ref, gx_ref, ox_ref, accx, lx // CHUNK, s_f, s_b)
    if ctx_out:
        zero = jnp.zeros((LANES, w2), F32)
        sweeps(qc_ref, kc_ref, vc_ref, gc_ref, oc_ref, accc, lc // CHUNK, zero, zero)


def retention(p, lg, gain, n_samples, lx, lc, heads, ctx_out):
    pairs = heads // 2
    half = heads // 2
    cb = n_samples * lx // lc
    w2 = 2 * V_DIM
    in_specs = [
        pl.BlockSpec(memory_space=pltpu.SMEM),
        pl.BlockSpec((lx, LANES), lambda b, h: (b, h)),
        pl.BlockSpec((lx, LANES), lambda b, h: (b, half + h)),
        pl.BlockSpec((lx, w2), lambda b, h: (b, half + h)),
        pl.BlockSpec((lx, w2), lambda b, h: (b, heads + h)),
        pl.BlockSpec((lc, LANES), lambda b, h: (cb + b, h)),
        pl.BlockSpec((lc, LANES), lambda b, h: (cb + b, half + h)),
        pl.BlockSpec((lc, w2), lambda b, h: (cb + b, half + h)),
        pl.BlockSpec((lc, w2), lambda b, h: (cb + b, heads + h)),
        pl.BlockSpec((1, w2), lambda b, h: (0, h)),
    ]
    out_specs = [pl.BlockSpec((lx, w2), lambda b, h: (b, h))]
    out_shape = [jax.ShapeDtypeStruct((n_samples * lx, heads * V_DIM), BF16)]
    scratch = [pltpu.VMEM((lx, w2), F32)]
    if ctx_out:
        out_specs.append(pl.BlockSpec((lc, w2), lambda b, h: (b, h)))
        out_shape.append(jax.ShapeDtypeStruct((n_samples * lc, heads * V_DIM), BF16))
        scratch.append(pltpu.VMEM((lc, w2), F32))
    return pl.pallas_call(
        functools.partial(_ret_kernel, lx=lx, lc=lc, ctx_out=ctx_out),
        name="retention",
        grid=(n_samples, pairs),
        in_specs=in_specs, out_specs=out_specs, out_shape=out_shape, scratch_shapes=scratch,
        compiler_params=_params(("arbitrary", "arbitrary")),
    )(lg, p, p, p, p, p, p, p, p, gain)


def _diff_kernel(lam_ref, q_ref, kc_ref, vc_ref, *rest, lam_init, has_x, kb):
    if has_x:
        kx_ref, vx_ref, gain_ref, o_ref = rest
    else:
        gain_ref, o_ref = rest
    lv = lam_ref[...]
    lam = (jnp.exp(jnp.sum(lv[0:1] * lv[1:2], axis=-1, keepdims=True))
           - jnp.exp(jnp.sum(lv[2:3] * lv[3:4], axis=-1, keepdims=True)) + lam_init)
    q = q_ref[...]
    tq = q.shape[0]
    lane = lax.broadcasted_iota(I32, q.shape, 1)
    blocks = [(kc_ref, vc_ref, 0, kc_ref.shape[0])]
    if has_x:
        blocks += [(kx_ref, vx_ref, r0, kb) for r0 in range(0, kx_ref.shape[0], kb)]
    v_ext = [jnp.concatenate([v_ref[r0:r0 + n, :], jnp.ones((n, V_DIM), BF16)], axis=1) for _, v_ref, r0, n in blocks]
    heads_out = []
    for t in range(2):
        qm = jnp.where(_group_of(lane) == t, q, 0)
        m = jnp.full((tq, 1), -jnp.inf, F32)
        acc = jnp.zeros((tq, 2 * V_DIM), F32)
        for (k_ref, _, r0, n), ve in zip(blocks, v_ext):
            s = _dot_nt(qm, k_ref[r0:r0 + n, :])
            m_new = jnp.maximum(m, jnp.max(s, axis=-1, keepdims=True))
            acc = acc * jnp.exp2(m - m_new) + _dot(jnp.exp2(s - m_new).astype(BF16), ve)
            m = m_new
        heads_out.append(acc[:, :V_DIM] * (1.0 / acc[:, V_DIM:V_DIM + 1]))
    out = heads_out[0] - lam * heads_out[1]
    y = out * lax.rsqrt(jnp.mean(out * out, axis=-1, keepdims=True) + EPS) * gain_ref[...]
    o_ref[...] = (y * (1.0 - lam_init)).astype(BF16)


def diff_attention(p, lam_vec, gain, n_samples, lx, lc, ret_heads, heads, lam_init, latent, tq):
    qo = 3 * ret_heads
    ko = qo + heads
    vo = ko + heads
    cb = n_samples * lx // lc
    lq = lx if latent else lc
    nq = lq // tq
    q_row = (lambda b, t: b * nq + t) if latent else (lambda b, t: n_samples * lx // tq + b * nq + t)
    in_specs = [
        pl.BlockSpec((4, QK_DIM), lambda b, h, t: (0, 0)),
        pl.BlockSpec((tq, LANES), lambda b, h, t: (q_row(b, t), qo + h)),
        pl.BlockSpec((lc, LANES), lambda b, h, t: (cb + b, ko + h)),
        pl.BlockSpec((lc, LANES), lambda b, h, t: (cb + b, vo + h)),
    ]
    args = [lam_vec, p, p, p]
    if latent:
        in_specs += [pl.BlockSpec((lx, LANES), lambda b, h, t: (b, ko + h)),
                     pl.BlockSpec((lx, LANES), lambda b, h, t: (b, vo + h))]
        args += [p, p]
    in_specs.append(pl.BlockSpec((1, LANES), lambda b, h, t: (0, h)))
    args.append(gain)
    return pl.pallas_call(
        functools.partial(_diff_kernel, lam_init=lam_init, has_x=latent, kb=min(1024, lx)),
        name="diff_attn_x" if latent else "diff_attn_ctx",
        grid=(n_samples, heads, nq),
        in_specs=in_specs,
        out_specs=pl.BlockSpec((tq, LANES), lambda b, h, t: (b * nq + t, h)),
        out_shape=jax.ShapeDtypeStruct((n_samples * lq, heads * V_DIM), BF16),
        compiler_params=_params(("arbitrary", "arbitrary", "arbitrary")),
    )(*args)


def _pack_pair(a, b):
    ua = pltpu.bitcast(a.astype(BF16).astype(F32), I32)
    ub = pltpu.bitcast(b.astype(BF16).astype(F32), I32)
    return ua | lax.shift_right_logical(ub, 16)


def _unpack_pair(w):
    return (pltpu.bitcast(w & -65536, F32).astype(BF16), pltpu.bitcast(w << 16, F32).astype(BF16))


def _store_packed_rows(ref, row0, n_rows, val):
    nw = val.shape[1] // (2 * LANES)
    for j in range(nw):
        w = _pack_pair(val[:, j * LANES:(j + 1) * LANES], val[:, (nw + j) * LANES:(nw + j + 1) * LANES])
        ref[pl.ds(row0 * nw + j, n_rows, stride=nw), :] = w


def _load_packed_rows(ref, row0, n_rows, nw, dst, dst_row0):
    for j in range(nw):
        hi, lo = _unpack_pair(ref[pl.ds(row0 * nw + j, n_rows, stride=nw), :])
        dst[dst_row0:dst_row0 + n_rows, j * LANES:(j + 1) * LANES] = hi
        dst[dst_row0:dst_row0 + n_rows, (nw + j) * LANES:(nw + j + 1) * LANES] = lo


def _outproj_kernel(*refs, d, wr, n_x_tiles, with_ctx):
    if with_ctx:
        retx_ref, difx_ref, retc_ref, difc_ref = refs[:4]
        is_ctx = pl.program_id(0) >= n_x_tiles
        ret = jnp.where(is_ctx, retc_ref[...], retx_ref[...])
        dif = jnp.where(is_ctx, difc_ref[...], difx_ref[...])
        refs = refs[4:]
    else:
        ret, dif = refs[0][...], refs[1][...]
        refs = refs[2:]
    w_ref, x_ref, mod_ref, g_ref, wr_ref, xm_ref, h_ref, aff_ref = refs
    y = _dot(ret, w_ref[0:wr, :]) + _dot(dif, w_ref[wr:, :])
    m = mod_ref[0]
    xm = x_ref[...] + m[:, 2 * d:3 * d] * y
    xm_ref[...] = xm
    h = _rmsnorm_mod(xm, g_ref[...], m[:, 3 * d:4 * d], m[:, 4 * d:5 * d])
    _store_packed_rows(h_ref, 0, h.shape[0], h)
    n_e = wr_ref.shape[1] // 2
    h_hi = h.astype(BF16)
    h_lo = (h - h_hi.astype(F32)).astype(BF16)
    both = _dot(h_hi, wr_ref[...])
    logits = both[:, :n_e] + both[:, n_e:] + _dot(h_lo, wr_ref[...])[:, :n_e]
    e = jnp.exp(logits - jnp.max(logits, axis=-1, keepdims=True))
    aff_ref[...] = e / jnp.sum(e, axis=-1, keepdims=True)


def out_projection(mix_x, mix_c, w_bf, xa, mod, gain, w_router, rows, n_x_tiles, tiles_per_sample, n_samples, tm):
    d = xa.shape[1]
    wr = mix_x[0].shape[1]
    wd = mix_x[1].shape[1]
    e = w_router.shape[1]
    nw = d // (2 * LANES)
    wr_hi = w_router.astype(BF16)
    wr_split = jnp.concatenate([wr_hi, (w_router - wr_hi.astype(F32)).astype(BF16)], axis=1)
    mod_idx = lambda i: (jnp.where(i < n_x_tiles, i // tiles_per_sample, n_samples), 0, 0)
    x_idx = lambda i: (jnp.minimum(i, n_x_tiles - 1), 0)
    c_idx = lambda i: (jnp.maximum(i - n_x_tiles, 0), 0)
    mix_specs = [pl.BlockSpec((tm, wr), x_idx), pl.BlockSpec((tm, wd), x_idx)]
    mix_args = list(mix_x)
    if mix_c is not None:
        mix_specs += [pl.BlockSpec((tm, wr), c_idx), pl.BlockSpec((tm, wd), c_idx)]
        mix_args += list(mix_c)
    return pl.pallas_call(
        functools.partial(_outproj_kernel, d=d, wr=wr, n_x_tiles=n_x_tiles, with_ctx=mix_c is not None),
        name="out_proj_router",
        grid=(rows // tm,),
        in_specs=mix_specs + [
                  pl.BlockSpec(w_bf.shape, lambda i: (0, 0), pipeline_mode=pl.Buffered(1)),
                  pl.BlockSpec((tm, d), lambda i: (i, 0)),
                  pl.BlockSpec((1, 1, N_MOD * d), mod_idx),
                  pl.BlockSpec((1, d), lambda i: (0, 0)),
                  pl.BlockSpec((d, 2 * e), lambda i: (0, 0))],
        out_specs=[pl.BlockSpec((tm, d), lambda i: (i, 0)),
                   pl.BlockSpec((tm * nw, LANES), lambda i: (i, 0)),
                   pl.BlockSpec((tm, e), lambda i: (i, 0))],
        out_shape=[jax.ShapeDtypeStruct((rows, d), F32),
                   jax.ShapeDtypeStruct((rows * nw, LANES), I32),
                   jax.ShapeDtypeStruct((rows, e), F32)],
        compiler_params=_params(("arbitrary",)),
    )(*mix_args, w_bf, xa, mod, gain, wr_split)


def _ones(mask, dtype):
    return jnp.where(mask, 1.0, 0.0).astype(dtype)


def _prefix_lanes(m):
    upper = _ones(lax.broadcasted_iota(I32, (LANES, LANES), 0) < lax.broadcasted_iota(I32, (LANES, LANES), 1), BF16)
    run = jnp.zeros((m.shape[0], 1), F32)
    outs = []
    for s in range(m.shape[1] // LANES):
        seg = m[:, s * LANES:(s + 1) * LANES]
        outs.append(_dot(seg, upper) + run)
        run = run + jnp.sum(seg.astype(F32), axis=1, keepdims=True)
    return jnp.concatenate(outs, axis=1)


DIRECT_BLOCKS = 8


def _topk_kernel(a_ref, idx_ref, gate_ref, dest_ref, start_ref, cnt_ref, *scr, lg, cap, row_base, ent_base):
    g = pl.program_id(0)
    a = a_ref[...]
    n_e = a.shape[0]
    bits = pltpu.bitcast(a, I32)

    def bisect(_, lohi):
        lo, hi = lohi
        mid = lo + ((hi - lo) >> 1)
        ge = jnp.sum(_ones(bits >= mid, F32), axis=1, keepdims=True) >= cap
        return jnp.where(ge, mid, lo), jnp.where(ge, hi, mid)

    thr, _ = lax.fori_loop(0, 31, bisect,
                           (jnp.zeros((n_e, 1), I32), jnp.full((n_e, 1), 0x7F800000, I32)))
    gt = bits > thr
    eq = bits == thr
    need = cap - jnp.sum(_ones(gt, F32), axis=1, keepdims=True)
    sel = gt | (eq & (_prefix_lanes(_ones(eq, BF16)) < need))
    selb = _ones(sel, BF16)
    pos = _prefix_lanes(selb)
    start_t = jnp.sum(pos, axis=0, keepdims=True)
    lower = _ones(lax.broadcasted_iota(I32, (n_e, n_e), 0) > lax.broadcasted_iota(I32, (n_e, n_e), 1), BF16)
    base = (ent_base + g * (n_e * cap)).astype(F32)
    start_ref[...] = (start_t + base).astype(I32)
    cnt_ref[...] = jnp.sum(_ones(sel, F32), axis=0, keepdims=True).astype(I32)
    posm = jnp.where(sel, pos, -1.0)
    dest = start_t + base + _dot(lower, selb)
    tok_base = (row_base + g * lg).astype(F32)
    slot = lax.broadcasted_iota(I32, (cap, 1), 0).astype(F32)
    col = lax.broadcasted_iota(I32, (cap, n_e), 1)
    z = jnp.zeros((cap, n_e), F32)

    def emit(ia, ga, da):
        idx_ref[0] = ia.astype(I32)
        gate_ref[0] = ga
        dest_ref[0] = da.astype(I32)

    nb = lg // LANES
    if nb <= DIRECT_BLOCKS:
        (row_scr,) = scr
        row_scr[0], row_scr[1], row_scr[2] = posm, a, dest
        tvals = tok_base + lax.broadcasted_iota(I32, (1, lg), 1).astype(F32)

        def compact_direct(e, carry):
            ia, ga, da = carry
            m = row_scr[0, pl.ds(e, 1), :] == slot
            i_e = jnp.sum(jnp.where(m, tvals, 0.0), axis=1, keepdims=True)
            g_e = jnp.sum(jnp.where(m, row_scr[1, pl.ds(e, 1), :], 0.0), axis=1, keepdims=True)
            d_e = jnp.sum(jnp.where(m, row_scr[2, pl.ds(e, 1), :], 0.0), axis=1, keepdims=True)
            here = col == e
            return jnp.where(here, i_e, ia), jnp.where(here, g_e, ga), jnp.where(here, d_e, da)

        emit(*lax.fori_loop(0, n_e, compact_direct, (z, z, z)))
        return

    blk_scr, first_scr = scr
    nbp = blk_scr.shape[1] // n_e
    if nbp != nb:
        blk_scr[...] = jnp.zeros_like(blk_scr)
    for k in range(nb):
        blk = slice(k * LANES, (k + 1) * LANES)
        pieces = []
        for x in (posm[:, blk], a[:, blk], dest[:, blk]):
            x1 = x.astype(BF16).astype(F32)
            x2 = (x - x1).astype(BF16).astype(F32)
            pieces += [x1, x2, (x - x1 - x2).astype(BF16).astype(F32)]
        for j in range(9):
            blk_scr[j, pl.ds(k, n_e, stride=nbp), :] = pieces[j]
    before = _ones(lax.broadcasted_iota(I32, (lg, nb), 0) < LANES * lax.broadcasted_iota(I32, (lg, nb), 1), BF16)
    first_scr[...] = _dot(selb, before)

    blane = lax.broadcasted_iota(I32, (cap, nbp), 1).astype(F32)
    lane_f = lax.broadcasted_iota(I32, (cap, LANES), 1).astype(F32)
    w3 = 3 * LANES

    def compact(e, carry):
        ia, ga, da = carry
        kb = jnp.sum(_ones(first_scr[pl.ds(e, 1), :] <= slot, F32), axis=1, keepdims=True) - 1.0
        onehot = _ones(blane == kb, BF16)
        own = pl.ds(pl.multiple_of(e * nbp, nbp), nbp)
        rows = jnp.concatenate([blk_scr[j, own, :] for j in range(9)], axis=1).astype(BF16)
        parts = _dot(onehot, rows)
        pos_g, aff_g, dest_g = [parts[:, q * w3:q * w3 + LANES] + parts[:, q * w3 + LANES:q * w3 + 2 * LANES]
                                + parts[:, q * w3 + 2 * LANES:(q + 1) * w3] for q in range(3)]
        m = pos_g == slot
        i_e = jnp.sum(jnp.where(m, lane_f, 0.0), axis=1, keepdims=True) + kb * LANES + tok_base
        g_e = jnp.sum(jnp.where(m, aff_g, 0.0), axis=1, keepdims=True)
        d_e = jnp.sum(jnp.where(m, dest_g, 0.0), axis=1, keepdims=True)
        here = col == e
        return jnp.where(here, i_e, ia), jnp.where(here, g_e, ga), jnp.where(here, d_e, da)

    emit(*lax.fori_loop(0, n_e, compact, (z, z, z)))


def expert_choice(aff_t, groups, lg, first_block, cap, row_base, ent_base):
    e = aff_t.shape[0]
    return pl.pallas_call(
        functools.partial(_topk_kernel, lg=lg, cap=cap, row_base=row_base, ent_base=ent_base),
        name="expert_choice",
        grid=(groups,),
        in_specs=[pl.BlockSpec((e, lg), lambda g: (0, first_block + g))],
        out_specs=[pl.BlockSpec((1, cap, e), lambda g: (g, 0, 0)),
                   pl.BlockSpec((1, cap, e), lambda g: (g, 0, 0)),
                   pl.BlockSpec((1, cap, e), lambda g: (g, 0, 0)),
                   pl.BlockSpec((1, lg), lambda g: (0, g)),
                   pl.BlockSpec((1, lg), lambda g: (0, g))],
        out_shape=[jax.ShapeDtypeStruct((groups, cap, e), I32),
                   jax.ShapeDtypeStruct((groups, cap, e), F32),
                   jax.ShapeDtypeStruct((groups, cap, e), I32),
                   jax.ShapeDtypeStruct((1, groups * lg), I32),
                   jax.ShapeDtypeStruct((1, groups * lg), I32)],
        scratch_shapes=([pltpu.VMEM((3, e, lg), F32)] if lg // LANES <= DIRECT_BLOCKS else
                        [pltpu.VMEM((9, e * (-(-(lg // LANES) // 8) * 8), LANES), F32),
                         pltpu.VMEM((e, lg // LANES), F32)]),
        compiler_params=_params(("arbitrary",)),
    )(aff_t)


def _ffn_kernel(idx_ref, dest_ref, h_ref, wg_ref, wu_ref, wd_ref, gate_ref, ys_ref,
                xs_buf, ys_buf, xb_scr, hid_scr, wd_scr, gsem, ssem, *, ts, nw, tf):
    e = pl.program_id(0)
    f = pl.program_id(1)
    n_e = pl.num_programs(0)
    last_f = pl.num_programs(1) - 1
    s = xb_scr.shape[0]

    n_f = hid_scr.shape[0]
    per_step = s // n_f
    slot = e % 2

    def gather_start(expert, into, row):
        src = pl.multiple_of(idx_ref[expert * s + row] * nw, nw)
        pltpu.make_async_copy(h_ref.at[pl.ds(src, nw)],
                              xs_buf.at[into, pl.ds(pl.multiple_of(row * nw, nw), nw)], gsem.at[into]).start()

    def wait_gather(into):
        pltpu.make_async_copy(h_ref.at[pl.ds(0, s * nw)], xs_buf.at[into], gsem.at[into]).wait()

    def scatter_start(row):
        dst = pl.multiple_of(dest_ref[e * s + row] * nw, nw)
        pltpu.make_async_copy(ys_buf.at[pl.ds(row * nw, nw)], ys_ref.at[pl.ds(dst, nw)], ssem).start()

    def wait_scatter():
        pltpu.make_async_copy(ys_buf, ys_ref.at[pl.ds(0, s * nw)], ssem).wait()

    @pl.when(f == 0)
    def _():
        @pl.when(e == 0)
        def _():
            def body(i, carry):
                gather_start(0, 0, i)
                return carry
            lax.fori_loop(0, s, body, 0, unroll=8)

        wait_gather(slot)
        _load_packed_rows(xs_buf.at[slot], 0, s, nw, xb_scr, 0)

    nxt = jnp.minimum(e + 1, n_e - 1)
    for i in range(per_step):
        gather_start(nxt, 1 - slot, f * per_step + i)

    wg = wg_ref[0, 0].astype(BF16)
    wu = wu_ref[0, 0].astype(BF16)
    for r in range(s // ts):
        rows = slice(r * ts, (r + 1) * ts)
        xb = xb_scr[rows, :]
        hid_scr[f, rows, :] = (_silu(_dot(xb, wg)) * _dot(xb, wu)).astype(BF16)
    wd_scr[pl.ds(pl.multiple_of(f * tf, tf), tf), :] = wd_ref[0, 0].astype(BF16)

    @pl.when(f == last_f)
    def _():
        @pl.when(e > 0)
        def _():
            wait_scatter()

        for r in range(s // ts):
            rows = slice(r * ts, (r + 1) * ts)
            hid = jnp.concatenate([hid_scr[c, rows, :] for c in range(n_f)], axis=1)
            y = _dot(hid, wd_scr[...])
            _store_packed_rows(ys_buf, r * ts, ts, y * gate_ref[0, rows, :])
            for i in range(ts):
                scatter_start(r * ts + i)

        @pl.when(e == n_e - 1)
        def _():
            wait_scatter()
            wait_gather(1 - slot)


def expert_ffn(h_packed, idx, dest, w_gate, w_up, w_down, layer, gates, d):
    nw = d // (2 * LANES)
    e, s = gates.shape[0], gates.shape[1]
    ff = w_gate.shape[3]
    tf = 256 if ff % 256 == 0 else ff
    ts = max(t for t in range(8, 577, 8) if s % t == 0)
    assert h_packed.shape[0] >= s * nw and s % (ff // tf) == 0
    return pl.pallas_call(
        functools.partial(_ffn_kernel, ts=ts, nw=nw, tf=tf),
        name="expert_ffn",
        grid_spec=pltpu.PrefetchScalarGridSpec(
            num_scalar_prefetch=2, grid=(e, ff // tf),
            in_specs=[pl.BlockSpec(memory_space=pl.ANY),
                      pl.BlockSpec((1, 1, d, tf), lambda i, f, si, di: (layer, i, 0, f)),
                      pl.BlockSpec((1, 1, d, tf), lambda i, f, si, di: (layer, i, 0, f)),
                      pl.BlockSpec((1, 1, tf, d), lambda i, f, si, di: (layer, i, f, 0)),
                      pl.BlockSpec((1, s, 1), lambda i, f, si, di: (i, 0, 0))],
            out_specs=pl.BlockSpec(memory_space=pl.ANY),
            scratch_shapes=[pltpu.VMEM((2, s * nw, LANES), I32), pltpu.VMEM((s * nw, LANES), I32),
                            pltpu.VMEM((s, d), BF16), pltpu.VMEM((ff // tf, s, tf), BF16),
                            pltpu.VMEM((ff, d), BF16),
                            pltpu.SemaphoreType.DMA((2,)), pltpu.SemaphoreType.DMA(())]),
        out_shape=jax.ShapeDtypeStruct((e * s * nw, LANES), I32),
        compiler_params=_params(("arbitrary", "arbitrary")),
    )(idx, dest, h_packed, w_gate, w_up, w_down, gates)


COMB_ROWS = 256
COMB_ENT = 256


def _combine_kernel(kb_ref, ch_ref, flag_ref, ys_ref, xm_ref, start_ref, cnt_ref, mod_ref, *rest, d, final):
    if final:
        gf_ref, o_ref, chunk_scr = rest
    else:
        o_ref, chunk_scr = rest
    p = pl.program_id(0)
    flags = flag_ref[p]

    @pl.when((flags & 2) != 0)
    def _():
        o_ref[...] = jnp.zeros_like(o_ref)

    @pl.when((flags & 8) != 0)
    def _():
        _load_packed_rows(ys_ref, 0, COMB_ENT, d // (2 * LANES), chunk_scr, 0)

    @pl.when((flags & 1) != 0)
    def _():
        gpos = ch_ref[p] * COMB_ENT + lax.broadcasted_iota(I32, (COMB_ROWS, COMB_ENT), 1)
        st = start_ref[...]
        onehot = _ones((gpos >= st) & (gpos < st + cnt_ref[...]), BF16)
        o_ref[...] += _dot(onehot, chunk_scr[...])

    @pl.when((flags & 4) != 0)
    def _():
        m = mod_ref[0]
        out = xm_ref[...] + m[:, 5 * d:6 * d] * o_ref[...]
        if final:
            out = out * lax.rsqrt(jnp.mean(out * out, axis=-1, keepdims=True) + EPS) * gf_ref[...]
        o_ref[...] = out


def combine(ys_sorted, xm, start_col, cnt_col, mod, kb, ch, flags, rows, tiles_per_sample, n_x_tiles,
            n_samples, final_gain):
    d = xm.shape[1]
    nw = d // (2 * LANES)
    n_pairs = kb.shape[0]
    mod_idx = lambda p, kb, ch, fl: (jnp.where(kb[p] < n_x_tiles, kb[p] // tiles_per_sample, n_samples), 0, 0)
    in_specs = [pl.BlockSpec((COMB_ENT * nw, LANES), lambda p, kb, ch, fl: (ch[p], 0)),
                pl.BlockSpec((COMB_ROWS, d), lambda p, kb, ch, fl: (kb[p], 0)),
                pl.BlockSpec((COMB_ROWS, 1), lambda p, kb, ch, fl: (kb[p], 0)),
                pl.BlockSpec((COMB_ROWS, 1), lambda p, kb, ch, fl: (kb[p], 0)),
                pl.BlockSpec((1, 1, N_MOD * d), mod_idx)]
    args = [ys_sorted, xm, start_col, cnt_col, mod]
    if final_gain is not None:
        in_specs.append(pl.BlockSpec((1, d), lambda p, kb, ch, fl: (0, 0)))
        args.append(final_gain)
    return pl.pallas_call(
        functools.partial(_combine_kernel, d=d, final=final_gain is not None),
        name="combine",
        grid_spec=pltpu.PrefetchScalarGridSpec(
            num_scalar_prefetch=3, grid=(n_pairs,),
            in_specs=in_specs,
            out_specs=pl.BlockSpec((COMB_ROWS, d), lambda p, kb, ch, fl: (kb[p], 0)),
            scratch_shapes=[pltpu.VMEM((COMB_ENT, d), BF16)]),
        out_shape=jax.ShapeDtypeStruct((rows, d), F32),
        compiler_params=_params(("arbitrary",)),
    )(kb, ch, flags, *args)


def _combine_schedule(start_row, n_blocks, n_chunks):
    blk_start = start_row[::COMB_ROWS]
    blk_end = jnp.concatenate([blk_start[1:], jnp.full((1,), n_chunks * COMB_ENT, I32)])
    lo = jnp.minimum(blk_start // COMB_ENT, n_chunks - 1)
    hi = jnp.maximum(lo, (blk_end - 1) // COMB_ENT)
    cnt = hi - lo + 1
    off = jnp.cumsum(cnt) - cnt
    total = off[-1] + cnt[-1]
    n_pairs = n_blocks + n_chunks
    pidx = jnp.arange(n_pairs, dtype=I32)
    kb = jnp.sum((off[None, :] <= pidx[:, None]).astype(I32), axis=1) - 1
    valid = pidx < total
    ch = jnp.where(valid, lo[kb] + pidx - off[kb], hi[n_blocks - 1]).astype(I32)
    first = valid & (pidx == off[kb])
    last = valid & (pidx == off[kb] + cnt[kb] - 1)
    fresh = jnp.concatenate([jnp.ones((1,), bool), ch[1:] != ch[:-1]])
    flags = valid.astype(I32) + 2 * first.astype(I32) + 4 * last.astype(I32) + 8 * fresh.astype(I32)
    return kb, ch, flags


def _rope_tables(lx, n_ctx_rows, n_samples):
    n_freq = QK_DIM // 4
    freqs = ROPE_BASE ** (-jnp.arange(n_freq, dtype=F32) / n_freq)
    rows = lx // GRID_W
    row = jnp.repeat(jnp.arange(rows, dtype=F32), GRID_W)
    col = jnp.tile(jnp.arange(GRID_W, dtype=F32), rows)
    ang = jnp.concatenate([row[:, None] * freqs, col[:, None] * freqs], axis=-1)
    cos, sin = jnp.cos(ang), jnp.sin(ang)
    ct = jnp.tile(jnp.concatenate([cos, cos], axis=-1), (n_samples, LANES // QK_DIM))
    st = jnp.tile(jnp.concatenate([-sin, sin], axis=-1), (n_samples, LANES // QK_DIM))
    ct = jnp.concatenate([ct, jnp.ones((n_ctx_rows, LANES), F32)], axis=0)
    st = jnp.concatenate([st, jnp.zeros((n_ctx_rows, LANES), F32)], axis=0)
    return ct, st


def kernel(x, c, ctx, c_ctx, w_ada, b_ada, norm_mix, norm_ffn, w_in, w_out, ret_log_decay, ret_norm,
           diff_lambda, diff_norm, w_router, w_gate, w_up, w_down, norm_final):
    n_b, lx, d = x.shape
    lc = ctx.shape[1]
    depth = w_ada.shape[0]
    heads = ret_log_decay.shape[-1]
    dheads = diff_norm.shape[-1] // V_DIM
    n_e = w_router.shape[-1]
    rx, rc = n_b * lx, n_b * lc
    cap_x = CAPACITY_FACTOR * lx // n_e
    cap_c = CAPACITY_FACTOR * lc // n_e
    tm = 512 if (lx % 512 == 0 and rx % 512 == 0 and rc % 512 == 0) else 256
    tmo = tm
    assert lx % lc == 0 and rc % tmo == 0 and heads % 2 == 0 and n_b < MOD_ROWS
    assert lx % CHUNK == 0 and lc % CHUNK == 0 and lx % GRID_W == 0 and d % (2 * LANES) == 0
    assert lx % COMB_ROWS == 0 and rc % COMB_ROWS == 0
    assert (n_b * n_e * cap_x) % COMB_ENT == 0 and (n_b * n_e * cap_c) % COMB_ENT == 0

    qk_scale = QK_DIM ** -0.5
    n_rq = heads * QK_DIM // LANES
    rope = tuple([True] * (2 * n_rq) + [False] * (2 * heads) + [True] * (2 * dheads) + [False] * dheads)
    slab_scale = tuple([qk_scale] * n_rq + [1.0] * (n_rq + 2 * heads) + [qk_scale * math.log2(math.e)] * dheads
                       + [1.0] * (2 * dheads))
    w_in_bf = _prep_w_in(w_in, slab_scale)

    xa = jnp.concatenate([x.reshape(rx, d), ctx.reshape(rc, d)], axis=0)
    cc = jnp.concatenate([c, c_ctx[None], jnp.zeros((MOD_ROWS - n_b - 1, d), F32)], axis=0)
    mods = ada_modulation(cc, w_ada, b_ada).reshape(depth, MOD_ROWS, 1, N_MOD * d)
    ct, st = _rope_tables(lx, rc, n_b)
    log_g = jnp.log1p(-jnp.exp(ret_log_decay.astype(F32)))
    out = None

    for layer in range(depth):
        last = layer == depth - 1
        lam_init = 0.8 - 0.6 * math.exp(-0.3 * layer)
        mod = mods[layer]
        p = in_projection(xa, mod, norm_mix[layer][None], w_in_bf, layer, ct, st,
                          rx // tm, lx // tm, n_b, rope, tm)

        ret = retention(p, log_g[layer], ret_norm[layer][None], n_b, lx, lc, heads, not last)
        dif_x = diff_attention(p, diff_lambda[layer], diff_norm[layer][None], n_b, lx, lc, heads, dheads,
                               lam_init, True, 1024 if lx % 1024 == 0 else 256)
        if last:
            rows = rx
            mix_c = None
        else:
            rows = rx + rc
            dif_c = diff_attention(p, diff_lambda[layer], diff_norm[layer][None], n_b, lx, lc, heads, dheads,
                                   lam_init, False, lc)
            mix_c = (ret[1], dif_c)

        xm, h2, aff = out_projection((ret[0], dif_x), mix_c, w_out[layer].astype(BF16), xa, mod,
                                     norm_ffn[layer][None], w_router[layer], rows, rx // tmo, lx // tmo, n_b, tmo)
        aff_t = aff.T

        sel = [expert_choice(aff_t, n_b, lx, 0, cap_x, 0, 0)]
        if not last:
            sel.append(expert_choice(aff_t, n_b, lc, rx // lc, cap_c, rx, n_b * n_e * cap_x))
        to_es = lambda t: jnp.transpose(t, (2, 0, 1)).reshape(n_e, -1)
        idx = jnp.concatenate([to_es(s[0]) for s in sel], axis=1)
        gates = jnp.concatenate([to_es(s[1]) for s in sel], axis=1)
        dest = jnp.concatenate([to_es(s[2]) for s in sel], axis=1)
        start_row = jnp.concatenate([s[3][0] for s in sel])
        cnt_row = jnp.concatenate([s[4][0] for s in sel])
        slots = idx.shape[1]
        n_ent = n_e * slots

        ys_sorted = expert_ffn(h2, idx.reshape(-1), dest.reshape(-1), w_gate, w_up, w_down, layer,
                               gates.reshape(n_e, slots, 1), d)

        kb, ch, flags = _combine_schedule(start_row, rows // COMB_ROWS, n_ent // COMB_ENT)
        xa_new = combine(ys_sorted, xm, start_row[:, None], cnt_row[:, None], mod, kb, ch, flags, rows,
                         lx // COMB_ROWS, rx // COMB_ROWS, n_b, norm_final[None] if last else None)
        if last:
            out = xa_new
        else:
            xa = xa_new
    return out.reshape(n_b, lx, d)
```

```python
import functools
import math

import jax
import jax.numpy as jnp
from jax import lax
from jax.experimental import pallas as pl
from jax.experimental.pallas import tpu as pltpu

F32 = jnp.float32
BF16 = jnp.bfloat16
I32 = jnp.int32

EPS = 1e-6
N_MOD = 6
GRID_W = 64
ROPE_BASE = 10000.0
QK_DIM = 64
V_DIM = 128
LANES = 128
CHUNK = 128
CAPACITY_FACTOR = 2
MOD_ROWS = 8
VMEM_LIMIT = 56 * 1024 * 1024


def _params(sem, vmem=VMEM_LIMIT):
    return pltpu.CompilerParams(dimension_semantics=sem, vmem_limit_bytes=vmem)


def _silu(x):
    return x / (1.0 + jnp.exp(-x))


def _dot(a, b):
    return jnp.dot(a, b, preferred_element_type=F32)


def _dot_nt(a, b):
    return lax.dot_general(a, b, (((1,), (1,)), ((), ())), preferred_element_type=F32)


def _ada_kernel(c_ref, w_ref, b_ref, o_ref):
    s = _silu(c_ref[...]).astype(BF16)
    o_ref[0] = _dot(s, w_ref[0].astype(BF16)) + b_ref[0]


def ada_modulation(cc, w_ada, b_ada):
    depth, d, n = w_ada.shape
    tn = 1024 if n % 1024 == 0 else n
    return pl.pallas_call(
        _ada_kernel,
        name="ada_mod",
        grid=(depth, n // tn),
        in_specs=[pl.BlockSpec((MOD_ROWS, d), lambda l, j: (0, 0)),
                  pl.BlockSpec((1, d, tn), lambda l, j: (l, 0, j)),
                  pl.BlockSpec((1, 1, tn), lambda l, j: (l, 0, j))],
        out_specs=pl.BlockSpec((1, MOD_ROWS, tn), lambda l, j: (l, 0, j)),
        out_shape=jax.ShapeDtypeStruct((depth, MOD_ROWS, n), F32),
        compiler_params=_params(("arbitrary", "arbitrary")),
    )(cc, w_ada, b_ada.reshape(depth, 1, n))


def _rmsnorm_mod(x, gain, shift, scale):
    y = x * lax.rsqrt(jnp.mean(x * x, axis=-1, keepdims=True) + EPS) * gain
    return y * (1.0 + scale) + shift


MXU_N = 256


def _group_of(lane):
    return (lane >> 6) & 1


def _swap_halves(a):
    lane = lax.broadcasted_iota(I32, a.shape, 1)
    return jnp.where((lane & 32) == 0, pltpu.roll(a, 96, 1), pltpu.roll(a, 32, 1))


def _inproj_kernel(x_ref, mod_ref, g_ref, w_ref, ct_ref, st_ref, o_ref, *, d, rope):
    m = mod_ref[0]
    h = _rmsnorm_mod(x_ref[...], g_ref[...], m[:, 0:d], m[:, d:2 * d]).astype(BF16)
    per = MXU_N // LANES
    for s in range(w_ref.shape[2] // MXU_N):
        acc = _dot(h, w_ref[0, :, s * MXU_N:(s + 1) * MXU_N])
        for t in range(per):
            a = acc[:, t * LANES:(t + 1) * LANES]
            if rope[s * per + t]:
                a = a * ct_ref[...] + _swap_halves(a) * st_ref[...]
            c0 = s * MXU_N + t * LANES
            o_ref[:, c0:c0 + LANES] = a.astype(BF16)


def in_projection(xa, mod, gain, w_bf, layer, ct, st, n_x_tiles, tiles_per_sample, n_samples, rope, tm):
    r, d = xa.shape
    n = w_bf.shape[2]
    mod_idx = lambda i: (jnp.where(i < n_x_tiles, i // tiles_per_sample, n_samples), 0, 0)
    return pl.pallas_call(
        functools.partial(_inproj_kernel, d=d, rope=rope),
        name="in_proj",
        grid=(r // tm,),
        in_specs=[pl.BlockSpec((tm, d), lambda i: (i, 0)),
                  pl.BlockSpec((1, 1, N_MOD * d), mod_idx),
                  pl.BlockSpec((1, d), lambda i: (0, 0)),
                  pl.BlockSpec((1, d, n), lambda i: (layer, 0, 0), pipeline_mode=pl.Buffered(1)),
                  pl.BlockSpec((tm, LANES), lambda i: (i, 0)),
                  pl.BlockSpec((tm, LANES), lambda i: (i, 0))],
        out_specs=pl.BlockSpec((tm, n), lambda i: (i, 0)),
        out_shape=jax.ShapeDtypeStruct((r, n), BF16),
        compiler_params=_params(("arbitrary",)),
    )(xa, mod, gain, w_bf, ct, st)


def _prep_w_in(w_in, slab_scale):
    col_scale = jnp.repeat(jnp.asarray(slab_scale, F32), LANES)
    return (w_in * col_scale).astype(BF16)


def _ret_kernel(lg_ref, qx_ref, kx_ref, vx_ref, gx_ref, qc_ref, kc_ref, vc_ref, gc_ref, gain_ref,
                *rest, lx, lc, ctx_out):
    if ctx_out:
        ox_ref, oc_ref, accx, accc = rest
    else:
        ox_ref, accx = rest
    pair = pl.program_id(1)
    w2 = 2 * V_DIM
    lane = lax.broadcasted_iota(I32, (CHUNK, LANES), 1)
    dif = (lax.broadcasted_iota(I32, (CHUNK, CHUNK), 0) - lax.broadcasted_iota(I32, (CHUNK, CHUNK), 1)).astype(F32)
    pos = lax.broadcasted_iota(I32, (CHUNK, 1), 0).astype(F32)
    cpos = lax.broadcasted_iota(I32, (lc, 1), 0).astype(F32)
    c_len = float(CHUNK)
    lgf = [lg_ref[0, 2 * pair + hh] for hh in range(2)]
    lgb = [lg_ref[1, 2 * pair + hh] for hh in range(2)]

    def per_head(fn):
        vals = [fn(hh) for hh in range(2)]
        return jnp.concatenate([jnp.broadcast_to(v, (v.shape[0], V_DIM)) for v in vals], axis=1)

    decay = per_head(lambda hh: jnp.where(dif >= 0, jnp.exp(lgf[hh] * jnp.maximum(dif, 0.0)),
                                          jnp.exp(lgb[hh] * jnp.maximum(-dif, 0.0))))
    qd_f = per_head(lambda hh: jnp.exp(lgf[hh] * (pos + 1.0)))
    kd_f = per_head(lambda hh: jnp.exp(lgf[hh] * (c_len - 1.0 - pos)))
    qd_b = per_head(lambda hh: jnp.exp(lgb[hh] * (c_len - pos)))
    kd_b = per_head(lambda hh: jnp.exp(lgb[hh] * pos))
    one = jnp.ones((1, 1), F32)
    cd_f = per_head(lambda hh: jnp.exp(lgf[hh] * c_len) * one)
    cd_b = per_head(lambda hh: jnp.exp(lgb[hh] * c_len) * one)
    srow = lax.broadcasted_iota(I32, (LANES, w2), 0)
    scol = lax.broadcasted_iota(I32, (LANES, w2), 1)
    diag = (_group_of(srow) == 0) == (scol < V_DIM)
    zeros_v = jnp.zeros((CHUNK, V_DIM), BF16)

    def kv_state(k, v_scaled):
        return jnp.where(diag, _dot(k.astype(F32).T.astype(BF16), v_scaled.astype(BF16)), 0.0)

    kc = kc_ref[...]
    vc = vc_ref[...].astype(F32)
    s_f = kv_state(kc, vc * per_head(lambda hh: jnp.exp(lgf[hh] * (lc - 1.0 - cpos))))
    s_b = kv_state(kc, vc * per_head(lambda hh: jnp.exp(lgb[hh] * cpos)))

    def sweeps(q_ref, k_ref, v_ref, g_ref, o_ref, acc_ref, n, s_f0, s_b0):
        def rows_of(c):
            return pl.ds(pl.multiple_of(c * CHUNK, CHUNK), CHUNK)

        def forward(c, s):
            rows = rows_of(c)
            q, k, v = q_ref[rows, :], k_ref[rows, :], v_ref[rows, :]
            k_bd = jnp.concatenate([jnp.where(_group_of(lane) == 0, k, 0), jnp.where(_group_of(lane) == 1, k, 0)],
                                   axis=0)
            v_bd = jnp.concatenate([jnp.concatenate([v[:, :V_DIM], zeros_v], axis=1),
                                    jnp.concatenate([zeros_v, v[:, V_DIM:]], axis=1)], axis=0)
            sc = _dot_nt(q, k_bd) * decay
            acc_ref[rows, :] = _dot(sc.astype(BF16), v_bd) + _dot(q, s.astype(BF16)) * qd_f
            return s * cd_f + kv_state(k, v.astype(F32) * kd_f)

        def backward(t, s):
            rows = rows_of(n - 1 - t)
            q, k, v = q_ref[rows, :], k_ref[rows, :], v_ref[rows, :]
            y = acc_ref[rows, :] + _dot(q, s.astype(BF16)) * qd_b
            for hh in range(2):
                vsl = slice(hh * V_DIM, (hh + 1) * V_DIM)
                yh = y[:, vsl]
                yc = yh - jnp.mean(yh, axis=-1, keepdims=True)
                out = yc * lax.rsqrt(jnp.mean(yc * yc, axis=-1, keepdims=True) + EPS) * gain_ref[:, vsl]
                o_ref[rows, vsl] = (_silu(g_ref[rows, vsl].astype(F32)) * out).astype(BF16)
            return s * cd_b + kv_state(k, v.astype(F32) * kd_b)

        unroll = 4 if n % 4 == 0 else n
        lax.fori_loop(0, n, forward, s_f0, unroll=unroll)
        lax.fori_loop(0, n, backward, s_b0, unroll=unroll)

    sweeps(qx_ref, kx_ref, vx_ref, gx_ref, ox_ref, accx, lx // CHUNK, s_f, s_b)
    if ctx_out:
        zero = jnp.zeros((LANES, w2), F32)
        sweeps(qc_ref, kc_ref, vc_ref, gc_ref, oc_ref, accc, lc // CHUNK, zero, zero)


def retention(p, lg, gain, n_samples, lx, lc, heads, ctx_out):
    pairs = heads // 2
    half = heads // 2
    cb = n_samples * lx // lc
    w2 = 2 * V_DIM
    in_specs = [
        pl.BlockSpec(memory_space=pltpu.SMEM),
        pl.BlockSpec((lx, LANES), lambda b, h: (b, h)),
        pl.BlockSpec((lx, LANES), lambda b, h: (b, half + h)),
        pl.BlockSpec((lx, w2), lambda b, h: (b, half + h)),
        pl.BlockSpec((lx, w2), lambda b, h: (b, heads + h)),
        pl.BlockSpec((lc, LANES), lambda b, h: (cb + b, h)),
        pl.BlockSpec((lc, LANES), lambda b, h: (cb + b, half + h)),
        pl.BlockSpec((lc, w2), lambda b, h: (cb + b, half + h)),
        pl.BlockSpec((lc, w2), lambda b, h: (cb + b, heads + h)),
        pl.BlockSpec((1, w2), lambda b, h: (0, h)),
    ]
    out_specs = [pl.BlockSpec((lx, w2), lambda b, h: (b, h))]
    out_shape = [jax.ShapeDtypeStruct((n_samples * lx, heads * V_DIM), BF16)]
    scratch = [pltpu.VMEM((lx, w2), F32)]
    if ctx_out:
        out_specs.append(pl.BlockSpec((lc, w2), lambda b, h: (b, h)))
        out_shape.append(jax.ShapeDtypeStruct((n_samples * lc, heads * V_DIM), BF16))
        scratch.append(pltpu.VMEM((lc, w2), F32))
    return pl.pallas_call(
        functools.partial(_ret_kernel, lx=lx, lc=lc, ctx_out=ctx_out),
        name="retention",
        grid=(n_samples, pairs),
        in_specs=in_specs, out_specs=out_specs, out_shape=out_shape, scratch_shapes=scratch,
        compiler_params=_params(("arbitrary", "arbitrary")),
    )(lg, p, p, p, p, p, p, p, p, gain)


def _diff_kernel(lam_ref, q_ref, kc_ref, vc_ref, *rest, lam_init, has_x, kb):
    if has_x:
        kx_ref, vx_ref, gain_ref, o_ref = rest
    else:
        gain_ref, o_ref = rest
    lv = lam_ref[...]
    lam = (jnp.exp(jnp.sum(lv[0:1] * lv[1:2], axis=-1, keepdims=True))
           - jnp.exp(jnp.sum(lv[2:3] * lv[3:4], axis=-1, keepdims=True)) + lam_init)
    q = q_ref[...]
    tq = q.shape[0]
    lane = lax.broadcasted_iota(I32, q.shape, 1)
    blocks = [(kc_ref, vc_ref, 0, kc_ref.shape[0])]
    if has_x:
        blocks += [(kx_ref, vx_ref, r0, kb) for r0 in range(0, kx_ref.shape[0], kb)]
    v_ext = [jnp.concatenate([v_ref[r0:r0 + n, :], jnp.ones((n, V_DIM), BF16)], axis=1) for _, v_ref, r0, n in blocks]
    heads_out = []
    for t in range(2):
        qm = jnp.where(_group_of(lane) == t, q, 0)
        m = jnp.full((tq, 1), -jnp.inf, F32)
        acc = jnp.zeros((tq, 2 * V_DIM), F32)
        for (k_ref, _, r0, n), ve in zip(blocks, v_ext):
            s = _dot_nt(qm, k_ref[r0:r0 + n, :])
            m_new = jnp.maximum(m, jnp.max(s, axis=-1, keepdims=True))
            acc = acc * jnp.exp2(m - m_new) + _dot(jnp.exp2(s - m_new).astype(BF16), ve)
            m = m_new
        heads_out.append(acc[:, :V_DIM] * (1.0 / acc[:, V_DIM:V_DIM + 1]))
    out = heads_out[0] - lam * heads_out[1]
    y = out * lax.rsqrt(jnp.mean(out * out, axis=-1, keepdims=True) + EPS) * gain_ref[...]
    o_ref[...] = (y * (1.0 - lam_init)).astype(BF16)


def diff_attention(p, lam_vec, gain, n_samples, lx, lc, ret_heads, heads, lam_init, latent, tq):
    qo = 3 * ret_heads
    ko = qo + heads
    vo = ko + heads
    cb = n_samples * lx // lc
    lq = lx if latent else lc
    nq = lq // tq
    q_row = (lambda b, t: b * nq + t) if latent else (lambda b, t: n_samples * lx // tq + b * nq + t)
    in_specs = [
        pl.BlockSpec((4, QK_DIM), lambda b, h, t: (0, 0)),
        pl.BlockSpec((tq, LANES), lambda b, h, t: (q_row(b, t), qo + h)),
        pl.BlockSpec((lc, LANES), lambda b, h, t: (cb + b, ko + h)),
        pl.BlockSpec((lc, LANES), lambda b, h, t: (cb + b, vo + h)),
    ]
    args = [lam_vec, p, p, p]
    if latent:
        in_specs += [pl.BlockSpec((lx, LANES), lambda b, h, t: (b, ko + h)),
                     pl.BlockSpec((lx, LANES), lambda b, h, t: (b, vo + h))]
        args += [p, p]
    in_specs.append(pl.BlockSpec((1, LANES), lambda b, h, t: (0, h)))
    args.append(gain)
    return pl.pallas_call(
        functools.partial(_diff_kernel, lam_init=lam_init, has_x=latent, kb=min(1024, lx)),
        name="diff_attn_x" if latent else "diff_attn_ctx",
        grid=(n_samples, heads, nq),
        in_specs=in_specs,
        out_specs=pl.BlockSpec((tq, LANES), lambda b, h, t: (b * nq + t, h)),
        out_shape=jax.ShapeDtypeStruct((n_samples * lq, heads * V_DIM), BF16),
        compiler_params=_params(("arbitrary", "arbitrary", "arbitrary")),
    )(*args)


def _pack_pair(a, b):
    ua = pltpu.bitcast(a.astype(BF16).astype(F32), I32)
    ub = pltpu.bitcast(b.astype(BF16).astype(F32), I32)
    return ua | lax.shift_right_logical(ub, 16)


def _unpack_pair(w):
    return (pltpu.bitcast(w & -65536, F32).astype(BF16), pltpu.bitcast(w << 16, F32).astype(BF16))


def _store_packed_rows(ref, row0, n_rows, val):
    nw = val.shape[1] // (2 * LANES)
    for j in range(nw):
        w = _pack_pair(val[:, j * LANES:(j + 1) * LANES], val[:, (nw + j) * LANES:(nw + j + 1) * LANES])
        ref[pl.ds(row0 * nw + j, n_rows, stride=nw), :] = w


def _load_packed_rows(ref, row0, n_rows, nw, dst, dst_row0):
    for j in range(nw):
        hi, lo = _unpack_pair(ref[pl.ds(row0 * nw + j, n_rows, stride=nw), :])
        dst[dst_row0:dst_row0 + n_rows, j * LANES:(j + 1) * LANES] = hi
        dst[dst_row0:dst_row0 + n_rows, (nw + j) * LANES:(nw + j + 1) * LANES] = lo


def _outproj_kernel(*refs, d, wr, n_x_tiles, with_ctx):
    if with_ctx:
        retx_ref, difx_ref, retc_ref, difc_ref = refs[:4]
        is_ctx = pl.program_id(0) >= n_x_tiles
        ret = jnp.where(is_ctx, retc_ref[...], retx_ref[...])
        dif = jnp.where(is_ctx, difc_ref[...], difx_ref[...])
        refs = refs[4:]
    else:
        ret, dif = refs[0][...], refs[1][...]
        refs = refs[2:]
    w_ref, x_ref, mod_ref, g_ref, wr_ref, xm_ref, h_ref, aff_ref = refs
    y = _dot(ret, w_ref[0:wr, :]) + _dot(dif, w_ref[wr:, :])
    m = mod_ref[0]
    xm = x_ref[...] + m[:, 2 * d:3 * d] * y
    xm_ref[...] = xm
    h = _rmsnorm_mod(xm, g_ref[...], m[:, 3 * d:4 * d], m[:, 4 * d:5 * d])
    _store_packed_rows(h_ref, 0, h.shape[0], h)
    n_e = wr_ref.shape[1] // 2
    h_hi = h.astype(BF16)
    h_lo = (h - h_hi.astype(F32)).astype(BF16)
    both = _dot(h_hi, wr_ref[...])
    logits = both[:, :n_e] + both[:, n_e:] + _dot(h_lo, wr_ref[...])[:, :n_e]
    e = jnp.exp(logits - jnp.max(logits, axis=-1, keepdims=True))
    aff_ref[...] = e / jnp.sum(e, axis=-1, keepdims=True)


def out_projection(mix_x, mix_c, w_bf, xa, mod, gain, w_router, rows, n_x_tiles, tiles_per_sample, n_samples, tm):
    d = xa.shape[1]
    wr = mix_x[0].shape[1]
    wd = mix_x[1].shape[1]
    e = w_router.shape[1]
    nw = d // (2 * LANES)
    wr_hi = w_router.astype(BF16)
    wr_split = jnp.concatenate([wr_hi, (w_router - wr_hi.astype(F32)).astype(BF16)], axis=1)
    mod_idx = lambda i: (jnp.where(i < n_x_tiles, i // tiles_per_sample, n_samples), 0, 0)
    x_idx = lambda i: (jnp.minimum(i, n_x_tiles - 1), 0)
    c_idx = lambda i: (jnp.maximum(i - n_x_tiles, 0), 0)
    mix_specs = [pl.BlockSpec((tm, wr), x_idx), pl.BlockSpec((tm, wd), x_idx)]
    mix_args = list(mix_x)
    if mix_c is not None:
        mix_specs += [pl.BlockSpec((tm, wr), c_idx), pl.BlockSpec((tm, wd), c_idx)]
        mix_args += list(mix_c)
    return pl.pallas_call(
        functools.partial(_outproj_kernel, d=d, wr=wr, n_x_tiles=n_x_tiles, with_ctx=mix_c is not None),
        name="out_proj_router",
        grid=(rows // tm,),
        in_specs=mix_specs + [
                  pl.BlockSpec(w_bf.shape, lambda i: (0, 0), pipeline_mode=pl.Buffered(1)),
                  pl.BlockSpec((tm, d), lambda i: (i, 0)),
                  pl.BlockSpec((1, 1, N_MOD * d), mod_idx),
                  pl.BlockSpec((1, d), lambda i: (0, 0)),
                  pl.BlockSpec((d, 2 * e), lambda i: (0, 0))],
        out_specs=[pl.BlockSpec((tm, d), lambda i: (i, 0)),
                   pl.BlockSpec((tm * nw, LANES), lambda i: (i, 0)),
                   pl.BlockSpec((tm, e), lambda i: (i, 0))],
        out_shape=[jax.ShapeDtypeStruct((rows, d), F32),
                   jax.ShapeDtypeStruct((rows * nw, LANES), I32),
                   jax.ShapeDtypeStruct((rows, e), F32)],
        compiler_params=_params(("arbitrary",)),
    )(*mix_args, w_bf, xa, mod, gain, wr_split)


def _ones(mask, dtype):
    return jnp.where(mask, 1.0, 0.0).astype(dtype)


def _prefix_lanes(m):
    upper = _ones(lax.broadcasted_iota(I32, (LANES, LANES), 0) < lax.broadcasted_iota(I32, (LANES, LANES), 1), BF16)
    run = jnp.zeros((m.shape[0], 1), F32)
    outs = []
    for s in range(m.shape[1] // LANES):
        seg = m[:, s * LANES:(s + 1) * LANES]
        outs.append(_dot(seg, upper) + run)
        run = run + jnp.sum(seg.astype(F32), axis=1, keepdims=True)
    return jnp.concatenate(outs, axis=1)


DIRECT_BLOCKS = 8


def _topk_kernel(a_ref, idx_ref, gate_ref, dest_ref, start_ref, cnt_ref, *scr, lg, cap, row_base, ent_base):
    g = pl.program_id(0)
    a = a_ref[...]
    n_e = a.shape[0]
    bits = pltpu.bitcast(a, I32)

    def bisect(_, lohi):
        lo, hi = lohi
        mid = lo + ((hi - lo) >> 1)
        ge = jnp.sum(_ones(bits >= mid, F32), axis=1, keepdims=True) >= cap
        return jnp.where(ge, mid, lo), jnp.where(ge, hi, mid)

    thr, _ = lax.fori_loop(0, 31, bisect,
                           (jnp.zeros((n_e, 1), I32), jnp.full((n_e, 1), 0x7F800000, I32)))
    gt = bits > thr
    eq = bits == thr
    need = cap - jnp.sum(_ones(gt, F32), axis=1, keepdims=True)
    sel = gt | (eq & (_prefix_lanes(_ones(eq, BF16)) < need))
    selb = _ones(sel, BF16)
    pos = _prefix_lanes(selb)
    start_t = jnp.sum(pos, axis=0, keepdims=True)
    lower = _ones(lax.broadcasted_iota(I32, (n_e, n_e), 0) > lax.broadcasted_iota(I32, (n_e, n_e), 1), BF16)
    base = (ent_base + g * (n_e * cap)).astype(F32)
    start_ref[...] = (start_t + base).astype(I32)
    cnt_ref[...] = jnp.sum(_ones(sel, F32), axis=0, keepdims=True).astype(I32)
    posm = jnp.where(sel, pos, -1.0)
    dest = start_t + base + _dot(lower, selb)
    tok_base = (row_base + g * lg).astype(F32)
    slot = lax.broadcasted_iota(I32, (cap, 1), 0).astype(F32)
    col = lax.broadcasted_iota(I32, (cap, n_e), 1)
    z = jnp.zeros((cap, n_e), F32)

    def emit(ia, ga, da):
        idx_ref[0] = ia.astype(I32)
        gate_ref[0] = ga
        dest_ref[0] = da.astype(I32)

    nb = lg // LANES
    if nb <= DIRECT_BLOCKS:
        (row_scr,) = scr
        row_scr[0], row_scr[1], row_scr[2] = posm, a, dest
        tvals = tok_base + lax.broadcasted_iota(I32, (1, lg), 1).astype(F32)

        def compact_direct(e, carry):
            ia, ga, da = carry
            m = row_scr[0, pl.ds(e, 1), :] == slot
            i_e = jnp.sum(jnp.where(m, tvals, 0.0), axis=1, keepdims=True)
            g_e = jnp.sum(jnp.where(m, row_scr[1, pl.ds(e, 1), :], 0.0), axis=1, keepdims=True)
            d_e = jnp.sum(jnp.where(m, row_scr[2, pl.ds(e, 1), :], 0.0), axis=1, keepdims=True)
            here = col == e
            return jnp.where(here, i_e, ia), jnp.where(here, g_e, ga), jnp.where(here, d_e, da)

        emit(*lax.fori_loop(0, n_e, compact_direct, (z, z, z)))
        return

    blk_scr, first_scr = scr
    nbp = blk_scr.shape[1] // n_e
    if nbp != nb:
        blk_scr[...] = jnp.zeros_like(blk_scr)
    for k in range(nb):
        blk = slice(k * LANES, (k + 1) * LANES)
        pieces = []
        for x in (posm[:, blk], a[:, blk], dest[:, blk]):
            x1 = x.astype(BF16).astype(F32)
            x2 = (x - x1).astype(BF16).astype(F32)
            pieces += [x1, x2, (x - x1 - x2).astype(BF16).astype(F32)]
        for j in range(9):
            blk_scr[j, pl.ds(k, n_e, stride=nbp), :] = pieces[j]
    before = _ones(lax.broadcasted_iota(I32, (lg, nb), 0) < LANES * lax.broadcasted_iota(I32, (lg, nb), 1), BF16)
    first_scr[...] = _dot(selb, before)

    blane = lax.broadcasted_iota(I32, (cap, nbp), 1).astype(F32)
    lane_f = lax.broadcasted_iota(I32, (cap, LANES), 1).astype(F32)
    w3 = 3 * LANES

    def compact(e, carry):
        ia, ga, da = carry
        kb = jnp.sum(_ones(first_scr[pl.ds(e, 1), :] <= slot, F32), axis=1, keepdims=True) - 1.0
        onehot = _ones(blane == kb, BF16)
        own = pl.ds(pl.multiple_of(e * nbp, nbp), nbp)
        rows = jnp.concatenate([blk_scr[j, own, :] for j in range(9)], axis=1).astype(BF16)
        parts = _dot(onehot, rows)
        pos_g, aff_g, dest_g = [parts[:, q * w3:q * w3 + LANES] + parts[:, q * w3 + LANES:q * w3 + 2 * LANES]
                                + parts[:, q * w3 + 2 * LANES:(q + 1) * w3] for q in range(3)]
        m = pos_g == slot
        i_e = jnp.sum(jnp.where(m, lane_f, 0.0), axis=1, keepdims=True) + kb * LANES + tok_base
        g_e = jnp.sum(jnp.where(m, aff_g, 0.0), axis=1, keepdims=True)
        d_e = jnp.sum(jnp.where(m, dest_g, 0.0), axis=1, keepdims=True)
        here = col == e
        return jnp.where(here, i_e, ia), jnp.where(here, g_e, ga), jnp.where(here, d_e, da)

    emit(*lax.fori_loop(0, n_e, compact, (z, z, z)))


def expert_choice(aff_t, groups, lg, first_block, cap, row_base, ent_base):
    e = aff_t.shape[0]
    return pl.pallas_call(
        functools.partial(_topk_kernel, lg=lg, cap=cap, row_base=row_base, ent_base=ent_base),
        name="expert_choice",
        grid=(groups,),
        in_specs=[pl.BlockSpec((e, lg), lambda g: (0, first_block + g))],
        out_specs=[pl.BlockSpec((1, cap, e), lambda g: (g, 0, 0)),
                   pl.BlockSpec((1, cap, e), lambda g: (g, 0, 0)),
                   pl.BlockSpec((1, cap, e), lambda g: (g, 0, 0)),
                   pl.BlockSpec((1, lg), lambda g: (0, g)),
                   pl.BlockSpec((1, lg), lambda g: (0, g))],
        out_shape=[jax.ShapeDtypeStruct((groups, cap, e), I32),
                   jax.ShapeDtypeStruct((groups, cap, e), F32),
                   jax.ShapeDtypeStruct((groups, cap, e), I32),
                   jax.ShapeDtypeStruct((1, groups * lg), I32),
                   jax.ShapeDtypeStruct((1, groups * lg), I32)],
        scratch_shapes=([pltpu.VMEM((3, e, lg), F32)] if lg // LANES <= DIRECT_BLOCKS else
                        [pltpu.VMEM((9, e * (-(-(lg // LANES) // 8) * 8), LANES), F32),
                         pltpu.VMEM((e, lg // LANES), F32)]),
        compiler_params=_params(("arbitrary",)),
    )(aff_t)


def _ffn_kernel(idx_ref, dest_ref, h_ref, wg_ref, wu_ref, wd_ref, gate_ref, ys_ref,
                xs_buf, ys_buf, xb_scr, hid_scr, wd_scr, gsem, ssem, *, ts, nw, tf):
    e = pl.program_id(0)
    f = pl.program_id(1)
    n_e = pl.num_programs(0)
    last_f = pl.num_programs(1) - 1
    s = xb_scr.shape[0]

    n_f = hid_scr.shape[0]
    per_step = s // n_f
    slot = e % 2

    def gather_start(expert, into, row):
        src = pl.multiple_of(idx_ref[expert * s + row] * nw, nw)
        pltpu.make_async_copy(h_ref.at[pl.ds(src, nw)],
                              xs_buf.at[into, pl.ds(pl.multiple_of(row * nw, nw), nw)], gsem.at[into]).start()

    def wait_gather(into):
        pltpu.make_async_copy(h_ref.at[pl.ds(0, s * nw)], xs_buf.at[into], gsem.at[into]).wait()

    def scatter_start(row):
        dst = pl.multiple_of(dest_ref[e * s + row] * nw, nw)
        pltpu.make_async_copy(ys_buf.at[pl.ds(row * nw, nw)], ys_ref.at[pl.ds(dst, nw)], ssem).start(priority=row % 2)

    def wait_scatter():
        pltpu.make_async_copy(ys_buf, ys_ref.at[pl.ds(0, s * nw)], ssem).wait()

    @pl.when(f == 0)
    def _():
        @pl.when(e == 0)
        def _():
            def body(i, carry):
                gather_start(0, 0, i)
                return carry
            lax.fori_loop(0, s, body, 0, unroll=8)

        wait_gather(slot)
        _load_packed_rows(xs_buf.at[slot], 0, s, nw, xb_scr, 0)

    nxt = jnp.minimum(e + 1, n_e - 1)
    for i in range(per_step):
        gather_start(nxt, 1 - slot, f * per_step + i)

    wg = wg_ref[0, 0].astype(BF16)
    wu = wu_ref[0, 0].astype(BF16)
    for r in range(s // ts):
        rows = slice(r * ts, (r + 1) * ts)
        xb = xb_scr[rows, :]
        hid_scr[f, rows, :] = (_silu(_dot(xb, wg)) * _dot(xb, wu)).astype(BF16)
    wd_scr[pl.ds(pl.multiple_of(f * tf, tf), tf), :] = wd_ref[0, 0].astype(BF16)

    @pl.when(f == last_f)
    def _():
        @pl.when(e > 0)
        def _():
            wait_scatter()

        for r in range(s // ts):
            rows = slice(r * ts, (r + 1) * ts)
            hid = jnp.concatenate([hid_scr[c, rows, :] for c in range(n_f)], axis=1)
            y = _dot(hid, wd_scr[...])
            _store_packed_rows(ys_buf, r * ts, ts, y * gate_ref[0, rows, :])
            for i in range(ts):
                scatter_start(r * ts + i)

        @pl.when(e == n_e - 1)
        def _():
            wait_scatter()
            wait_gather(1 - slot)


def expert_ffn(h_packed, idx, dest, w_gate, w_up, w_down, layer, gates, d):
    nw = d // (2 * LANES)
    e, s = gates.shape[0], gates.shape[1]
    ff = w_gate.shape[3]
    tf = 256 if ff % 256 == 0 else ff
    ts = max(t for t in range(8, 577, 8) if s % t == 0)
    assert h_packed.shape[0] >= s * nw and s % (ff // tf) == 0
    return pl.pallas_call(
        functools.partial(_ffn_kernel, ts=ts, nw=nw, tf=tf),
        name="expert_ffn",
        grid_spec=pltpu.PrefetchScalarGridSpec(
            num_scalar_prefetch=2, grid=(e, ff // tf),
            in_specs=[pl.BlockSpec(memory_space=pl.ANY),
                      pl.BlockSpec((1, 1, d, tf), lambda i, f, si, di: (layer, i, 0, f)),
                      pl.BlockSpec((1, 1, d, tf), lambda i, f, si, di: (layer, i, 0, f)),
                      pl.BlockSpec((1, 1, tf, d), lambda i, f, si, di: (layer, i, f, 0)),
                      pl.BlockSpec((1, s, 1), lambda i, f, si, di: (i, 0, 0))],
            out_specs=pl.BlockSpec(memory_space=pl.ANY),
            scratch_shapes=[pltpu.VMEM((2, s * nw, LANES), I32), pltpu.VMEM((s * nw, LANES), I32),
                            pltpu.VMEM((s, d), BF16), pltpu.VMEM((ff // tf, s, tf), BF16),
                            pltpu.VMEM((ff, d), BF16),
                            pltpu.SemaphoreType.DMA((2,)), pltpu.SemaphoreType.DMA(())]),
        out_shape=jax.ShapeDtypeStruct((e * s * nw, LANES), I32),
        compiler_params=_params(("arbitrary", "arbitrary")),
    )(idx, dest, h_packed, w_gate, w_up, w_down, gates)


COMB_ROWS = 256
COMB_ENT = 256


def _combine_kernel(kb_ref, ch_ref, flag_ref, ys_ref, xm_ref, start_ref, cnt_ref, mod_ref, *rest, d, final):
    if final:
        gf_ref, o_ref, chunk_scr = rest
    else:
        o_ref, chunk_scr = rest
    p = pl.program_id(0)
    flags = flag_ref[p]

    @pl.when((flags & 2) != 0)
    def _():
        o_ref[...] = jnp.zeros_like(o_ref)

    @pl.when((flags & 8) != 0)
    def _():
        _load_packed_rows(ys_ref, 0, COMB_ENT, d // (2 * LANES), chunk_scr, 0)

    @pl.when((flags & 1) != 0)
    def _():
        gpos = ch_ref[p] * COMB_ENT + lax.broadcasted_iota(I32, (COMB_ROWS, COMB_ENT), 1)
        st = start_ref[...]
        onehot = _ones((gpos >= st) & (gpos < st + cnt_ref[...]), BF16)
        o_ref[...] += _dot(onehot, chunk_scr[...])

    @pl.when((flags & 4) != 0)
    def _():
        m = mod_ref[0]
        out = xm_ref[...] + m[:, 5 * d:6 * d] * o_ref[...]
        if final:
            out = out * lax.rsqrt(jnp.mean(out * out, axis=-1, keepdims=True) + EPS) * gf_ref[...]
        o_ref[...] = out


def combine(ys_sorted, xm, start_col, cnt_col, mod, kb, ch, flags, rows, tiles_per_sample, n_x_tiles,
            n_samples, final_gain):
    d = xm.shape[1]
    nw = d // (2 * LANES)
    n_pairs = kb.shape[0]
    mod_idx = lambda p, kb, ch, fl: (jnp.where(kb[p] < n_x_tiles, kb[p] // tiles_per_sample, n_samples), 0, 0)
    in_specs = [pl.BlockSpec((COMB_ENT * nw, LANES), lambda p, kb, ch, fl: (ch[p], 0)),
                pl.BlockSpec((COMB_ROWS, d), lambda p, kb, ch, fl: (kb[p], 0)),
                pl.BlockSpec((COMB_ROWS, 1), lambda p, kb, ch, fl: (kb[p], 0)),
                pl.BlockSpec((COMB_ROWS, 1), lambda p, kb, ch, fl: (kb[p], 0)),
                pl.BlockSpec((1, 1, N_MOD * d), mod_idx)]
    args = [ys_sorted, xm, start_col, cnt_col, mod]
    if final_gain is not None:
        in_specs.append(pl.BlockSpec((1, d), lambda p, kb, ch, fl: (0, 0)))
        args.append(final_gain)
    return pl.pallas_call(
        functools.partial(_combine_kernel, d=d, final=final_gain is not None),
        name="combine",
        grid_spec=pltpu.PrefetchScalarGridSpec(
            num_scalar_prefetch=3, grid=(n_pairs,),
            in_specs=in_specs,
            out_specs=pl.BlockSpec((COMB_ROWS, d), lambda p, kb, ch, fl: (kb[p], 0)),
            scratch_shapes=[pltpu.VMEM((COMB_ENT, d), BF16)]),
        out_shape=jax.ShapeDtypeStruct((rows, d), F32),
        compiler_params=_params(("arbitrary",)),
    )(kb, ch, flags, *args)


def _combine_schedule(start_row, n_blocks, n_chunks):
    blk_start = start_row[::COMB_ROWS]
    blk_end = jnp.concatenate([blk_start[1:], jnp.full((1,), n_chunks * COMB_ENT, I32)])
    lo = jnp.minimum(blk_start // COMB_ENT, n_chunks - 1)
    hi = jnp.maximum(lo, (blk_end - 1) // COMB_ENT)
    cnt = hi - lo + 1
    off = jnp.cumsum(cnt) - cnt
    total = off[-1] + cnt[-1]
    n_pairs = n_blocks + n_chunks
    pidx = jnp.arange(n_pairs, dtype=I32)
    kb = jnp.sum((off[None, :] <= pidx[:, None]).astype(I32), axis=1) - 1
    valid = pidx < total
    ch = jnp.where(valid, lo[kb] + pidx - off[kb], hi[n_blocks - 1]).astype(I32)
    first = valid & (pidx == off[kb])
    last = valid & (pidx == off[kb] + cnt[kb] - 1)
    fresh = jnp.concatenate([jnp.ones((1,), bool), ch[1:] != ch[:-1]])
    flags = valid.astype(I32) + 2 * first.astype(I32) + 4 * last.astype(I32) + 8 * fresh.astype(I32)
    return kb, ch, flags


def _rope_tables(lx, n_ctx_rows, n_samples):
    n_freq = QK_DIM // 4
    freqs = ROPE_BASE ** (-jnp.arange(n_freq, dtype=F32) / n_freq)
    rows = lx // GRID_W
    row = jnp.repeat(jnp.arange(rows, dtype=F32), GRID_W)
    col = jnp.tile(jnp.arange(GRID_W, dtype=F32), rows)
    ang = jnp.concatenate([row[:, None] * freqs, col[:, None] * freqs], axis=-1)
    cos, sin = jnp.cos(ang), jnp.sin(ang)
    ct = jnp.tile(jnp.concatenate([cos, cos], axis=-1), (n_samples, LANES // QK_DIM))
    st = jnp.tile(jnp.concatenate([-sin, sin], axis=-1), (n_samples, LANES // QK_DIM))
    ct = jnp.concatenate([ct, jnp.ones((n_ctx_rows, LANES), F32)], axis=0)
    st = jnp.concatenate([st, jnp.zeros((n_ctx_rows, LANES), F32)], axis=0)
    return ct, st


def kernel(x, c, ctx, c_ctx, w_ada, b_ada, norm_mix, norm_ffn, w_in, w_out, ret_log_decay, ret_norm,
           diff_lambda, diff_norm, w_router, w_gate, w_up, w_down, norm_final):
    n_b, lx, d = x.shape
    lc = ctx.shape[1]
    depth = w_ada.shape[0]
    heads = ret_log_decay.shape[-1]
    dheads = diff_norm.shape[-1] // V_DIM
    n_e = w_router.shape[-1]
    rx, rc = n_b * lx, n_b * lc
    cap_x = CAPACITY_FACTOR * lx // n_e
    cap_c = CAPACITY_FACTOR * lc // n_e
    tm = 512 if (lx % 512 == 0 and rx % 512 == 0 and rc % 512 == 0) else 256
    tmo = tm
    assert lx % lc == 0 and rc % tmo == 0 and heads % 2 == 0 and n_b < MOD_ROWS
    assert lx % CHUNK == 0 and lc % CHUNK == 0 and lx % GRID_W == 0 and d % (2 * LANES) == 0
    assert lx % COMB_ROWS == 0 and rc % COMB_ROWS == 0
    assert (n_b * n_e * cap_x) % COMB_ENT == 0 and (n_b * n_e * cap_c) % COMB_ENT == 0

    qk_scale = QK_DIM ** -0.5
    n_rq = heads * QK_DIM // LANES
    rope = tuple([True] * (2 * n_rq) + [False] * (2 * heads) + [True] * (2 * dheads) + [False] * dheads)
    slab_scale = tuple([qk_scale] * n_rq + [1.0] * (n_rq + 2 * heads) + [qk_scale * math.log2(math.e)] * dheads
                       + [1.0] * (2 * dheads))
    w_in_bf = _prep_w_in(w_in, slab_scale)

    xa = jnp.concatenate([x.reshape(rx, d), ctx.reshape(rc, d)], axis=0)
    cc = jnp.concatenate([c, c_ctx[None], jnp.zeros((MOD_ROWS - n_b - 1, d), F32)], axis=0)
    mods = ada_modulation(cc, w_ada, b_ada).reshape(depth, MOD_ROWS, 1, N_MOD * d)
    ct, st = _rope_tables(lx, rc, n_b)
    log_g = jnp.log1p(-jnp.exp(ret_log_decay.astype(F32)))
    out = None

    for layer in range(depth):
        last = layer == depth - 1
        lam_init = 0.8 - 0.6 * math.exp(-0.3 * layer)
        mod = mods[layer]
        p = in_projection(xa, mod, norm_mix[layer][None], w_in_bf, layer, ct, st,
                          rx // tm, lx // tm, n_b, rope, tm)

        ret = retention(p, log_g[layer], ret_norm[layer][None], n_b, lx, lc, heads, not last)
        dif_x = diff_attention(p, diff_lambda[layer], diff_norm[layer][None], n_b, lx, lc, heads, dheads,
                               lam_init, True, 1024 if lx % 1024 == 0 else 256)
        if last:
            rows = rx
            mix_c = None
        else:
            rows = rx + rc
            dif_c = diff_attention(p, diff_lambda[layer], diff_norm[layer][None], n_b, lx, lc, heads, dheads,
                                   lam_init, False, lc)
            mix_c = (ret[1], dif_c)

        xm, h2, aff = out_projection((ret[0], dif_x), mix_c, w_out[layer].astype(BF16), xa, mod,
                                     norm_ffn[layer][None], w_router[layer], rows, rx // tmo, lx // tmo, n_b, tmo)
        aff_t = aff.T

        sel = [expert_choice(aff_t, n_b, lx, 0, cap_x, 0, 0)]
        if not last:
            sel.append(expert_choice(aff_t, n_b, lc, rx // lc, cap_c, rx, n_b * n_e * cap_x))
        to_es = lambda t: jnp.transpose(t, (2, 0, 1)).reshape(n_e, -1)
        idx = jnp.concatenate([to_es(s[0]) for s in sel], axis=1)
        gates = jnp.concatenate([to_es(s[1]) for s in sel], axis=1)
        dest = jnp.concatenate([to_es(s[2]) for s in sel], axis=1)
        start_row = jnp.concatenate([s[3][0] for s in sel])
        cnt_row = jnp.concatenate([s[4][0] for s in sel])
        slots = idx.shape[1]
        n_ent = n_e * slots

        ys_sorted = expert_ffn(h2, idx.reshape(-1), dest.reshape(-1), w_gate, w_up, w_down, layer,
                               gates.reshape(n_e, slots, 1), d)

        kb, ch, flags = _combine_schedule(start_row, rows // COMB_ROWS, n_ent // COMB_ENT)
        xa_new = combine(ys_sorted, xm, start_row[:, None], cnt_row[:, None], mod, kb, ch, flags, rows,
                         lx // COMB_ROWS, rx // COMB_ROWS, n_b, norm_final[None] if last else None)
        if last:
            out = xa_new
        else:
            xa = xa_new
    return out.reshape(n_b, lx, d)
```
